```python
import jax, jax.numpy as jnp
from jax import lax
import numpy as np

D_MODEL = 2048
BATCH = 4
SEQ = 8192
DEPTH = 2
DEC_BATCH = 4
DEC_SEQ = 2048
PAST_LEN = 128

A_HEADS = 6
A_HEAD_DIM = 128
A_WIDTH = 768
DILATED_CONFIGS = ((128, 1), (512, 4), (2048, 16))
B_HEADS = 6
Q_LORA = 512
KV_LORA = 512
QK_NOPE = 128
QK_ROPE = 64
V_HEAD = 128
B_WIDTH = 768
ROPE_THETA = 10000.0
MLA_QBLOCK = 128
C_WIDTH = 512
C_BLOCKS = 8
C_BLOCK_W = 64
CONV_WIDTH = 4
RG_C = 8.0
MIX_WIDTH = 2048
IN_WIDTH = 4416
IN_CUTS = (2304, 2816, 3328, 3392, 3904)
X_HEADS = 4
X_HEAD_DIM = 128
X_WIDTH = 512
N_MEM = 256
D_FF = 5632
EPS = 1e-6
NEG_INF = -1e30

kernel_name = "hybrid_bidir_dilated_mla_rglru_encoder"


def _rmsnorm(x, g):
    xf = x.astype(jnp.float32)
    y = xf * lax.rsqrt(jnp.mean(xf * xf, axis=-1, keepdims=True) + EPS)
    return (y * g.astype(jnp.float32)).astype(x.dtype)


def _swiglu(h, w_gate, w_up, w_down):
    return (jax.nn.silu(h @ w_gate) * (h @ w_up)) @ w_down


def _alibi_slopes(n):
    return 2.0 ** (-8.0 * jnp.arange(1, n + 1, dtype=jnp.float32) / n)


def _band_attention(q, k, v, slopes, dil, half):
    N, L, H, dh = q.shape
    blk = half
    nb = -(-L // blk)
    lp = nb * blk
    qb = jnp.pad(q, ((0, 0), (0, lp - L), (0, 0), (0, 0))).reshape(N, nb, blk, H, dh)
    kv_pad = ((0, 0), (half, lp - L + half), (0, 0), (0, 0))
    kb = jnp.pad(k, kv_pad).reshape(N, nb + 2, blk, H, dh)
    vb = jnp.pad(v, kv_pad).reshape(N, nb + 2, blk, H, dh)
    kw = jnp.concatenate([kb[:, :-2], kb[:, 1:-1], kb[:, 2:]], axis=2)
    vw = jnp.concatenate([vb[:, :-2], vb[:, 1:-1], vb[:, 2:]], axis=2)
    s = jnp.einsum('nbqhd,nbkhd->nbhqk', qb, kw, preferred_element_type=jnp.float32) * (dh ** -0.5)
    qpos = jnp.arange(lp).reshape(nb, blk)
    kpos = jnp.arange(nb)[:, None] * blk - half + jnp.arange(3 * blk)[None, :]
    rel = jnp.abs(qpos[:, :, None] - kpos[:, None, :])
    valid = (rel <= half) & (kpos[:, None, :] >= 0) & (kpos[:, None, :] < L)
    bias = -slopes[None, :, None, None] * (dil * rel).astype(jnp.float32)[:, None]
    s = jnp.where(valid[:, None], s + bias, NEG_INF)
    lse = jax.nn.logsumexp(s, axis=-1)
    p = jnp.exp(s - lse[..., None])
    o = jnp.einsum('nbhqk,nbkhd->nbqhd', p.astype(v.dtype), vw)
    o = o.reshape(N, lp, H, dh)[:, :L]
    lse = lse.transpose(0, 1, 3, 2).reshape(N, lp, H)[:, :L]
    return o, lse


def _dilated_mixture(q, k, v, slopes):
    B, S, H, dh = q.shape
    outs, lses = [], []
    for window, dil in DILATED_CONFIGS:
        half = (window // 2) // dil
        n_cls = S // dil
        def to_cls(t):
            return t.reshape(B, n_cls, dil, H, dh).transpose(0, 2, 1, 3, 4).reshape(B * dil, n_cls, H, dh)
        o, lse = _band_attention(to_cls(q), to_cls(k), to_cls(v), slopes, dil, half)
        outs.append(o.reshape(B, dil, n_cls, H, dh).transpose(0, 2, 1, 3, 4).reshape(B, S, H, dh))
        lses.append(lse.reshape(B, dil, n_cls, H).transpose(0, 2, 1, 3).reshape(B, S, H))
    w = jax.nn.softmax(jnp.stack(lses), axis=0)
    o = jnp.sum(w[..., None] * jnp.stack(outs).astype(jnp.float32), axis=0)
    return o.astype(q.dtype).reshape(B, S, H * dh)


def _rope_cos_sin(S):
    inv = ROPE_THETA ** (-jnp.arange(0, QK_ROPE, 2, dtype=jnp.float32) / QK_ROPE)
    ang = jnp.arange(S, dtype=jnp.float32)[:, None] * inv[None, :]
    return jnp.cos(ang), jnp.sin(ang)


def _apply_rope(x, cos, sin):
    xf = x.astype(jnp.float32)
    x1, x2 = jnp.split(xf, 2, axis=-1)
    return jnp.concatenate([x1 * cos - x2 * sin, x2 * cos + x1 * sin], axis=-1).astype(x.dtype)


def _mla(c_q, c_kv, k_rope, g_q_lat, w_q_up, g_kv_lat, w_kv_up):
    B, S, _ = c_q.shape
    q = (_rmsnorm(c_q, g_q_lat) @ w_q_up).reshape(B, S, B_HEADS, QK_NOPE + QK_ROPE)
    kv = (_rmsnorm(c_kv, g_kv_lat) @ w_kv_up).reshape(B, S, B_HEADS, QK_NOPE + V_HEAD)
    q_nope, q_rope = q[..., :QK_NOPE], q[..., QK_NOPE:]
    k_nope, v = kv[..., :QK_NOPE], kv[..., QK_NOPE:]
    cos, sin = _rope_cos_sin(S)
    q_rope = _apply_rope(q_rope, cos[:, None, :], sin[:, None, :])
    k_rope = _apply_rope(k_rope, cos, sin)
    scale = (QK_NOPE + QK_ROPE) ** -0.5
    nq = S // MLA_QBLOCK

    def block(args):
        qn, qr = args
        s = (jnp.einsum('bqhd,bkhd->bhqk', qn, k_nope, preferred_element_type=jnp.float32)
             + jnp.einsum('bqhr,bkr->bhqk', qr, k_rope, preferred_element_type=jnp.float32))
        p = jax.nn.softmax(s * scale, axis=-1)
        return jnp.einsum('bhqk,bkhd->bqhd', p.astype(v.dtype), v)

    qn_b = q_nope.reshape(B, nq, MLA_QBLOCK, B_HEADS, QK_NOPE).swapaxes(0, 1)
    qr_b = q_rope.reshape(B, nq, MLA_QBLOCK, B_HEADS, QK_ROPE).swapaxes(0, 1)
    o = lax.map(block, (qn_b, qr_b))
    return o.swapaxes(0, 1).reshape(B, S, B_WIDTH)


def _linear_scan(a, b, reverse):
    if reverse:
        a, b = jnp.flip(a, 1), jnp.flip(b, 1)

    def comb(l, r):
        return l[0] * r[0], r[0] * l[1] + r[1]

    h = lax.associative_scan(comb, (a, b), axis=1)[1]
    return jnp.flip(h, 1) if reverse else h


def _rglru_branch(u, gate, conv_w, conv_b, w_r, b_r, w_i, b_i, lam):
    B, S, C = u.shape
    u = lax.conv_general_dilated(
        u, conv_w[:, None, :], window_strides=(1,),
        padding=[(CONV_WIDTH // 2, CONV_WIDTH - 1 - CONV_WIDTH // 2)],
        dimension_numbers=('NWC', 'WIO', 'NWC'), feature_group_count=C) + conv_b
    ub = u.reshape(B, S, C_BLOCKS, C_BLOCK_W)
    uf = u.astype(jnp.float32)
    h = jnp.zeros_like(uf)
    for d, reverse in ((0, False), (1, True)):
        r = jax.nn.sigmoid((jnp.einsum('bsnc,ncd->bsnd', ub, w_r[d]).reshape(B, S, C) + b_r[d]).astype(jnp.float32))
        i = jax.nn.sigmoid((jnp.einsum('bsnc,ncd->bsnd', ub, w_i[d]).reshape(B, S, C) + b_i[d]).astype(jnp.float32))
        log_a = -RG_C * r * jax.nn.softplus(-lam[d].astype(jnp.float32))
        a = jnp.exp(log_a)
        b = jnp.sqrt(-jnp.expm1(2.0 * log_a)) * (i * uf)
        h = h + _linear_scan(a, b, reverse)
    return (jax.nn.gelu(gate.astype(jnp.float32)) * h).astype(u.dtype)


def _memory_xattn(h, mem, g_mem, w_xq, w_xk, w_xv, w_xo):
    B, S, _ = h.shape
    M = mem.shape[1]
    mn = _rmsnorm(mem, g_mem)
    q = (h @ w_xq).reshape(B, S, X_HEADS, X_HEAD_DIM)
    k = (mn @ w_xk).reshape(B, M, X_HEADS, X_HEAD_DIM)
    v = (mn @ w_xv).reshape(B, M, X_HEADS, X_HEAD_DIM)
    s = jnp.einsum('bqhd,bkhd->bhqk', q, k, preferred_element_type=jnp.float32) * (X_HEAD_DIM ** -0.5)
    p = jax.nn.softmax(s, axis=-1)
    o = jnp.einsum('bhqk,bkhd->bqhd', p.astype(v.dtype), v).reshape(B, S, X_WIDTH)
    return o @ w_xo


def _token_mixing(h, p, l):
    B, S, _ = h.shape
    proj = h @ p['w_in'][l]
    qkv_a, c_q, c_kv, k_rope, u_c, g_c = jnp.split(proj, list(IN_CUTS), axis=-1)
    qkv_a = qkv_a.reshape(B, S, 3, A_HEADS, A_HEAD_DIM)
    y_a = _dilated_mixture(qkv_a[:, :, 0], qkv_a[:, :, 1], qkv_a[:, :, 2], _alibi_slopes(A_HEADS))
    y_b = _mla(c_q, c_kv, k_rope, p['g_q_lat'][l], p['w_q_up'][l], p['g_kv_lat'][l], p['w_kv_up'][l])
    y_c = _rglru_branch(u_c, g_c, p['conv_w'][l], p['conv_b'][l], p['w_rg_r'][l], p['b_rg_r'][l],
                        p['w_rg_i'][l], p['b_rg_i'][l], p['rg_lambda'][l])
    y = jnp.concatenate([_rmsnorm(y_a, p['g_out_a'][l]), _rmsnorm(y_b, p['g_out_b'][l]),
                         _rmsnorm(y_c, p['g_out_c'][l])], axis=-1)
    return y @ p['w_out'][l]


def _trunk(x, mem, p):
    for l in range(DEPTH):
        h = _rmsnorm(x, p['g_ffn1'][l])
        x = x + 0.5 * _swiglu(h, p['w1_gate'][l], p['w1_up'][l], p['w1_down'][l])
        x = x + _token_mixing(_rmsnorm(x, p['g_mix'][l]), p, l)
        x = x + _memory_xattn(_rmsnorm(x, p['g_xattn'][l]), mem, p['g_mem'][l],
                              p['w_xq'][l], p['w_xk'][l], p['w_xv'][l], p['w_xo'][l])
        h = _rmsnorm(x, p['g_ffn2'][l])
        x = x + 0.5 * _swiglu(h, p['w2_gate'][l], p['w2_up'][l], p['w2_down'][l])
    return _rmsnorm(x, p['g_final'])


def setup_inputs(seed: int = 0) -> dict:
    key = jax.random.key(seed)
    ks = iter(jax.random.split(key, 48))
    f32 = jnp.float32

    def nrm(shape, scale):
        return jax.random.normal(next(ks), shape, f32) * scale

    def gain(shape):
        return 1.0 + 0.02 * jax.random.normal(next(ks), shape, f32)

    D, L2 = D_MODEL, DEPTH
    u = jax.random.uniform(next(ks), (L2, 2, C_WIDTH), f32, minval=0.9, maxval=0.999)
    a = u ** (1.0 / RG_C)
    rg_lambda = jnp.log(a) - jnp.log1p(-a)
    return {
        'x_prompt': nrm((BATCH, SEQ, D), 1.0),
        'x_sample': nrm((DEC_BATCH, DEC_SEQ, D), 1.0),
        'mem_prompt': nrm((BATCH, N_MEM, D), 1.0),
        'mem_sample': nrm((DEC_BATCH, N_MEM, D), 1.0),
        'g_ffn1': gain((L2, D)),
        'w1_gate': nrm((L2, D, D_FF), D ** -0.5),
        'w1_up': nrm((L2, D, D_FF), D ** -0.5),
        'w1_down': nrm((L2, D_FF, D), D_FF ** -0.5),
        'g_mix': gain((L2, D)),
        'w_in': nrm((L2, D, IN_WIDTH), D ** -0.5),
        'g_q_lat': gain((L2, Q_LORA)),
        'w_q_up': nrm((L2, Q_LORA, B_HEADS * (QK_NOPE + QK_ROPE)), Q_LORA ** -0.5),
        'g_kv_lat': gain((L2, KV_LORA)),
        'w_kv_up': nrm((L2, KV_LORA, B_HEADS * (QK_NOPE + V_HEAD)), KV_LORA ** -0.5),
        'conv_w': nrm((L2, CONV_WIDTH, C_WIDTH), CONV_WIDTH ** -0.5),
        'conv_b': nrm((L2, C_WIDTH), 0.01),
        'w_rg_r': nrm((L2, 2, C_BLOCKS, C_BLOCK_W, C_BLOCK_W), C_BLOCK_W ** -0.5),
        'b_rg_r': nrm((L2, 2, C_WIDTH), 0.01),
        'w_rg_i': nrm((L2, 2, C_BLOCKS, C_BLOCK_W, C_BLOCK_W), C_BLOCK_W ** -0.5),
        'b_rg_i': nrm((L2, 2, C_WIDTH), 0.01),
        'rg_lambda': rg_lambda,
        'g_out_a': gain((L2, A_WIDTH)),
        'g_out_b': gain((L2, B_WIDTH)),
        'g_out_c': gain((L2, C_WIDTH)),
        'w_out': nrm((L2, MIX_WIDTH, D), MIX_WIDTH ** -0.5),
        'g_xattn': gain((L2, D)),
        'g_mem': gain((L2, D)),
        'w_xq': nrm((L2, D, X_WIDTH), D ** -0.5),
        'w_xk': nrm((L2, D, X_WIDTH), D ** -0.5),
        'w_xv': nrm((L2, D, X_WIDTH), D ** -0.5),
        'w_xo': nrm((L2, X_WIDTH, D), X_WIDTH ** -0.5),
        'g_ffn2': gain((L2, D)),
        'w2_gate': nrm((L2, D, D_FF), D ** -0.5),
        'w2_up': nrm((L2, D, D_FF), D ** -0.5),
        'w2_down': nrm((L2, D_FF, D), D_FF ** -0.5),
        'g_final': gain((D,)),
    }


def reference(x_prompt, x_sample, mem_prompt, mem_sample,
              g_ffn1, w1_gate, w1_up, w1_down,
              g_mix, w_in, g_q_lat, w_q_up, g_kv_lat, w_kv_up,
              conv_w, conv_b, w_rg_r, b_rg_r, w_rg_i, b_rg_i, rg_lambda,
              g_out_a, g_out_b, g_out_c, w_out,
              g_xattn, g_mem, w_xq, w_xk, w_xv, w_xo,
              g_ffn2, w2_gate, w2_up, w2_down, g_final):
    p = dict(g_ffn1=g_ffn1, w1_gate=w1_gate, w1_up=w1_up, w1_down=w1_down,
             g_mix=g_mix, w_in=w_in, g_q_lat=g_q_lat, w_q_up=w_q_up, g_kv_lat=g_kv_lat, w_kv_up=w_kv_up,
             conv_w=conv_w, conv_b=conv_b, w_rg_r=w_rg_r, b_rg_r=b_rg_r, w_rg_i=w_rg_i, b_rg_i=b_rg_i,
             rg_lambda=rg_lambda, g_out_a=g_out_a, g_out_b=g_out_b, g_out_c=g_out_c, w_out=w_out,
             g_xattn=g_xattn, g_mem=g_mem, w_xq=w_xq, w_xk=w_xk, w_xv=w_xv, w_xo=w_xo,
             g_ffn2=g_ffn2, w2_gate=w2_gate, w2_up=w2_up, w2_down=w2_down, g_final=g_final)
    y_prompt = _trunk(x_prompt, mem_prompt, p)
    y_sample = _trunk(x_sample, mem_sample, p)
    return (y_prompt, y_sample)
```

```python
import functools
import math

import jax
import jax.numpy as jnp
from jax import lax
from jax.experimental import pallas as pl
from jax.experimental.pallas import tpu as pltpu

BF16 = jnp.bfloat16
F32 = jnp.float32

D_MODEL = 2048
A_HEADS = 6
HEAD = 128
A_WIDTH = A_HEADS * HEAD
DILATED_CONFIGS = ((128, 1), (512, 4), (2048, 16))
B_HEADS = 6
QK_ROPE = 64
ROPE_THETA = 10000.0
C_WIDTH = 512
C_BLOCKS = 8
C_BLOCK_W = 64
RG_C = 8.0
X_HEADS = 4
X_WIDTH = 512
D_FF = 5632
EPS = 1e-6
NEG_INF = -1e30

V7X_VMEM_LIMIT_BYTES = 56 * 1024 * 1024
LANES = 128
SUBLANES = 8

REST_WIDTH = 4 * 512 + LANES
BAND_HALF = 64


def _params(*sem):
    return pltpu.CompilerParams(dimension_semantics=sem, vmem_limit_bytes=V7X_VMEM_LIMIT_BYTES)


def _rms(x, g):
    return x * lax.rsqrt(jnp.mean(x * x, axis=-1, keepdims=True) + EPS) * g


def _dot(a, b):
    return jnp.dot(a, b, preferred_element_type=F32)


def _dot_nt(a, b):
    return lax.dot_general(a, b, (((1,), (1,)), ((), ())), preferred_element_type=F32)


def _ffn_kernel(x_ref, g_ref, wg_ref, wu_ref, wd_ref, gf_ref, o_ref, h_scr, *, final_norm):
    j = pl.program_id(1)

    @pl.when(j == 0)
    def _():
        h_scr[...] = _rms(x_ref[...], g_ref[...]).astype(BF16)

    h = h_scr[...]
    a = _dot(h, wg_ref[...])
    u = _dot(h, wu_ref[...])
    part = _dot((jax.nn.silu(a) * u).astype(BF16), wd_ref[...])

    @pl.when(j == 0)
    def _():
        o_ref[...] = part

    @pl.when(j > 0)
    def _():
        o_ref[...] += part

    @pl.when(j == pl.num_programs(1) - 1)
    def _():
        y = x_ref[...] + 0.5 * o_ref[...]
        if final_norm:
            y = _rms(y, gf_ref[...])
        o_ref[...] = y


def _ffn(x, g, wg, wu, wd, gf, *, final_norm, tm=512, tf=512):
    t, d = x.shape
    f = wg.shape[1]
    return pl.pallas_call(
        functools.partial(_ffn_kernel, final_norm=final_norm),
        out_shape=jax.ShapeDtypeStruct((t, d), F32),
        grid=(t // tm, f // tf),
        in_specs=[
            pl.BlockSpec((tm, d), lambda i, j: (i, 0)),
            pl.BlockSpec((1, d), lambda i, j: (0, 0)),
            pl.BlockSpec((d, tf), lambda i, j: (0, j)),
            pl.BlockSpec((d, tf), lambda i, j: (0, j)),
            pl.BlockSpec((tf, d), lambda i, j: (j, 0)),
            pl.BlockSpec((1, d), lambda i, j: (0, 0)),
        ],
        out_specs=pl.BlockSpec((tm, d), lambda i, j: (i, 0)),
        scratch_shapes=[pltpu.VMEM((tm, d), BF16)],
        compiler_params=_params("parallel", "arbitrary"),
        name="ffn",
    )(x, g, wg, wu, wd, gf)


def _inproj_kernel(x_ref, g_ref, w1_ref, w2_ref, o1_ref, o2_ref):
    h = _rms(x_ref[...], g_ref[...]).astype(BF16)
    o1_ref[...] = _dot(h, w1_ref[...]).astype(BF16)
    o2_ref[...] = _dot(h, w2_ref[...])


def _inproj(x, g, w_qkv, w_rest, *, tm=256):
    t, d = x.shape
    n1, n2 = w_qkv.shape[1], w_rest.shape[1]
    return pl.pallas_call(
        _inproj_kernel,
        out_shape=(jax.ShapeDtypeStruct((t, n1), BF16), jax.ShapeDtypeStruct((t, n2), F32)),
        grid=(t // tm,),
        in_specs=[
            pl.BlockSpec((tm, d), lambda i: (i, 0)),
            pl.BlockSpec((1, d), lambda i: (0, 0)),
            pl.BlockSpec((d, n1), lambda i: (0, 0)),
            pl.BlockSpec((d, n2), lambda i: (0, 0)),
        ],
        out_specs=(pl.BlockSpec((tm, n1), lambda i: (i, 0)), pl.BlockSpec((tm, n2), lambda i: (i, 0))),
        compiler_params=_params("parallel"),
        name="inproj",
    )(x, g, w_qkv, w_rest)


def _dilated_kernel(q_ref, kp_ref, kc_ref, kn_ref, vp_ref, vc_ref, vn_ref, bias_ref, o_ref, lse_ref,
                    k_scr, v_scr, *, tq, n_cls):
    i = pl.program_id(2)
    k_scr[0:BAND_HALF] = kp_ref[...]
    k_scr[BAND_HALF:BAND_HALF + tq] = kc_ref[...]
    k_scr[BAND_HALF + tq:] = kn_ref[...]
    v_scr[0:BAND_HALF] = vp_ref[...]
    v_scr[BAND_HALF:BAND_HALF + tq] = vc_ref[...]
    v_scr[BAND_HALF + tq:] = vn_ref[...]
    scale = HEAD ** -0.5
    sub = 2 * BAND_HALF
    win = sub + 2 * BAND_HALF
    lane = lax.broadcasted_iota(jnp.int32, (sub, LANES), 1)
    for a in range(tq // sub):
        kidx = i * tq + (a * sub - BAND_HALF) + lax.broadcasted_iota(jnp.int32, (1, win), 1)
        valid = (kidx >= 0) & (kidx < n_cls)
        lse_all = jnp.zeros((sub, LANES), F32)
        for h in range(A_HEADS):
            cols = slice(h * HEAD, (h + 1) * HEAD)
            q = q_ref[a * sub:(a + 1) * sub, cols]
            k = k_scr[a * sub:a * sub + win, cols]
            v = v_scr[a * sub:a * sub + win, cols]
            s = _dot_nt(q, k) * scale + bias_ref[h]
            s = jnp.where(valid, s, NEG_INF)
            m = jnp.max(s, axis=1, keepdims=True)
            p = jnp.exp(s - m)
            l = jnp.sum(p, axis=1, keepdims=True)
            o_ref[a * sub:(a + 1) * sub, cols] = _dot(p.astype(BF16), v) / l
            lse_all = jnp.where(lane == h, m + jnp.log(l), lse_all)
        lse_ref[a * sub:(a + 1) * sub, :] = lse_all


def _band_bias(dil):
    slopes = 2.0 ** (-8.0 * jnp.arange(1, A_HEADS + 1, dtype=F32) / A_HEADS)
    sub, win = 2 * BAND_HALF, 4 * BAND_HALF
    rel = jnp.abs(BAND_HALF + jnp.arange(sub)[:, None] - jnp.arange(win)[None, :])
    bias = -slopes[:, None, None] * (dil * rel).astype(F32)[None]
    return jnp.where((rel <= BAND_HALF)[None], bias, NEG_INF)


def _dilated_branch(qkv, b, s, dil):
    n_cls = s // dil
    tq = min(512, n_cls)
    nh = tq // BAND_HALF
    last_halo = n_cls // BAND_HALF - 1
    view = qkv.reshape(b, n_cls, dil * 3 * A_WIDTH)

    def cur(which):
        return pl.BlockSpec((None, tq, A_WIDTH), lambda bi, r, i: (bi, i, 3 * r + which))

    def prev(which):
        return pl.BlockSpec((None, BAND_HALF, A_WIDTH),
                            lambda bi, r, i: (bi, jnp.maximum(i * nh - 1, 0), 3 * r + which))

    def nxt(which):
        return pl.BlockSpec((None, BAND_HALF, A_WIDTH),
                            lambda bi, r, i: (bi, jnp.minimum((i + 1) * nh, last_halo), 3 * r + which))

    o, lse = pl.pallas_call(
        functools.partial(_dilated_kernel, tq=tq, n_cls=n_cls),
        out_shape=(jax.ShapeDtypeStruct((b, n_cls, dil * A_WIDTH), F32),
                   jax.ShapeDtypeStruct((b, n_cls, dil * LANES), F32)),
        grid=(b, dil, n_cls // tq),
        in_specs=[cur(0), prev(1), cur(1), nxt(1), prev(2), cur(2), nxt(2),
                  pl.BlockSpec((A_HEADS, 2 * BAND_HALF, 4 * BAND_HALF), lambda bi, r, i: (0, 0, 0))],
        out_specs=(pl.BlockSpec((None, tq, A_WIDTH), lambda bi, r, i: (bi, i, r)),
                   pl.BlockSpec((None, tq, LANES), lambda bi, r, i: (bi, i, r))),
        scratch_shapes=[pltpu.VMEM((tq + 2 * BAND_HALF, A_WIDTH), BF16),
                        pltpu.VMEM((tq + 2 * BAND_HALF, A_WIDTH), BF16)],
        compiler_params=_params("parallel", "parallel", "parallel"),
        name=f"dilated{dil}",
    )(view, view, view, view, view, view, view, _band_bias(dil))
    return o.reshape(b * s, A_WIDTH), lse.reshape(b * s, LANES)


def _rotate(y, tab):
    z = y * tab
    r = z + pltpu.roll(z, QK_ROPE, axis=1)
    lane = lax.broadcasted_iota(jnp.int32, r.shape, 1)
    return jnp.where(lane < QK_ROPE, r, 0.0)


def _mla_proj_kernel(cq_ref, ckv_ref, kr_ref, tab_ref, gq_ref, gkv_ref, wq_ref, wkv_ref,
                     q_ref, k_ref, v_ref, cq_scr, ckv_scr, kr_scr):
    j = pl.program_id(1)

    @pl.when(j == 0)
    def _():
        cq_scr[...] = _rms(cq_ref[...], gq_ref[...]).astype(BF16)
        ckv_scr[...] = _rms(ckv_ref[...], gkv_ref[...]).astype(BF16)
        kr_scr[...] = _rotate(kr_ref[...], tab_ref[...]).astype(BF16)

    q = _dot(cq_scr[...], wq_ref[...])
    q_ref[:, :HEAD] = q[:, :HEAD].astype(BF16)
    q_ref[:, HEAD:] = _rotate(q[:, HEAD:], tab_ref[...]).astype(BF16)
    kv = _dot(ckv_scr[...], wkv_ref[...])
    k_ref[:, :HEAD] = kv[:, :HEAD].astype(BF16)
    k_ref[:, HEAD:] = kr_scr[...]
    v_ref[...] = kv[:, HEAD:].astype(BF16)


def _mla_proj(rest, tab, gq, gkv, wq, wkv, *, s, tm=512):
    t = rest.shape[0]
    lat = 512
    pos_blocks = s // tm
    return pl.pallas_call(
        _mla_proj_kernel,
        out_shape=(jax.ShapeDtypeStruct((t, B_HEADS * 2 * HEAD), BF16),
                   jax.ShapeDtypeStruct((t, B_HEADS * 2 * HEAD), BF16),
                   jax.ShapeDtypeStruct((t, B_HEADS * HEAD), BF16)),
        grid=(t // tm, B_HEADS),
        in_specs=[
            pl.BlockSpec((tm, lat), lambda i, j: (i, 0)),
            pl.BlockSpec((tm, lat), lambda i, j: (i, 1)),
            pl.BlockSpec((tm, LANES), lambda i, j: (i, 4 * lat // LANES)),
            pl.BlockSpec((tm, LANES), lambda i, j: (i % pos_blocks, 0)),
            pl.BlockSpec((1, lat), lambda i, j: (0, 0)),
            pl.BlockSpec((1, lat), lambda i, j: (0, 0)),
            pl.BlockSpec((lat, 2 * HEAD), lambda i, j: (0, j)),
            pl.BlockSpec((lat, 2 * HEAD), lambda i, j: (0, j)),
        ],
        out_specs=(pl.BlockSpec((tm, 2 * HEAD), lambda i, j: (i, j)),
                   pl.BlockSpec((tm, 2 * HEAD), lambda i, j: (i, j)),
                   pl.BlockSpec((tm, HEAD), lambda i, j: (i, j))),
        scratch_shapes=[pltpu.VMEM((tm, lat), BF16), pltpu.VMEM((tm, lat), BF16),
                        pltpu.VMEM((tm, LANES), BF16)],
        compiler_params=_params("parallel", "arbitrary"),
        name="mla_proj",
    )(rest, rest, rest, tab, gq, gkv, wq, wkv)


def _mla_attn_kernel(q_ref, k_ref, v_ref, o_ref, m_scr, l_scr, acc_scr, *, tk, nk):
    c = (HEAD + QK_ROPE) ** -0.5 * math.log2(math.e)
    q = q_ref[...]

    def scores(kk):
        start = pl.multiple_of(kk * tk, tk)
        return _dot_nt(q, k_ref[pl.ds(start, tk), :]), v_ref[pl.ds(start, tk), :]

    s, v = scores(0)
    m = jnp.max(s, axis=1, keepdims=True)
    p = jnp.exp2((s - m) * c)
    m_scr[...] = m
    l_scr[...] = jnp.sum(p, axis=1, keepdims=True)
    acc_scr[...] = _dot(p.astype(BF16), v)

    def body(kk, carry):
        s, v = scores(kk)
        m_old = m_scr[...]
        m_new = jnp.maximum(m_old, jnp.max(s, axis=1, keepdims=True))
        alpha = jnp.exp2((m_old - m_new) * c)
        p = jnp.exp2((s - m_new) * c)
        l_scr[...] = alpha * l_scr[...] + jnp.sum(p, axis=1, keepdims=True)
        acc_scr[...] = alpha * acc_scr[...] + _dot(p.astype(BF16), v)
        m_scr[...] = m_new
        return carry

    lax.fori_loop(1, nk, body, 0)
    o_ref[...] = acc_scr[...] / l_scr[...]


def _mla_attn(q, k, v, *, b, s, tq=512, tk=512):
    t = b * s
    nq = s // tq
    return pl.pallas_call(
        functools.partial(_mla_attn_kernel, tk=tk, nk=s // tk),
        out_shape=jax.ShapeDtypeStruct((t, B_HEADS * HEAD), F32),
        grid=(b, B_HEADS, nq),
        in_specs=[
            pl.BlockSpec((tq, 2 * HEAD), lambda bi, h, i: (bi * nq + i, h)),
            pl.BlockSpec((s, 2 * HEAD), lambda bi, h, i: (bi, h)),
            pl.BlockSpec((s, HEAD), lambda bi, h, i: (bi, h)),
        ],
        out_specs=pl.BlockSpec((tq, HEAD), lambda bi, h, i: (bi * nq + i, h)),
        scratch_shapes=[pltpu.VMEM((tq, 1), F32), pltpu.VMEM((tq, 1), F32), pltpu.VMEM((tq, HEAD), F32)],
        compiler_params=_params("parallel", "parallel", "arbitrary"),
        name="mla_attn",
    )(q, k, v)


def _softplus(x):
    return jnp.maximum(x, 0.0) + jnp.log1p(jnp.exp(-jnp.abs(x)))


def _scan_tile(a, b, row, reverse):
    for sh in (1, 2, 4):
        if reverse:
            keep = row < SUBLANES - sh
            a_s = pltpu.roll(a, SUBLANES - sh, axis=0)
            b_s = pltpu.roll(b, SUBLANES - sh, axis=0)
        else:
            keep = row >= sh
            a_s = pltpu.roll(a, sh, axis=0)
            b_s = pltpu.roll(b, sh, axis=0)
        b = b + a * jnp.where(keep, b_s, 0.0)
        a = a * jnp.where(keep, a_s, 1.0)
    return a, b


def _rglru_fwd_kernel(uc_ref, up_ref, un_ref, cw_ref, cb_ref, wg_ref, bg_ref, lam_ref,
                      hf_ref, ab_ref, bb_ref, ubuf, af_scr, bf_scr, carry, *, ts):
    i = pl.program_id(1)
    ns = pl.num_programs(1)
    ubuf[0:SUBLANES] = jnp.where(i > 0, up_ref[...], 0.0)
    ubuf[SUBLANES:SUBLANES + ts] = uc_ref[...]
    ubuf[SUBLANES + ts:] = jnp.where(i < ns - 1, un_ref[...], 0.0)
    u = cb_ref[...] + cw_ref[0:1, :] * ubuf[SUBLANES - 2:SUBLANES - 2 + ts]
    u = u + cw_ref[1:2, :] * ubuf[SUBLANES - 1:SUBLANES - 1 + ts]
    u = u + cw_ref[2:3, :] * ubuf[SUBLANES:SUBLANES + ts]
    u = u + cw_ref[3:4, :] * ubuf[SUBLANES + 1:SUBLANES + 1 + ts]
    gates = _dot(u.astype(BF16), wg_ref[...]) + bg_ref[...]
    sp = _softplus(-lam_ref[...])
    for d in range(2):
        r = jax.nn.sigmoid(gates[:, (2 * d) * C_WIDTH:(2 * d + 1) * C_WIDTH])
        ig = jax.nn.sigmoid(gates[:, (2 * d + 1) * C_WIDTH:(2 * d + 2) * C_WIDTH])
        log_a = -RG_C * r * sp[d:d + 1, :]
        a = jnp.exp(log_a)
        bterm = jnp.sqrt(-jnp.tanh(log_a) * (a * a + 1.0)) * (ig * u)
        if d == 0:
            af_scr[...] = a
            bf_scr[...] = bterm
        else:
            ab_ref[...] = a
            bb_ref[...] = bterm

    @pl.when(i == 0)
    def _():
        carry[...] = jnp.zeros_like(carry)

    row = lax.broadcasted_iota(jnp.int32, (SUBLANES, C_WIDTH), 0)

    def body(t, c):
        rows = pl.ds(pl.multiple_of(t * SUBLANES, SUBLANES), SUBLANES)
        a, b = _scan_tile(af_scr[rows, :], bf_scr[rows, :], row, reverse=False)
        h = b + a * carry[...]
        hf_ref[rows, :] = h
        carry[...] = h[SUBLANES - 1:SUBLANES, :]
        return c

    lax.fori_loop(0, ts // SUBLANES, body, 0)


def _rglru_bwd_kernel(a_ref, b_ref, hf_ref, g_ref, y_ref, carry, *, ts):
    @pl.when(pl.program_id(1) == 0)
    def _():
        carry[...] = jnp.zeros_like(carry)

    row = lax.broadcasted_iota(jnp.int32, (SUBLANES, C_WIDTH), 0)
    nt = ts // SUBLANES

    def body(t, c):
        rows = pl.ds(pl.multiple_of((nt - 1 - t) * SUBLANES, SUBLANES), SUBLANES)
        a, b = _scan_tile(a_ref[rows, :], b_ref[rows, :], row, reverse=True)
        h = b + a * carry[...]
        carry[...] = h[0:1, :]
        y_ref[rows, :] = jax.nn.gelu(g_ref[rows, :]) * (hf_ref[rows, :] + h)
        return c

    lax.fori_loop(0, nt, body, 0)


def _rglru(rest, conv_w, conv_b, w_gates, b_gates, lam, *, b, s, ts=512):
    t = b * s
    ns = s // ts
    hb = ts // SUBLANES
    last = s // SUBLANES - 1
    u_col, g_col = 2, 3
    tile = lambda bi, i: (bi * ns + i, 0)
    hf, ab, bb = pl.pallas_call(
        functools.partial(_rglru_fwd_kernel, ts=ts),
        out_shape=tuple(jax.ShapeDtypeStruct((t, C_WIDTH), F32) for _ in range(3)),
        grid=(b, ns),
        in_specs=[
            pl.BlockSpec((ts, C_WIDTH), lambda bi, i: (bi * ns + i, u_col)),
            pl.BlockSpec((SUBLANES, C_WIDTH),
                         lambda bi, i: (bi * ns * hb + jnp.maximum(i * hb - 1, 0), u_col)),
            pl.BlockSpec((SUBLANES, C_WIDTH),
                         lambda bi, i: (bi * ns * hb + jnp.minimum((i + 1) * hb, last), u_col)),
            pl.BlockSpec((4, C_WIDTH), lambda bi, i: (0, 0)),
            pl.BlockSpec((1, C_WIDTH), lambda bi, i: (0, 0)),
            pl.BlockSpec((C_WIDTH, 4 * C_WIDTH), lambda bi, i: (0, 0)),
            pl.BlockSpec((1, 4 * C_WIDTH), lambda bi, i: (0, 0)),
            pl.BlockSpec((2, C_WIDTH), lambda bi, i: (0, 0)),
        ],
        out_specs=tuple(pl.BlockSpec((ts, C_WIDTH), tile) for _ in range(3)),
        scratch_shapes=[pltpu.VMEM((ts + 2 * SUBLANES, C_WIDTH), F32),
                        pltpu.VMEM((ts, C_WIDTH), F32), pltpu.VMEM((ts, C_WIDTH), F32),
                        pltpu.VMEM((1, C_WIDTH), F32)],
        compiler_params=_params("parallel", "arbitrary"),
        name="rglru_fwd",
    )(rest, rest, rest, conv_w, conv_b, w_gates, b_gates, lam)
    rev = lambda bi, i: (bi * ns + ns - 1 - i, 0)
    return pl.pallas_call(
        functools.partial(_rglru_bwd_kernel, ts=ts),
        out_shape=jax.ShapeDtypeStruct((t, C_WIDTH), F32),
        grid=(b, ns),
        in_specs=[
            pl.BlockSpec((ts, C_WIDTH), rev),
            pl.BlockSpec((ts, C_WIDTH), rev),
            pl.BlockSpec((ts, C_WIDTH), rev),
            pl.BlockSpec((ts, C_WIDTH), lambda bi, i: (bi * ns + ns - 1 - i, g_col)),
        ],
        out_specs=pl.BlockSpec((ts, C_WIDTH), rev),
        scratch_shapes=[pltpu.VMEM((1, C_WIDTH), F32)],
        compiler_params=_params("parallel", "arbitrary"),
        name="rglru_bwd",
    )(ab, bb, hf, rest)


def _outproj_kernel(x_ref, o1_ref, o2_ref, o3_ref, l1_ref, l2_ref, l3_ref, yb_ref, yc_ref,
                    ga_ref, gb_ref, gc_ref, w_ref, out_ref):
    l1, l2, l3 = l1_ref[...], l2_ref[...], l3_ref[...]
    m = jnp.maximum(jnp.maximum(l1, l2), l3)
    e1, e2, e3 = jnp.exp(l1 - m), jnp.exp(l2 - m), jnp.exp(l3 - m)
    z = e1 + e2 + e3
    w1, w2, w3 = e1 / z, e2 / z, e3 / z
    parts = []
    for h in range(A_HEADS):
        cols = slice(h * HEAD, (h + 1) * HEAD)
        parts.append(w1[:, h:h + 1] * o1_ref[:, cols] + w2[:, h:h + 1] * o2_ref[:, cols]
                     + w3[:, h:h + 1] * o3_ref[:, cols])
    ya = jnp.concatenate(parts, axis=1)
    b0, c0 = A_WIDTH, 2 * A_WIDTH
    y = _dot(_rms(ya, ga_ref[...]).astype(BF16), w_ref[0:b0, :])
    y = y + _dot(_rms(yb_ref[...], gb_ref[...]).astype(BF16), w_ref[b0:c0, :])
    y = y + _dot(_rms(yc_ref[...], gc_ref[...]).astype(BF16), w_ref[c0:, :])
    out_ref[...] = x_ref[...] + y


def _outproj(x, o_branches, lse_branches, yb, yc, ga, gb, gc, w, *, tm=256):
    t, d = x.shape
    row = lambda n: pl.BlockSpec((tm, n), lambda i: (i, 0))
    full = lambda r, n: pl.BlockSpec((r, n), lambda i: (0, 0))
    return pl.pallas_call(
        _outproj_kernel,
        out_shape=jax.ShapeDtypeStruct((t, d), F32),
        grid=(t // tm,),
        in_specs=[row(d), row(A_WIDTH), row(A_WIDTH), row(A_WIDTH), row(LANES), row(LANES), row(LANES),
                  row(A_WIDTH), row(C_WIDTH), full(1, A_WIDTH), full(1, A_WIDTH), full(1, C_WIDTH),
                  full(d, d)],
        out_specs=row(d),
        compiler_params=_params("parallel"),
        name="outproj",
    )(x, *o_branches, *lse_branches, yb, yc, ga, gb, gc, w)


def _mem_kv_kernel(m_ref, g_ref, w_ref, o_ref):
    o_ref[...] = _dot(_rms(m_ref[...], g_ref[...]).astype(BF16), w_ref[...]).astype(BF16)


def _mem_kv(mem, g, w_kv):
    t, d = mem.shape
    n = w_kv.shape[1]
    tm = 256
    return pl.pallas_call(
        _mem_kv_kernel,
        out_shape=jax.ShapeDtypeStruct((t, n), BF16),
        grid=(t // tm,),
        in_specs=[pl.BlockSpec((tm, d), lambda i: (i, 0)), pl.BlockSpec((1, d), lambda i: (0, 0)),
                  pl.BlockSpec((d, n), lambda i: (0, 0))],
        out_specs=pl.BlockSpec((tm, n), lambda i: (i, 0)),
        compiler_params=_params("parallel"),
        name="mem_kv",
    )(mem, g, w_kv)


def _xattn_kernel(x_ref, g_ref, wq_ref, kv_ref, wo_ref, o_ref):
    x = x_ref[...]
    q = _dot(_rms(x, g_ref[...]).astype(BF16), wq_ref[...]).astype(BF16)
    scale = HEAD ** -0.5
    outs = []
    for h in range(X_HEADS):
        k = kv_ref[:, h * HEAD:(h + 1) * HEAD]
        v = kv_ref[:, X_WIDTH + h * HEAD:X_WIDTH + (h + 1) * HEAD]
        s = _dot_nt(q[:, h * HEAD:(h + 1) * HEAD], k) * scale
        p = jnp.exp(s - jnp.max(s, axis=1, keepdims=True))
        l = jnp.sum(p, axis=1, keepdims=True)
        outs.append(_dot(p.astype(BF16), v) / l)
    o = jnp.concatenate(outs, axis=1).astype(BF16)
    o_ref[...] = x + _dot(o, wo_ref[...])


def _xattn(x, g, wq, kv, wo, *, s, n_mem, tm=512):
    t, d = x.shape
    per_seq = s // tm
    return pl.pallas_call(
        _xattn_kernel,
        out_shape=jax.ShapeDtypeStruct((t, d), F32),
        grid=(t // tm,),
        in_specs=[
            pl.BlockSpec((tm, d), lambda i: (i, 0)),
            pl.BlockSpec((1, d), lambda i: (0, 0)),
            pl.BlockSpec((d, X_WIDTH), lambda i: (0, 0)),
            pl.BlockSpec((n_mem, 2 * X_WIDTH), lambda i: (i // per_seq, 0)),
            pl.BlockSpec((X_WIDTH, d), lambda i: (0, 0)),
        ],
        out_specs=pl.BlockSpec((tm, d), lambda i: (i, 0)),
        compiler_params=_params("parallel"),
        name="xattn",
    )(x, g, wq, kv, wo)


def _rope_table(s):
    inv = ROPE_THETA ** (-jnp.arange(0, QK_ROPE, 2, dtype=F32) / QK_ROPE)
    ang = jnp.arange(s, dtype=F32)[:, None] * inv[None, :]
    cos, sin = jnp.cos(ang), jnp.sin(ang)
    return jnp.concatenate([cos, cos, -sin, sin], axis=1)


def _swap_halves(w):
    half = w.shape[-1] // 2
    return jnp.concatenate([w[..., half:], w[..., :half]], axis=-1)


def _prep_layer(p, l):
    row = lambda v: v.reshape(1, -1)
    w_in = p['w_in'][l]
    qkv_end, cq_end, ckv_end, kr_end, u_end = 2304, 2816, 3328, 3392, 3904
    w_rope = w_in[:, ckv_end:kr_end]
    w_rest = jnp.concatenate([w_in[:, qkv_end:ckv_end], w_in[:, kr_end:], w_rope, _swap_halves(w_rope)], axis=1)
    wq = p['w_q_up'][l].reshape(-1, B_HEADS, HEAD + QK_ROPE)
    wq = jnp.concatenate([wq, _swap_halves(wq[..., HEAD:])], axis=-1).reshape(-1, B_HEADS * 2 * HEAD)
    eye = jnp.eye(C_BLOCKS, dtype=F32)
    dense = lambda w: jnp.einsum('ncd,nm->ncmd', w, eye).reshape(C_WIDTH, C_WIDTH)
    w_r, w_i = p['w_rg_r'][l], p['w_rg_i'][l]
    w_gates = jnp.concatenate([dense(w_r[0]), dense(w_i[0]), dense(w_r[1]), dense(w_i[1])], axis=1)
    b_r, b_i = p['b_rg_r'][l], p['b_rg_i'][l]
    b_gates = jnp.concatenate([b_r[0], b_i[0], b_r[1], b_i[1]]).reshape(1, -1)
    bf = lambda w: w.astype(BF16)
    return dict(
        g_ffn1=row(p['g_ffn1'][l]), w1_gate=bf(p['w1_gate'][l]), w1_up=bf(p['w1_up'][l]), w1_down=bf(p['w1_down'][l]),
        g_mix=row(p['g_mix'][l]), w_qkv=bf(w_in[:, :qkv_end]), w_rest=bf(w_rest),
        g_q_lat=row(p['g_q_lat'][l]), g_kv_lat=row(p['g_kv_lat'][l]), w_q=bf(wq), w_kv=bf(p['w_kv_up'][l]),
        conv_w=p['conv_w'][l], conv_b=row(p['conv_b'][l]), w_gates=bf(w_gates), b_gates=b_gates,
        lam=p['rg_lambda'][l],
        g_out_a=row(p['g_out_a'][l]), g_out_b=row(p['g_out_b'][l]), g_out_c=row(p['g_out_c'][l]),
        w_out=bf(p['w_out'][l]),
        g_xattn=row(p['g_xattn'][l]), g_mem=row(p['g_mem'][l]), w_xq=bf(p['w_xq'][l]),
        w_xkv=bf(jnp.concatenate([p['w_xk'][l], p['w_xv'][l]], axis=1)), w_xo=bf(p['w_xo'][l]),
        g_ffn2=row(p['g_ffn2'][l]), w2_gate=bf(p['w2_gate'][l]), w2_up=bf(p['w2_up'][l]), w2_down=bf(p['w2_down'][l]),
    )


def _trunk(x, mem, layers, g_final):
    b, s, d = x.shape
    n_mem = mem.shape[1]
    x = x.reshape(b * s, d)
    mem = mem.reshape(b * n_mem, d)
    tab = _rope_table(s)
    for l, w in enumerate(layers):
        x = _ffn(x, w['g_ffn1'], w['w1_gate'], w['w1_up'], w['w1_down'], g_final, final_norm=False)
        qkv, rest = _inproj(x, w['g_mix'], w['w_qkv'], w['w_rest'])
        branches = [_dilated_branch(qkv, b, s, dil) for _, dil in DILATED_CONFIGS]
        q, k, v = _mla_proj(rest, tab, w['g_q_lat'], w['g_kv_lat'], w['w_q'], w['w_kv'], s=s)
        yb = _mla_attn(q, k, v, b=b, s=s)
        yc = _rglru(rest, w['conv_w'], w['conv_b'], w['w_gates'], w['b_gates'], w['lam'], b=b, s=s)
        x = _outproj(x, [o for o, _ in branches], [lse for _, lse in branches], yb, yc,
                     w['g_out_a'], w['g_out_b'], w['g_out_c'], w['w_out'])
        kv = _mem_kv(mem, w['g_mem'], w['w_xkv'])
        x = _xattn(x, w['g_xattn'], w['w_xq'], kv, w['w_xo'], s=s, n_mem=n_mem)
        x = _ffn(x, w['g_ffn2'], w['w2_gate'], w['w2_up'], w['w2_down'], g_final,
                 final_norm=(l == len(layers) - 1))
    return x.reshape(b, s, d)


def kernel(x_prompt, x_sample, mem_prompt, mem_sample, g_ffn1, w1_gate, w1_up, w1_down, g_mix, w_in, g_q_lat, w_q_up, g_kv_lat, w_kv_up, conv_w, conv_b, w_rg_r, b_rg_r, w_rg_i, b_rg_i, rg_lambda, g_out_a, g_out_b, g_out_c, w_out, g_xattn, g_mem, w_xq, w_xk, w_xv, w_xo, g_ffn2, w2_gate, w2_up, w2_down, g_final):
    p = dict(g_ffn1=g_ffn1, w1_gate=w1_gate, w1_up=w1_up, w1_down=w1_down,
             g_mix=g_mix, w_in=w_in, g_q_lat=g_q_lat, w_q_up=w_q_up, g_kv_lat=g_kv_lat, w_kv_up=w_kv_up,
             conv_w=conv_w, conv_b=conv_b, w_rg_r=w_rg_r, b_rg_r=b_rg_r, w_rg_i=w_rg_i, b_rg_i=b_rg_i,
             rg_lambda=rg_lambda, g_out_a=g_out_a, g_out_b=g_out_b, g_out_c=g_out_c, w_out=w_out,
             g_xattn=g_xattn, g_mem=g_mem, w_xq=w_xq, w_xk=w_xk, w_xv=w_xv, w_xo=w_xo,
             g_ffn2=g_ffn2, w2_gate=w2_gate, w2_up=w2_up, w2_down=w2_down)
    layers = [_prep_layer(p, l) for l in range(g_ffn1.shape[0])]
    gf = g_final.reshape(1, -1)
    return (_trunk(x_prompt, mem_prompt, layers, gf), _trunk(x_sample, mem_sample, layers, gf))
```

```python
import functools
import math

import jax
import jax.numpy as jnp
from jax import lax
from jax.experimental import pallas as pl
from jax.experimental.pallas import tpu as pltpu

BF16 = jnp.bfloat16
F32 = jnp.float32

D_MODEL = 2048
A_HEADS = 6
HEAD = 128
A_WIDTH = A_HEADS * HEAD
DILATED_CONFIGS = ((128, 1), (512, 4), (2048, 16))
B_HEADS = 6
QK_ROPE = 64
ROPE_THETA = 10000.0
C_WIDTH = 512
C_BLOCKS = 8
C_BLOCK_W = 64
RG_C = 8.0
X_HEADS = 4
X_WIDTH = 512
D_FF = 5632
EPS = 1e-6
NEG_INF = -1e30

V7X_VMEM_LIMIT_BYTES = 56 * 1024 * 1024
LANES = 128
SUBLANES = 8

REST_WIDTH = 4 * 512 + LANES
BAND_HALF = 64


def _params(*sem):
    return pltpu.CompilerParams(dimension_semantics=sem, vmem_limit_bytes=V7X_VMEM_LIMIT_BYTES)


def _rms(x, g):
    return x * lax.rsqrt(jnp.mean(x * x, axis=-1, keepdims=True) + EPS) * g


def _dot(a, b):
    return jnp.dot(a, b, preferred_element_type=F32)


def _dot_nt(a, b):
    return lax.dot_general(a, b, (((1,), (1,)), ((), ())), preferred_element_type=F32)


def _ffn_kernel(x_ref, g_ref, wg_ref, wu_ref, wd_ref, gf_ref, o_ref, h_scr, *, final_norm):
    j = pl.program_id(1)

    @pl.when(j == 0)
    def _():
        h_scr[...] = _rms(x_ref[...], g_ref[...]).astype(BF16)
        o_ref[...] = jnp.zeros_like(o_ref)

    h = h_scr[...]
    a = _dot(h, wg_ref[...])
    u = _dot(h, wu_ref[...])
    o_ref[...] += _dot((jax.nn.silu(a) * u).astype(BF16), wd_ref[...])

    @pl.when(j == pl.num_programs(1) - 1)
    def _():
        y = x_ref[...] + 0.5 * o_ref[...]
        if final_norm:
            y = _rms(y, gf_ref[...])
        o_ref[...] = y


def _ffn(x, g, wg, wu, wd, gf, *, final_norm, tm=512, tf=512):
    t, d = x.shape
    f = wg.shape[1]
    return pl.pallas_call(
        functools.partial(_ffn_kernel, final_norm=final_norm),
        out_shape=jax.ShapeDtypeStruct((t, d), F32),
        grid=(t // tm, f // tf),
        in_specs=[
            pl.BlockSpec((tm, d), lambda i, j: (i, 0)),
            pl.BlockSpec((1, d), lambda i, j: (0, 0)),
            pl.BlockSpec((d, tf), lambda i, j: (0, j)),
            pl.BlockSpec((d, tf), lambda i, j: (0, j)),
            pl.BlockSpec((tf, d), lambda i, j: (j, 0)),
            pl.BlockSpec((1, d), lambda i, j: (0, 0)),
        ],
        out_specs=pl.BlockSpec((tm, d), lambda i, j: (i, 0)),
        scratch_shapes=[pltpu.VMEM((tm, d), BF16)],
        compiler_params=_params("parallel", "arbitrary"),
        name="ffn",
    )(x, g, wg, wu, wd, gf)


def _inproj_kernel(x_ref, g_ref, w1_ref, w2_ref, o1_ref, o2_ref):
    h = _rms(x_ref[...], g_ref[...]).astype(BF16)
    o1_ref[...] = _dot(h, w1_ref[...]).astype(BF16)
    o2_ref[...] = _dot(h, w2_ref[...])


def _inproj(x, g, w_qkv, w_rest, *, tm=256):
    t, d = x.shape
    n1, n2 = w_qkv.shape[1], w_rest.shape[1]
    return pl.pallas_call(
        _inproj_kernel,
        out_shape=(jax.ShapeDtypeStruct((t, n1), BF16), jax.ShapeDtypeStruct((t, n2), F32)),
        grid=(t // tm,),
        in_specs=[
            pl.BlockSpec((tm, d), lambda i: (i, 0)),
            pl.BlockSpec((1, d), lambda i: (0, 0)),
            pl.BlockSpec((d, n1), lambda i: (0, 0)),
            pl.BlockSpec((d, n2), lambda i: (0, 0)),
        ],
        out_specs=(pl.BlockSpec((tm, n1), lambda i: (i, 0)), pl.BlockSpec((tm, n2), lambda i: (i, 0))),
        compiler_params=_params("parallel"),
        name="inproj",
    )(x, g, w_qkv, w_rest)


def _dilated_kernel(q_ref, kp_ref, kc_ref, kn_ref, vp_ref, vc_ref, vn_ref, bias_ref, o_ref, lse_ref,
                    k_scr, v_scr, *, tq, n_cls):
    i = pl.program_id(2)
    k_scr[0:BAND_HALF] = kp_ref[...]
    k_scr[BAND_HALF:BAND_HALF + tq] = kc_ref[...]
    k_scr[BAND_HALF + tq:] = kn_ref[...]
    v_scr[0:BAND_HALF] = vp_ref[...]
    v_scr[BAND_HALF:BAND_HALF + tq] = vc_ref[...]
    v_scr[BAND_HALF + tq:] = vn_ref[...]
    scale = HEAD ** -0.5
    sub = 2 * BAND_HALF
    win = sub + 2 * BAND_HALF
    lane = lax.broadcasted_iota(jnp.int32, (sub, LANES), 1)
    for a in range(tq // sub):
        kidx = i * tq + (a * sub - BAND_HALF) + lax.broadcasted_iota(jnp.int32, (1, win), 1)
        valid = (kidx >= 0) & (kidx < n_cls)
        lse_all = jnp.zeros((sub, LANES), F32)
        for h in range(A_HEADS):
            cols = slice(h * HEAD, (h + 1) * HEAD)
            q = q_ref[a * sub:(a + 1) * sub, cols]
            k = k_scr[a * sub:a * sub + win, cols]
            v = v_scr[a * sub:a * sub + win, cols]
            s = _dot_nt(q, k) * scale + bias_ref[h]
            s = jnp.where(valid, s, NEG_INF)
            m = jnp.max(s, axis=1, keepdims=True)
            p = jnp.exp(s - m)
            l = jnp.sum(p, axis=1, keepdims=True)
            o_ref[a * sub:(a + 1) * sub, cols] = _dot(p.astype(BF16), v) / l
            lse_all = jnp.where(lane == h, m + jnp.log(l), lse_all)
        lse_ref[a * sub:(a + 1) * sub, :] = lse_all


def _band_bias(dil):
    slopes = 2.0 ** (-8.0 * jnp.arange(1, A_HEADS + 1, dtype=F32) / A_HEADS)
    sub, win = 2 * BAND_HALF, 4 * BAND_HALF
    rel = jnp.abs(BAND_HALF + jnp.arange(sub)[:, None] - jnp.arange(win)[None, :])
    bias = -slopes[:, None, None] * (dil * rel).astype(F32)[None]
    return jnp.where((rel <= BAND_HALF)[None], bias, NEG_INF)


def _dilated_branch(qkv, b, s, dil):
    n_cls = s // dil
    tq = min(512, n_cls)
    nh = tq // BAND_HALF
    last_halo = n_cls // BAND_HALF - 1
    view = qkv.reshape(b, n_cls, dil * 3 * A_WIDTH)

    def cur(which):
        return pl.BlockSpec((None, tq, A_WIDTH), lambda bi, r, i: (bi, i, 3 * r + which))

    def prev(which):
        return pl.BlockSpec((None, BAND_HALF, A_WIDTH),
                            lambda bi, r, i: (bi, jnp.maximum(i * nh - 1, 0), 3 * r + which))

    def nxt(which):
        return pl.BlockSpec((None, BAND_HALF, A_WIDTH),
                            lambda bi, r, i: (bi, jnp.minimum((i + 1) * nh, last_halo), 3 * r + which))

    o, lse = pl.pallas_call(
        functools.partial(_dilated_kernel, tq=tq, n_cls=n_cls),
        out_shape=(jax.ShapeDtypeStruct((b, n_cls, dil * A_WIDTH), F32),
                   jax.ShapeDtypeStruct((b, n_cls, dil * LANES), F32)),
        grid=(b, dil, n_cls // tq),
        in_specs=[cur(0), prev(1), cur(1), nxt(1), prev(2), cur(2), nxt(2),
                  pl.BlockSpec((A_HEADS, 2 * BAND_HALF, 4 * BAND_HALF), lambda bi, r, i: (0, 0, 0))],
        out_specs=(pl.BlockSpec((None, tq, A_WIDTH), lambda bi, r, i: (bi, i, r)),
                   pl.BlockSpec((None, tq, LANES), lambda bi, r, i: (bi, i, r))),
        scratch_shapes=[pltpu.VMEM((tq + 2 * BAND_HALF, A_WIDTH), BF16),
                        pltpu.VMEM((tq + 2 * BAND_HALF, A_WIDTH), BF16)],
        compiler_params=_params("parallel", "parallel", "parallel"),
        name=f"dilated{dil}",
    )(view, view, view, view, view, view, view, _band_bias(dil))
    return o.reshape(b * s, A_WIDTH), lse.reshape(b * s, LANES)


def _rotate(y, tab):
    z = y * tab
    r = z + pltpu.roll(z, QK_ROPE, axis=1)
    lane = lax.broadcasted_iota(jnp.int32, r.shape, 1)
    return jnp.where(lane < QK_ROPE, r, 0.0)


def _mla_proj_kernel(cq_ref, ckv_ref, kr_ref, tab_ref, gq_ref, gkv_ref, wq_ref, wkv_ref,
                     q_ref, k_ref, v_ref, cq_scr, ckv_scr, kr_scr):
    j = pl.program_id(1)

    @pl.when(j == 0)
    def _():
        cq_scr[...] = _rms(cq_ref[...], gq_ref[...]).astype(BF16)
        ckv_scr[...] = _rms(ckv_ref[...], gkv_ref[...]).astype(BF16)
        kr_scr[...] = _rotate(kr_ref[...], tab_ref[...]).astype(BF16)

    q = _dot(cq_scr[...], wq_ref[...])
    q_ref[:, :HEAD] = q[:, :HEAD].astype(BF16)
    q_ref[:, HEAD:] = _rotate(q[:, HEAD:], tab_ref[...]).astype(BF16)
    kv = _dot(ckv_scr[...], wkv_ref[...])
    k_ref[:, :HEAD] = kv[:, :HEAD].astype(BF16)
    k_ref[:, HEAD:] = kr_scr[...]
    v_ref[:, :HEAD] = kv[:, HEAD:].astype(BF16)
    v_ref[:, HEAD:] = jnp.ones((v_ref.shape[0], HEAD), BF16)


def _mla_proj(rest, tab, gq, gkv, wq, wkv, *, s, tm=512):
    t = rest.shape[0]
    lat = 512
    pos_blocks = s // tm
    return pl.pallas_call(
        _mla_proj_kernel,
        out_shape=(jax.ShapeDtypeStruct((t, B_HEADS * 2 * HEAD), BF16),
                   jax.ShapeDtypeStruct((t, B_HEADS * 2 * HEAD), BF16),
                   jax.ShapeDtypeStruct((t, B_HEADS * 2 * HEAD), BF16)),
        grid=(t // tm, B_HEADS),
        in_specs=[
            pl.BlockSpec((tm, lat), lambda i, j: (i, 0)),
            pl.BlockSpec((tm, lat), lambda i, j: (i, 1)),
            pl.BlockSpec((tm, LANES), lambda i, j: (i, 4 * lat // LANES)),
            pl.BlockSpec((tm, LANES), lambda i, j: (i % pos_blocks, 0)),
            pl.BlockSpec((1, lat), lambda i, j: (0, 0)),
            pl.BlockSpec((1, lat), lambda i, j: (0, 0)),
            pl.BlockSpec((lat, 2 * HEAD), lambda i, j: (0, j)),
            pl.BlockSpec((lat, 2 * HEAD), lambda i, j: (0, j)),
        ],
        out_specs=(pl.BlockSpec((tm, 2 * HEAD), lambda i, j: (i, j)),
                   pl.BlockSpec((tm, 2 * HEAD), lambda i, j: (i, j)),
                   pl.BlockSpec((tm, 2 * HEAD), lambda i, j: (i, j))),
        scratch_shapes=[pltpu.VMEM((tm, lat), BF16), pltpu.VMEM((tm, lat), BF16),
                        pltpu.VMEM((tm, LANES), BF16)],
        compiler_params=_params("parallel", "arbitrary"),
        name="mla_proj",
    )(rest, rest, rest, tab, gq, gkv, wq, wkv)


def _mla_attn_kernel(q_ref, k_ref, v_ref, o_ref, m_scr, acc_scr, s_scr, *, tq, tk, nk):
    c = (HEAD + QK_ROPE) ** -0.5 * math.log2(math.e)
    q = q_ref[...]
    nl = tk // LANES

    def chunk(kk):
        return pl.ds(pl.multiple_of(kk * tk, tk), tk)

    def scores(kk, slot):
        s_scr[slot] = _dot_nt(q, k_ref[chunk(kk), :])

    def softmax_pv(kk, slot, first):
        s = s_scr[slot]
        mx = s[:, :LANES]
        for j in range(1, nl):
            mx = jnp.maximum(mx, s[:, j * LANES:(j + 1) * LANES])
        m_new = jnp.broadcast_to(jnp.max(mx, axis=1, keepdims=True), (tq, LANES))
        if not first:
            m_old = m_scr[...]
            m_new = jnp.maximum(m_old, m_new)
        p = jnp.concatenate(
            [jnp.exp2((s[:, j * LANES:(j + 1) * LANES] - m_new) * c).astype(BF16) for j in range(nl)], axis=1)
        pv = _dot(p, v_ref[chunk(kk), :])
        if first:
            acc_scr[...] = pv
        else:
            alpha = jnp.exp2((m_old - m_new) * c)
            acc_scr[...] = jnp.concatenate([alpha, alpha], axis=1) * acc_scr[...] + pv
        m_scr[...] = m_new

    scores(0, 0)
    scores(1, 1)
    softmax_pv(0, 0, True)
    scores(2, 0)
    softmax_pv(1, 1, False)

    def body(g, carry):
        kk = 2 * g
        scores(kk + 1, 1)
        softmax_pv(kk, 0, False)
        scores(jnp.minimum(kk + 2, nk - 1), 0)
        softmax_pv(kk + 1, 1, False)
        return carry

    lax.fori_loop(1, nk // 2, body, 0)
    o_ref[...] = acc_scr[:, :HEAD] / acc_scr[:, HEAD:]


def _mla_attn(q, k, v, *, b, s, tq=512, tk=512):
    t = b * s
    nq = s // tq
    nk = s // tk
    assert nk % 2 == 0 and nk >= 4
    return pl.pallas_call(
        functools.partial(_mla_attn_kernel, tq=tq, tk=tk, nk=nk),
        out_shape=jax.ShapeDtypeStruct((t, B_HEADS * HEAD), F32),
        grid=(b, B_HEADS, nq),
        in_specs=[
            pl.BlockSpec((tq, 2 * HEAD), lambda bi, h, i: (bi * nq + i, h)),
            pl.BlockSpec((s, 2 * HEAD), lambda bi, h, i: (bi, h)),
            pl.BlockSpec((s, 2 * HEAD), lambda bi, h, i: (bi, h)),
        ],
        out_specs=pl.BlockSpec((tq, HEAD), lambda bi, h, i: (bi * nq + i, h)),
        scratch_shapes=[pltpu.VMEM((tq, LANES), F32), pltpu.VMEM((tq, 2 * HEAD), F32),
                        pltpu.VMEM((2, tq, tk), F32)],
        compiler_params=_params("parallel", "parallel", "arbitrary"),
        name="mla_attn",
    )(q, k, v)


def _softplus(x):
    return jnp.maximum(x, 0.0) + jnp.log1p(jnp.exp(-jnp.abs(x)))


def _scan_tile(a, b, row, reverse):
    for sh in (1, 2, 4):
        if reverse:
            keep = row < SUBLANES - sh
            a_s = pltpu.roll(a, SUBLANES - sh, axis=0)
            b_s = pltpu.roll(b, SUBLANES - sh, axis=0)
        else:
            keep = row >= sh
            a_s = pltpu.roll(a, sh, axis=0)
            b_s = pltpu.roll(b, sh, axis=0)
        b = b + a * jnp.where(keep, b_s, 0.0)
        a = a * jnp.where(keep, a_s, 1.0)
    return a, b


def _rglru_fwd_kernel(uc_ref, up_ref, un_ref, cw_ref, cb_ref, wg_ref, bg_ref, lam_ref,
                      hf_ref, ab_ref, bb_ref, ubuf, af_scr, bf_scr, carry, *, ts):
    i = pl.program_id(1)
    ns = pl.num_programs(1)
    ubuf[0:SUBLANES] = jnp.where(i > 0, up_ref[...], 0.0)
    ubuf[SUBLANES:SUBLANES + ts] = uc_ref[...]
    ubuf[SUBLANES + ts:] = jnp.where(i < ns - 1, un_ref[...], 0.0)
    u = cb_ref[...] + cw_ref[0:1, :] * ubuf[SUBLANES - 2:SUBLANES - 2 + ts]
    u = u + cw_ref[1:2, :] * ubuf[SUBLANES - 1:SUBLANES - 1 + ts]
    u = u + cw_ref[2:3, :] * ubuf[SUBLANES:SUBLANES + ts]
    u = u + cw_ref[3:4, :] * ubuf[SUBLANES + 1:SUBLANES + 1 + ts]
    gates = _dot(u.astype(BF16), wg_ref[...]) + bg_ref[...]
    sp = _softplus(-lam_ref[...])
    for d in range(2):
        r = jax.nn.sigmoid(gates[:, (2 * d) * C_WIDTH:(2 * d + 1) * C_WIDTH])
        ig = jax.nn.sigmoid(gates[:, (2 * d + 1) * C_WIDTH:(2 * d + 2) * C_WIDTH])
        log_a = -RG_C * r * sp[d:d + 1, :]
        a = jnp.exp(log_a)
        bterm = jnp.sqrt(-jnp.tanh(log_a) * (a * a + 1.0)) * (ig * u)
        if d == 0:
            af_scr[...] = a
            bf_scr[...] = bterm
        else:
            ab_ref[...] = a
            bb_ref[...] = bterm

    @pl.when(i == 0)
    def _():
        carry[...] = jnp.zeros_like(carry)

    row = lax.broadcasted_iota(jnp.int32, (SUBLANES, C_WIDTH), 0)

    def body(t, c):
        rows = pl.ds(pl.multiple_of(t * SUBLANES, SUBLANES), SUBLANES)
        a, b = _scan_tile(af_scr[rows, :], bf_scr[rows, :], row, reverse=False)
        h = b + a * carry[...]
        hf_ref[rows, :] = h
        carry[...] = h[SUBLANES - 1:SUBLANES, :]
        return c

    lax.fori_loop(0, ts // SUBLANES, body, 0)


def _rglru_bwd_kernel(a_ref, b_ref, hf_ref, g_ref, y_ref, carry, *, ts):
    @pl.when(pl.program_id(1) == 0)
    def _():
        carry[...] = jnp.zeros_like(carry)

    row = lax.broadcasted_iota(jnp.int32, (SUBLANES, C_WIDTH), 0)
    nt = ts // SUBLANES

    def body(t, c):
        rows = pl.ds(pl.multiple_of((nt - 1 - t) * SUBLANES, SUBLANES), SUBLANES)
        a, b = _scan_tile(a_ref[rows, :], b_ref[rows, :], row, reverse=True)
        h = b + a * carry[...]
        carry[...] = h[0:1, :]
        y_ref[rows, :] = jax.nn.gelu(g_ref[rows, :]) * (hf_ref[rows, :] + h)
        return c

    lax.fori_loop(0, nt, body, 0)


def _rglru(rest, conv_w, conv_b, w_gates, b_gates, lam, *, b, s, ts=512):
    t = b * s
    ns = s // ts
    hb = ts // SUBLANES
    last = s // SUBLANES - 1
    u_col, g_col = 2, 3
    tile = lambda bi, i: (bi * ns + i, 0)
    hf, ab, bb = pl.pallas_call(
        functools.partial(_rglru_fwd_kernel, ts=ts),
        out_shape=tuple(jax.ShapeDtypeStruct((t, C_WIDTH), F32) for _ in range(3)),
        grid=(b, ns),
        in_specs=[
            pl.BlockSpec((ts, C_WIDTH), lambda bi, i: (bi * ns + i, u_col)),
            pl.BlockSpec((SUBLANES, C_WIDTH),
                         lambda bi, i: (bi * ns * hb + jnp.maximum(i * hb - 1, 0), u_col)),
            pl.BlockSpec((SUBLANES, C_WIDTH),
                         lambda bi, i: (bi * ns * hb + jnp.minimum((i + 1) * hb, last), u_col)),
            pl.BlockSpec((4, C_WIDTH), lambda bi, i: (0, 0)),
            pl.BlockSpec((1, C_WIDTH), lambda bi, i: (0, 0)),
            pl.BlockSpec((C_WIDTH, 4 * C_WIDTH), lambda bi, i: (0, 0)),
            pl.BlockSpec((1, 4 * C_WIDTH), lambda bi, i: (0, 0)),
            pl.BlockSpec((2, C_WIDTH), lambda bi, i: (0, 0)),
        ],
        out_specs=tuple(pl.BlockSpec((ts, C_WIDTH), tile) for _ in range(3)),
        scratch_shapes=[pltpu.VMEM((ts + 2 * SUBLANES, C_WIDTH), F32),
                        pltpu.VMEM((ts, C_WIDTH), F32), pltpu.VMEM((ts, C_WIDTH), F32),
                        pltpu.VMEM((1, C_WIDTH), F32)],
        compiler_params=_params("parallel", "arbitrary"),
        name="rglru_fwd",
    )(rest, rest, rest, conv_w, conv_b, w_gates, b_gates, lam)
    rev = lambda bi, i: (bi * ns + ns - 1 - i, 0)
    return pl.pallas_call(
        functools.partial(_rglru_bwd_kernel, ts=ts),
        out_shape=jax.ShapeDtypeStruct((t, C_WIDTH), F32),
        grid=(b, ns),
        in_specs=[
            pl.BlockSpec((ts, C_WIDTH), rev),
            pl.BlockSpec((ts, C_WIDTH), rev),
            pl.BlockSpec((ts, C_WIDTH), rev),
            pl.BlockSpec((ts, C_WIDTH), lambda bi, i: (bi * ns + ns - 1 - i, g_col)),
        ],
        out_specs=pl.BlockSpec((ts, C_WIDTH), rev),
        scratch_shapes=[pltpu.VMEM((1, C_WIDTH), F32)],
        compiler_params=_params("parallel", "arbitrary"),
        name="rglru_bwd",
    )(ab, bb, hf, rest)


def _outproj_kernel(x_ref, o1_ref, o2_ref, o3_ref, l1_ref, l2_ref, l3_ref, yb_ref, yc_ref,
                    ga_ref, gb_ref, gc_ref, w_ref, out_ref):
    l1, l2, l3 = l1_ref[...], l2_ref[...], l3_ref[...]
    m = jnp.maximum(jnp.maximum(l1, l2), l3)
    e1, e2, e3 = jnp.exp(l1 - m), jnp.exp(l2 - m), jnp.exp(l3 - m)
    z = e1 + e2 + e3
    w1, w2, w3 = e1 / z, e2 / z, e3 / z
    parts = []
    for h in range(A_HEADS):
        cols = slice(h * HEAD, (h + 1) * HEAD)
        parts.append(w1[:, h:h + 1] * o1_ref[:, cols] + w2[:, h:h + 1] * o2_ref[:, cols]
                     + w3[:, h:h + 1] * o3_ref[:, cols])
    ya = jnp.concatenate(parts, axis=1)
    b0, c0 = A_WIDTH, 2 * A_WIDTH
    y = _dot(_rms(ya, ga_ref[...]).astype(BF16), w_ref[0:b0, :])
    y = y + _dot(_rms(yb_ref[...], gb_ref[...]).astype(BF16), w_ref[b0:c0, :])
    y = y + _dot(_rms(yc_ref[...], gc_ref[...]).astype(BF16), w_ref[c0:, :])
    out_ref[...] = x_ref[...] + y


def _outproj(x, o_branches, lse_branches, yb, yc, ga, gb, gc, w, *, tm=256):
    t, d = x.shape
    row = lambda n: pl.BlockSpec((tm, n), lambda i: (i, 0))
    full = lambda r, n: pl.BlockSpec((r, n), lambda i: (0, 0))
    return pl.pallas_call(
        _outproj_kernel,
        out_shape=jax.ShapeDtypeStruct((t, d), F32),
        grid=(t // tm,),
        in_specs=[row(d), row(A_WIDTH), row(A_WIDTH), row(A_WIDTH), row(LANES), row(LANES), row(LANES),
                  row(A_WIDTH), row(C_WIDTH), full(1, A_WIDTH), full(1, A_WIDTH), full(1, C_WIDTH),
                  full(d, d)],
        out_specs=row(d),
        compiler_params=_params("parallel"),
        name="outproj",
    )(x, *o_branches, *lse_branches, yb, yc, ga, gb, gc, w)


def _mem_kv_kernel(m_ref, g_ref, w_ref, o_ref):
    o_ref[...] = _dot(_rms(m_ref[...], g_ref[...]).astype(BF16), w_ref[...]).astype(BF16)


def _mem_kv(mem, g, w_kv):
    t, d = mem.shape
    n = w_kv.shape[1]
    tm = 256
    return pl.pallas_call(
        _mem_kv_kernel,
        out_shape=jax.ShapeDtypeStruct((t, n), BF16),
        grid=(t // tm,),
        in_specs=[pl.BlockSpec((tm, d), lambda i: (i, 0)), pl.BlockSpec((1, d), lambda i: (0, 0)),
                  pl.BlockSpec((d, n), lambda i: (0, 0))],
        out_specs=pl.BlockSpec((tm, n), lambda i: (i, 0)),
        compiler_params=_params("parallel"),
        name="mem_kv",
    )(mem, g, w_kv)


def _xattn_kernel(x_ref, g_ref, wq_ref, kv_ref, wo_ref, o_ref):
    x = x_ref[...]
    q = _dot(_rms(x, g_ref[...]).astype(BF16), wq_ref[...]).astype(BF16)
    scale = HEAD ** -0.5
    outs = []
    for h in range(X_HEADS):
        k = kv_ref[:, h * HEAD:(h + 1) * HEAD]
        v = kv_ref[:, X_WIDTH + h * HEAD:X_WIDTH + (h + 1) * HEAD]
        s = _dot_nt(q[:, h * HEAD:(h + 1) * HEAD], k) * scale
        p = jnp.exp(s - jnp.max(s, axis=1, keepdims=True))
        l = jnp.sum(p, axis=1, keepdims=True)
        outs.append(_dot(p.astype(BF16), v) / l)
    o = jnp.concatenate(outs, axis=1).astype(BF16)
    o_ref[...] = x + _dot(o, wo_ref[...])


def _xattn(x, g, wq, kv, wo, *, s, n_mem, tm=512):
    t, d = x.shape
    per_seq = s // tm
    return pl.pallas_call(
        _xattn_kernel,
        out_shape=jax.ShapeDtypeStruct((t, d), F32),
        grid=(t // tm,),
        in_specs=[
            pl.BlockSpec((tm, d), lambda i: (i, 0)),
            pl.BlockSpec((1, d), lambda i: (0, 0)),
            pl.BlockSpec((d, X_WIDTH), lambda i: (0, 0)),
            pl.BlockSpec((n_mem, 2 * X_WIDTH), lambda i: (i // per_seq, 0)),
            pl.BlockSpec((X_WIDTH, d), lambda i: (0, 0)),
        ],
        out_specs=pl.BlockSpec((tm, d), lambda i: (i, 0)),
        compiler_params=_params("parallel"),
        name="xattn",
    )(x, g, wq, kv, wo)


def _rope_table(s):
    inv = ROPE_THETA ** (-jnp.arange(0, QK_ROPE, 2, dtype=F32) / QK_ROPE)
    ang = jnp.arange(s, dtype=F32)[:, None] * inv[None, :]
    cos, sin = jnp.cos(ang), jnp.sin(ang)
    return jnp.concatenate([cos, cos, -sin, sin], axis=1)


def _swap_halves(w):
    half = w.shape[-1] // 2
    return jnp.concatenate([w[..., half:], w[..., :half]], axis=-1)


def _prep_layer(p, l):
    row = lambda v: v.reshape(1, -1)
    w_in = p['w_in'][l]
    qkv_end, cq_end, ckv_end, kr_end, u_end = 2304, 2816, 3328, 3392, 3904
    w_rope = w_in[:, ckv_end:kr_end]
    w_rest = jnp.concatenate([w_in[:, qkv_end:ckv_end], w_in[:, kr_end:], w_rope, _swap_halves(w_rope)], axis=1)
    wq = p['w_q_up'][l].reshape(-1, B_HEADS, HEAD + QK_ROPE)
    wq = jnp.concatenate([wq, _swap_halves(wq[..., HEAD:])], axis=-1).reshape(-1, B_HEADS * 2 * HEAD)
    eye = jnp.eye(C_BLOCKS, dtype=F32)
    dense = lambda w: jnp.einsum('ncd,nm->ncmd', w, eye).reshape(C_WIDTH, C_WIDTH)
    w_r, w_i = p['w_rg_r'][l], p['w_rg_i'][l]
    w_gates = jnp.concatenate([dense(w_r[0]), dense(w_i[0]), dense(w_r[1]), dense(w_i[1])], axis=1)
    b_r, b_i = p['b_rg_r'][l], p['b_rg_i'][l]
    b_gates = jnp.concatenate([b_r[0], b_i[0], b_r[1], b_i[1]]).reshape(1, -1)
    bf = lambda w: w.astype(BF16)
    return dict(
        g_ffn1=row(p['g_ffn1'][l]), w1_gate=bf(p['w1_gate'][l]), w1_up=bf(p['w1_up'][l]), w1_down=bf(p['w1_down'][l]),
        g_mix=row(p['g_mix'][l]), w_qkv=bf(w_in[:, :qkv_end]), w_rest=bf(w_rest),
        g_q_lat=row(p['g_q_lat'][l]), g_kv_lat=row(p['g_kv_lat'][l]), w_q=bf(wq), w_kv=bf(p['w_kv_up'][l]),
        conv_w=p['conv_w'][l], conv_b=row(p['conv_b'][l]), w_gates=bf(w_gates), b_gates=b_gates,
        lam=p['rg_lambda'][l],
        g_out_a=row(p['g_out_a'][l]), g_out_b=row(p['g_out_b'][l]), g_out_c=row(p['g_out_c'][l]),
        w_out=bf(p['w_out'][l]),
        g_xattn=row(p['g_xattn'][l]), g_mem=row(p['g_mem'][l]), w_xq=bf(p['w_xq'][l]),
        w_xkv=bf(jnp.concatenate([p['w_xk'][l], p['w_xv'][l]], axis=1)), w_xo=bf(p['w_xo'][l]),
        g_ffn2=row(p['g_ffn2'][l]), w2_gate=bf(p['w2_gate'][l]), w2_up=bf(p['w2_up'][l]), w2_down=bf(p['w2_down'][l]),
    )


def _trunk(x, mem, layers, g_final):
    b, s, d = x.shape
    n_mem = mem.shape[1]
    x = x.reshape(b * s, d)
    mem = mem.reshape(b * n_mem, d)
    tab = _rope_table(s)
    for l, w in enumerate(layers):
        x = _ffn(x, w['g_ffn1'], w['w1_gate'], w['w1_up'], w['w1_down'], g_final, final_norm=False)
        qkv, rest = _inproj(x, w['g_mix'], w['w_qkv'], w['w_rest'])
        branches = [_dilated_branch(qkv, b, s, dil) for _, dil in DILATED_CONFIGS]
        q, k, v = _mla_proj(rest, tab, w['g_q_lat'], w['g_kv_lat'], w['w_q'], w['w_kv'], s=s)
        yb = _mla_attn(q, k, v, b=b, s=s)
        yc = _rglru(rest, w['conv_w'], w['conv_b'], w['w_gates'], w['b_gates'], w['lam'], b=b, s=s)
        x = _outproj(x, [o for o, _ in branches], [lse for _, lse in branches], yb, yc,
                     w['g_out_a'], w['g_out_b'], w['g_out_c'], w['w_out'])
        kv = _mem_kv(mem, w['g_mem'], w['w_xkv'])
        x = _xattn(x, w['g_xattn'], w['w_xq'], kv, w['w_xo'], s=s, n_mem=n_mem)
        x = _ffn(x, w['g_ffn2'], w['w2_gate'], w['w2_up'], w['w2_down'], g_final,
                 final_norm=(l == len(layers) - 1))
    return x.reshape(b, s, d)


def kernel(x_prompt, x_sample, mem_prompt, mem_sample, g_ffn1, w1_gate, w1_up, w1_down, g_mix, w_in, g_q_lat, w_q_up, g_kv_lat, w_kv_up, conv_w, conv_b, w_rg_r, b_rg_r, w_rg_i, b_rg_i, rg_lambda, g_out_a, g_out_b, g_out_c, w_out, g_xattn, g_mem, w_xq, w_xk, w_xv, w_xo, g_ffn2, w2_gate, w2_up, w2_down, g_final):
    p = dict(g_ffn1=g_ffn1, w1_gate=w1_gate, w1_up=w1_up, w1_down=w1_down,
             g_mix=g_mix, w_in=w_in, g_q_lat=g_q_lat, w_q_up=w_q_up, g_kv_lat=g_kv_lat, w_kv_up=w_kv_up,
             conv_w=conv_w, conv_b=conv_b, w_rg_r=w_rg_r, b_rg_r=b_rg_r, w_rg_i=w_rg_i, b_rg_i=b_rg_i,
             rg_lambda=rg_lambda, g_out_a=g_out_a, g_out_b=g_out_b, g_out_c=g_out_c, w_out=w_out,
             g_xattn=g_xattn, g_mem=g_mem, w_xq=w_xq, w_xk=w_xk, w_xv=w_xv, w_xo=w_xo,
             g_ffn2=g_ffn2, w2_gate=w2_gate, w2_up=w2_up, w2_down=w2_down)
    layers = [_prep_layer(p, l) for l in range(g_ffn1.shape[0])]
    gf = g_final.reshape(1, -1)
    return (_trunk(x_prompt, mem_prompt, layers, gf), _trunk(x_sample, mem_sample, layers, gf))
```

```python
import functools
import math

import jax
import jax.numpy as jnp
from jax import lax
from jax.experimental import pallas as pl
from jax.experimental.pallas import tpu as pltpu

BF16 = jnp.bfloat16
F32 = jnp.float32

D_MODEL = 2048
A_HEADS = 6
HEAD = 128
A_WIDTH = A_HEADS * HEAD
DILATED_CONFIGS = ((128, 1), (512, 4), (2048, 16))
B_HEADS = 6
QK_ROPE = 64
ROPE_THETA = 10000.0
C_WIDTH = 512
C_BLOCKS = 8
C_BLOCK_W = 64
RG_C = 8.0
X_HEADS = 4
X_WIDTH = 512
D_FF = 5632
EPS = 1e-6
NEG_INF = -1e30

V7X_VMEM_LIMIT_BYTES = 56 * 1024 * 1024
LANES = 128
SUBLANES = 8

REST_WIDTH = 4 * 512 + LANES
BAND_HALF = 64


def _params(*sem):
    return pltpu.CompilerParams(dimension_semantics=sem, vmem_limit_bytes=V7X_VMEM_LIMIT_BYTES)


def _rms(x, g):
    return x * lax.rsqrt(jnp.mean(x * x, axis=-1, keepdims=True) + EPS) * g


def _dot(a, b):
    return jnp.dot(a, b, preferred_element_type=F32)


def _dot_nt(a, b):
    return lax.dot_general(a, b, (((1,), (1,)), ((), ())), preferred_element_type=F32)


def _ffn_kernel(x_ref, g_ref, wg_ref, wu_ref, wd_ref, gf_ref, o_ref, h_scr, *, final_norm):
    j = pl.program_id(1)

    @pl.when(j == 0)
    def _():
        h_scr[...] = _rms(x_ref[...], g_ref[...]).astype(BF16)
        o_ref[...] = jnp.zeros_like(o_ref)

    h = h_scr[...]
    a = _dot(h, wg_ref[...])
    u = _dot(h, wu_ref[...])
    o_ref[...] += _dot((jax.nn.silu(a) * u).astype(BF16), wd_ref[...])

    @pl.when(j == pl.num_programs(1) - 1)
    def _():
        y = x_ref[...] + 0.5 * o_ref[...]
        if final_norm:
            y = _rms(y, gf_ref[...])
        o_ref[...] = y


def _ffn(x, g, wg, wu, wd, gf, *, final_norm, tm=512, tf=512):
    t, d = x.shape
    f = wg.shape[1]
    return pl.pallas_call(
        functools.partial(_ffn_kernel, final_norm=final_norm),
        out_shape=jax.ShapeDtypeStruct((t, d), F32),
        grid=(t // tm, f // tf),
        in_specs=[
            pl.BlockSpec((tm, d), lambda i, j: (i, 0)),
            pl.BlockSpec((1, d), lambda i, j: (0, 0)),
            pl.BlockSpec((d, tf), lambda i, j: (0, j)),
            pl.BlockSpec((d, tf), lambda i, j: (0, j)),
            pl.BlockSpec((tf, d), lambda i, j: (j, 0)),
            pl.BlockSpec((1, d), lambda i, j: (0, 0)),
        ],
        out_specs=pl.BlockSpec((tm, d), lambda i, j: (i, 0)),
        scratch_shapes=[pltpu.VMEM((tm, d), BF16)],
        compiler_params=_params("parallel", "arbitrary"),
        name="ffn",
    )(x, g, wg, wu, wd, gf)


def _inproj_kernel(x_ref, g_ref, w1_ref, w2_ref, o1_ref, o4_ref, o16_ref, o2_ref, y_scr, *, tm):
    h = _rms(x_ref[...], g_ref[...]).astype(BF16)
    y = _dot(h, w1_ref[...])
    o1_ref[0] = y.astype(BF16)
    n_blocks = y.shape[1] // LANES
    for cb in range(n_blocks):
        y_scr[cb] = y[:, cb * LANES:(cb + 1) * LANES]
    for dil, o_ref in ((4, o4_ref), (16, o16_ref)):
        for r in range(dil):
            for cb in range(n_blocks):
                o_ref[r, :, cb * LANES:(cb + 1) * LANES] = (
                    y_scr[cb, pl.ds(r, tm // dil, stride=dil), :].astype(BF16))
    o2_ref[...] = _dot(h, w2_ref[...])


def _inproj(x, g, w_qkv, w_rest, *, b, s, tm=256):
    t, d = x.shape
    n1, n2 = w_qkv.shape[1], w_rest.shape[1]
    per_seq = s // tm
    dils = [dil for _, dil in DILATED_CONFIGS]
    cls_shape = lambda dil: jax.ShapeDtypeStruct((b, dil, s // dil, n1), BF16)
    cls_spec = lambda dil: pl.BlockSpec((None, dil, tm // dil, n1), lambda i: (i // per_seq, 0, i % per_seq, 0))
    resident = dict(pipeline_mode=pl.Buffered(1))
    return pl.pallas_call(
        functools.partial(_inproj_kernel, tm=tm),
        out_shape=(*[cls_shape(dil) for dil in dils], jax.ShapeDtypeStruct((t, n2), F32)),
        grid=(t // tm,),
        in_specs=[
            pl.BlockSpec((tm, d), lambda i: (i, 0)),
            pl.BlockSpec((1, d), lambda i: (0, 0)),
            pl.BlockSpec((d, n1), lambda i: (0, 0), **resident),
            pl.BlockSpec((d, n2), lambda i: (0, 0), **resident),
        ],
        out_specs=(*[cls_spec(dil) for dil in dils], pl.BlockSpec((tm, n2), lambda i: (i, 0))),
        scratch_shapes=[pltpu.VMEM((n1 // LANES, tm, LANES), F32)],
        compiler_params=_params("parallel"),
        name="inproj",
    )(x, g, w_qkv, w_rest)


def _dilated_kernel(q_ref, kp_ref, kc_ref, kn_ref, vp_ref, vc_ref, vn_ref, bias_ref, o_ref, lse_ref,
                    k_scr, v_scr, *, tq, n_cls, dil):
    i = pl.program_id(1)
    r = pl.program_id(2)
    k_scr[0:BAND_HALF] = kp_ref[...]
    k_scr[BAND_HALF:BAND_HALF + tq] = kc_ref[...]
    k_scr[BAND_HALF + tq:] = kn_ref[...]
    v_scr[0:BAND_HALF] = vp_ref[...]
    v_scr[BAND_HALF:BAND_HALF + tq] = vc_ref[...]
    v_scr[BAND_HALF + tq:] = vn_ref[...]
    scale = HEAD ** -0.5
    sub = 2 * BAND_HALF
    win = sub + 2 * BAND_HALF
    lane = lax.broadcasted_iota(jnp.int32, (sub, LANES), 1)
    for a in range(tq // sub):
        rows = pl.ds(a * sub, sub) if dil == 1 else pl.ds(a * sub * dil + r, sub, stride=dil)
        kidx = i * tq + (a * sub - BAND_HALF) + lax.broadcasted_iota(jnp.int32, (1, win), 1)
        valid = (kidx >= 0) & (kidx < n_cls)
        lse_all = jnp.zeros((sub, LANES), F32)
        for h in range(A_HEADS):
            cols = slice(h * HEAD, (h + 1) * HEAD)
            q = q_ref[a * sub:(a + 1) * sub, cols]
            k = k_scr[a * sub:a * sub + win, cols]
            v = v_scr[a * sub:a * sub + win, cols]
            s = _dot_nt(q, k) * scale + bias_ref[h]
            s = jnp.where(valid, s, NEG_INF)
            m = jnp.max(s, axis=1, keepdims=True)
            p = jnp.exp(s - m)
            l = jnp.sum(p, axis=1, keepdims=True)
            o_ref[h, rows, :] = _dot(p.astype(BF16), v) / l
            lse_all = jnp.where(lane == h, m + jnp.log(l), lse_all)
        lse_ref[rows, :] = lse_all


def _band_bias(dil):
    slopes = 2.0 ** (-8.0 * jnp.arange(1, A_HEADS + 1, dtype=F32) / A_HEADS)
    sub, win = 2 * BAND_HALF, 4 * BAND_HALF
    rel = jnp.abs(BAND_HALF + jnp.arange(sub)[:, None] - jnp.arange(win)[None, :])
    bias = -slopes[:, None, None] * (dil * rel).astype(F32)[None]
    return jnp.where((rel <= BAND_HALF)[None], bias, NEG_INF)


def _dilated_branch(qkv, dil):
    b, _, n_cls, _ = qkv.shape
    tq = min(512, n_cls, 2048 // dil)
    nq = n_cls // tq
    nh = tq // BAND_HALF
    last_halo = n_cls // BAND_HALF - 1

    def cur(which):
        return pl.BlockSpec((None, None, tq, A_WIDTH), lambda bi, i, r: (bi, r, i, which))

    def prev(which):
        return pl.BlockSpec((None, None, BAND_HALF, A_WIDTH),
                            lambda bi, i, r: (bi, r, jnp.maximum(i * nh - 1, 0), which))

    def nxt(which):
        return pl.BlockSpec((None, None, BAND_HALF, A_WIDTH),
                            lambda bi, i, r: (bi, r, jnp.minimum((i + 1) * nh, last_halo), which))

    t = b * n_cls * dil
    return pl.pallas_call(
        functools.partial(_dilated_kernel, tq=tq, n_cls=n_cls, dil=dil),
        out_shape=(jax.ShapeDtypeStruct((A_HEADS, t, HEAD), F32), jax.ShapeDtypeStruct((t, LANES), F32)),
        grid=(b, nq, dil),
        in_specs=[cur(0), prev(1), cur(1), nxt(1), prev(2), cur(2), nxt(2),
                  pl.BlockSpec((A_HEADS, 2 * BAND_HALF, 4 * BAND_HALF), lambda bi, i, r: (0, 0, 0))],
        out_specs=(pl.BlockSpec((A_HEADS, tq * dil, HEAD), lambda bi, i, r: (0, bi * nq + i, 0)),
                   pl.BlockSpec((tq * dil, LANES), lambda bi, i, r: (bi * nq + i, 0))),
        scratch_shapes=[pltpu.VMEM((tq + 2 * BAND_HALF, A_WIDTH), BF16),
                        pltpu.VMEM((tq + 2 * BAND_HALF, A_WIDTH), BF16)],
        compiler_params=_params("parallel", "parallel", "arbitrary"),
        name=f"dilated{dil}",
    )(qkv, qkv, qkv, qkv, qkv, qkv, qkv, _band_bias(dil))


def _rotate(y, tab):
    z = y * tab
    r = z + pltpu.roll(z, QK_ROPE, axis=1)
    lane = lax.broadcasted_iota(jnp.int32, r.shape, 1)
    return jnp.where(lane < QK_ROPE, r, 0.0)


def _mla_proj_kernel(cq_ref, ckv_ref, kr_ref, tab_ref, gq_ref, gkv_ref, wq_ref, wkv_ref,
                     q_ref, k_ref, v_ref):
    cq = _rms(cq_ref[...], gq_ref[...]).astype(BF16)
    ckv = _rms(ckv_ref[...], gkv_ref[...]).astype(BF16)
    tab = tab_ref[...]
    kr = _rotate(kr_ref[...], tab).astype(BF16)
    ones = jnp.ones((v_ref.shape[0], HEAD), BF16)
    for h in range(B_HEADS):
        lo, mid, hi = 2 * h * HEAD, (2 * h + 1) * HEAD, (2 * h + 2) * HEAD
        q = _dot(cq, wq_ref[:, lo:hi])
        q_ref[:, lo:mid] = q[:, :HEAD].astype(BF16)
        q_ref[:, mid:hi] = _rotate(q[:, HEAD:], tab).astype(BF16)
        kv = _dot(ckv, wkv_ref[:, lo:hi])
        k_ref[:, lo:mid] = kv[:, :HEAD].astype(BF16)
        k_ref[:, mid:hi] = kr
        v_ref[:, lo:mid] = kv[:, HEAD:].astype(BF16)
        v_ref[:, mid:hi] = ones


def _mla_proj(rest, tab, gq, gkv, wq, wkv, *, s, tm=512):
    t = rest.shape[0]
    lat = 512
    width = B_HEADS * 2 * HEAD
    pos_blocks = s // tm
    return pl.pallas_call(
        _mla_proj_kernel,
        out_shape=tuple(jax.ShapeDtypeStruct((t, width), BF16) for _ in range(3)),
        grid=(t // tm,),
        in_specs=[
            pl.BlockSpec((tm, lat), lambda i: (i, 0)),
            pl.BlockSpec((tm, lat), lambda i: (i, 1)),
            pl.BlockSpec((tm, LANES), lambda i: (i, 4 * lat // LANES)),
            pl.BlockSpec((tm, LANES), lambda i: (i % pos_blocks, 0)),
            pl.BlockSpec((1, lat), lambda i: (0, 0)),
            pl.BlockSpec((1, lat), lambda i: (0, 0)),
            pl.BlockSpec((lat, width), lambda i: (0, 0)),
            pl.BlockSpec((lat, width), lambda i: (0, 0)),
        ],
        out_specs=tuple(pl.BlockSpec((tm, width), lambda i: (i, 0)) for _ in range(3)),
        compiler_params=_params("parallel"),
        name="mla_proj",
    )(rest, rest, rest, tab, gq, gkv, wq, wkv)


def _mla_attn_kernel(q_ref, k_ref, v_ref, o_ref, m_scr, a_scr, acc_scr, s_scr, p_scr, *, tq, tk, nk):
    c = (HEAD + QK_ROPE) ** -0.5 * math.log2(math.e)
    q = q_ref[...]
    nl = tk // LANES

    def chunk(kk):
        return pl.ds(pl.multiple_of(kk * tk, tk), tk)

    def scores(kk, slot):
        s_scr[slot] = _dot_nt(q, k_ref[chunk(kk), :])

    def softmax(slot, first):
        s = s_scr[slot]
        mx = s[:, :LANES]
        for j in range(1, nl):
            mx = jnp.maximum(mx, s[:, j * LANES:(j + 1) * LANES])
        m_new = jnp.broadcast_to(jnp.max(mx, axis=1, keepdims=True), (tq, LANES))
        if not first:
            m_old = m_scr[...]
            m_new = jnp.maximum(m_old, m_new)
            a_scr[slot] = jnp.exp2((m_old - m_new) * c)
        for j in range(nl):
            cols = slice(j * LANES, (j + 1) * LANES)
            p_scr[slot, :, cols] = jnp.exp2((s[:, cols] - m_new) * c).astype(BF16)
        m_scr[...] = m_new

    def values(kk, slot, first):
        pv = _dot(p_scr[slot], v_ref[chunk(kk), :])
        if first:
            acc_scr[...] = pv
        else:
            alpha = a_scr[slot]
            acc_scr[...] = jnp.concatenate([alpha, alpha], axis=1) * acc_scr[...] + pv

    scores(0, 0)
    softmax(0, True)
    scores(1, 1)
    softmax(1, False)
    values(0, 0, True)
    scores(2, 0)

    def body(g, carry):
        kk = 2 * g
        softmax(0, False)
        values(kk - 1, 1, False)
        scores(kk + 1, 1)
        softmax(1, False)
        values(kk, 0, False)
        scores(kk + 2, 0)
        return carry

    lax.fori_loop(1, nk // 2 - 1, body, 0)
    softmax(0, False)
    values(nk - 3, 1, False)
    scores(nk - 1, 1)
    softmax(1, False)
    values(nk - 2, 0, False)
    values(nk - 1, 1, False)
    o_ref[...] = acc_scr[:, :HEAD] / acc_scr[:, HEAD:]


def _mla_attn(q, k, v, *, b, s, tq=1024, tk=512):
    t = b * s
    nq = s // tq
    nk = s // tk
    assert nk % 2 == 0 and nk >= 4
    return pl.pallas_call(
        functools.partial(_mla_attn_kernel, tq=tq, tk=tk, nk=nk),
        out_shape=jax.ShapeDtypeStruct((t, B_HEADS * HEAD), F32),
        grid=(b, B_HEADS, nq),
        in_specs=[
            pl.BlockSpec((tq, 2 * HEAD), lambda bi, h, i: (bi * nq + i, h)),
            pl.BlockSpec((s, 2 * HEAD), lambda bi, h, i: (bi, h)),
            pl.BlockSpec((s, 2 * HEAD), lambda bi, h, i: (bi, h)),
        ],
        out_specs=pl.BlockSpec((tq, HEAD), lambda bi, h, i: (bi * nq + i, h)),
        scratch_shapes=[pltpu.VMEM((tq, LANES), F32), pltpu.VMEM((2, tq, LANES), F32),
                        pltpu.VMEM((tq, 2 * HEAD), F32), pltpu.VMEM((2, tq, tk), F32),
                        pltpu.VMEM((2, tq, tk), BF16)],
        compiler_params=_params("parallel", "parallel", "arbitrary"),
        name="mla_attn",
    )(q, k, v)


def _softplus(x):
    return jnp.maximum(x, 0.0) + jnp.log1p(jnp.exp(-jnp.abs(x)))


def _scan_tile(a, b, row, reverse):
    for sh in (1, 2, 4):
        if reverse:
            keep = row < SUBLANES - sh
            a_s = pltpu.roll(a, SUBLANES - sh, axis=0)
            b_s = pltpu.roll(b, SUBLANES - sh, axis=0)
        else:
            keep = row >= sh
            a_s = pltpu.roll(a, sh, axis=0)
            b_s = pltpu.roll(b, sh, axis=0)
        b = b + a * jnp.where(keep, b_s, 0.0)
        a = a * jnp.where(keep, a_s, 1.0)
    return a, b


def _rglru_fwd_kernel(uc_ref, up_ref, un_ref, cw_ref, cb_ref, wg_ref, bg_ref, lam_ref,
                      hf_ref, ab_ref, bb_ref, ubuf, af_scr, bf_scr, carry, *, ts):
    i = pl.program_id(1)
    ns = pl.num_programs(1)
    ubuf[0:SUBLANES] = jnp.where(i > 0, up_ref[...], 0.0)
    ubuf[SUBLANES:SUBLANES + ts] = uc_ref[...]
    ubuf[SUBLANES + ts:] = jnp.where(i < ns - 1, un_ref[...], 0.0)
    u = cb_ref[...] + cw_ref[0:1, :] * ubuf[SUBLANES - 2:SUBLANES - 2 + ts]
    u = u + cw_ref[1:2, :] * ubuf[SUBLANES - 1:SUBLANES - 1 + ts]
    u = u + cw_ref[2:3, :] * ubuf[SUBLANES:SUBLANES + ts]
    u = u + cw_ref[3:4, :] * ubuf[SUBLANES + 1:SUBLANES + 1 + ts]
    gates = _dot(u.astype(BF16), wg_ref[...]) + bg_ref[...]
    sp = _softplus(-lam_ref[...])
    for d in range(2):
        r = jax.nn.sigmoid(gates[:, (2 * d) * C_WIDTH:(2 * d + 1) * C_WIDTH])
        ig = jax.nn.sigmoid(gates[:, (2 * d + 1) * C_WIDTH:(2 * d + 2) * C_WIDTH])
        log_a = -RG_C * r * sp[d:d + 1, :]
        a = jnp.exp(log_a)
        bterm = jnp.sqrt(-jnp.tanh(log_a) * (a * a + 1.0)) * (ig * u)
        if d == 0:
            af_scr[...] = a
            bf_scr[...] = bterm
        else:
            ab_ref[...] = a
            bb_ref[...] = bterm

    @pl.when(i == 0)
    def _():
        carry[...] = jnp.zeros_like(carry)

    row = lax.broadcasted_iota(jnp.int32, (SUBLANES, C_WIDTH), 0)

    def body(t, c):
        rows = pl.ds(pl.multiple_of(t * SUBLANES, SUBLANES), SUBLANES)
        a, b = _scan_tile(af_scr[rows, :], bf_scr[rows, :], row, reverse=False)
        h = b + a * carry[...]
        hf_ref[rows, :] = h
        carry[...] = h[SUBLANES - 1:SUBLANES, :]
        return c

    lax.fori_loop(0, ts // SUBLANES, body, 0)


def _rglru_bwd_kernel(a_ref, b_ref, hf_ref, g_ref, y_ref, carry, *, ts):
    @pl.when(pl.program_id(1) == 0)
    def _():
        carry[...] = jnp.zeros_like(carry)

    row = lax.broadcasted_iota(jnp.int32, (SUBLANES, C_WIDTH), 0)
    nt = ts // SUBLANES

    def body(t, c):
        rows = pl.ds(pl.multiple_of((nt - 1 - t) * SUBLANES, SUBLANES), SUBLANES)
        a, b = _scan_tile(a_ref[rows, :], b_ref[rows, :], row, reverse=True)
        h = b + a * carry[...]
        carry[...] = h[0:1, :]
        y_ref[rows, :] = jax.nn.gelu(g_ref[rows, :]) * (hf_ref[rows, :] + h)
        return c

    lax.fori_loop(0, nt, body, 0)


def _rglru(rest, conv_w, conv_b, w_gates, b_gates, lam, *, b, s, ts=512):
    t = b * s
    ns = s // ts
    hb = ts // SUBLANES
    last = s // SUBLANES - 1
    u_col, g_col = 2, 3
    tile = lambda bi, i: (bi * ns + i, 0)
    hf, ab, bb = pl.pallas_call(
        functools.partial(_rglru_fwd_kernel, ts=ts),
        out_shape=tuple(jax.ShapeDtypeStruct((t, C_WIDTH), F32) for _ in range(3)),
        grid=(b, ns),
        in_specs=[
            pl.BlockSpec((ts, C_WIDTH), lambda bi, i: (bi * ns + i, u_col)),
            pl.BlockSpec((SUBLANES, C_WIDTH),
                         lambda bi, i: (bi * ns * hb + jnp.maximum(i * hb - 1, 0), u_col)),
            pl.BlockSpec((SUBLANES, C_WIDTH),
                         lambda bi, i: (bi * ns * hb + jnp.minimum((i + 1) * hb, last), u_col)),
            pl.BlockSpec((4, C_WIDTH), lambda bi, i: (0, 0)),
            pl.BlockSpec((1, C_WIDTH), lambda bi, i: (0, 0)),
            pl.BlockSpec((C_WIDTH, 4 * C_WIDTH), lambda bi, i: (0, 0)),
            pl.BlockSpec((1, 4 * C_WIDTH), lambda bi, i: (0, 0)),
            pl.BlockSpec((2, C_WIDTH), lambda bi, i: (0, 0)),
        ],
        out_specs=tuple(pl.BlockSpec((ts, C_WIDTH), tile) for _ in range(3)),
        scratch_shapes=[pltpu.VMEM((ts + 2 * SUBLANES, C_WIDTH), F32),
                        pltpu.VMEM((ts, C_WIDTH), F32), pltpu.VMEM((ts, C_WIDTH), F32),
                        pltpu.VMEM((1, C_WIDTH), F32)],
        compiler_params=_params("parallel", "arbitrary"),
        name="rglru_fwd",
    )(rest, rest, rest, conv_w, conv_b, w_gates, b_gates, lam)
    rev = lambda bi, i: (bi * ns + ns - 1 - i, 0)
    return pl.pallas_call(
        functools.partial(_rglru_bwd_kernel, ts=ts),
        out_shape=jax.ShapeDtypeStruct((t, C_WIDTH), F32),
        grid=(b, ns),
        in_specs=[
            pl.BlockSpec((ts, C_WIDTH), rev),
            pl.BlockSpec((ts, C_WIDTH), rev),
            pl.BlockSpec((ts, C_WIDTH), rev),
            pl.BlockSpec((ts, C_WIDTH), lambda bi, i: (bi * ns + ns - 1 - i, g_col)),
        ],
        out_specs=pl.BlockSpec((ts, C_WIDTH), rev),
        scratch_shapes=[pltpu.VMEM((1, C_WIDTH), F32)],
        compiler_params=_params("parallel", "arbitrary"),
        name="rglru_bwd",
    )(ab, bb, hf, rest)


def _outproj_kernel(x_ref, o1_ref, o2_ref, o3_ref, l1_ref, l2_ref, l3_ref, yb_ref, yc_ref,
                    ga_ref, gb_ref, gc_ref, w_ref, out_ref):
    l1, l2, l3 = l1_ref[...], l2_ref[...], l3_ref[...]
    m = jnp.maximum(jnp.maximum(l1, l2), l3)
    e1, e2, e3 = jnp.exp(l1 - m), jnp.exp(l2 - m), jnp.exp(l3 - m)
    z = e1 + e2 + e3
    w1, w2, w3 = e1 / z, e2 / z, e3 / z
    parts = []
    for h in range(A_HEADS):
        parts.append(w1[:, h:h + 1] * o1_ref[h] + w2[:, h:h + 1] * o2_ref[h] + w3[:, h:h + 1] * o3_ref[h])
    ya = jnp.concatenate(parts, axis=1)
    b0, c0 = A_WIDTH, 2 * A_WIDTH
    y = _dot(_rms(ya, ga_ref[...]).astype(BF16), w_ref[0:b0, :])
    y = y + _dot(_rms(yb_ref[...], gb_ref[...]).astype(BF16), w_ref[b0:c0, :])
    y = y + _dot(_rms(yc_ref[...], gc_ref[...]).astype(BF16), w_ref[c0:, :])
    out_ref[...] = x_ref[...] + y


def _outproj(x, o_branches, lse_branches, yb, yc, ga, gb, gc, w, *, tm=256):
    t, d = x.shape
    row = lambda n: pl.BlockSpec((tm, n), lambda i: (i, 0))
    full = lambda r, n: pl.BlockSpec((r, n), lambda i: (0, 0))
    heads = pl.BlockSpec((A_HEADS, tm, HEAD), lambda i: (0, i, 0))
    return pl.pallas_call(
        _outproj_kernel,
        out_shape=jax.ShapeDtypeStruct((t, d), F32),
        grid=(t // tm,),
        in_specs=[row(d), heads, heads, heads, row(LANES), row(LANES), row(LANES),
                  row(A_WIDTH), row(C_WIDTH), full(1, A_WIDTH), full(1, A_WIDTH), full(1, C_WIDTH),
                  full(d, d)],
        out_specs=row(d),
        compiler_params=_params("parallel"),
        name="outproj",
    )(x, *o_branches, *lse_branches, yb, yc, ga, gb, gc, w)


def _mem_kv_kernel(m_ref, g_ref, w_ref, o_ref):
    o_ref[...] = _dot(_rms(m_ref[...], g_ref[...]).astype(BF16), w_ref[...]).astype(BF16)


def _mem_kv(mem, g, w_kv):
    t, d = mem.shape
    n = w_kv.shape[1]
    tm = 256
    return pl.pallas_call(
        _mem_kv_kernel,
        out_shape=jax.ShapeDtypeStruct((t, n), BF16),
        grid=(t // tm,),
        in_specs=[pl.BlockSpec((tm, d), lambda i: (i, 0)), pl.BlockSpec((1, d), lambda i: (0, 0)),
                  pl.BlockSpec((d, n), lambda i: (0, 0))],
        out_specs=pl.BlockSpec((tm, n), lambda i: (i, 0)),
        compiler_params=_params("parallel"),
        name="mem_kv",
    )(mem, g, w_kv)


def _xattn_kernel(x_ref, g_ref, wq_ref, kv_ref, wo_ref, o_ref):
    x = x_ref[...]
    q = _dot(_rms(x, g_ref[...]).astype(BF16), wq_ref[...]).astype(BF16)
    scale = HEAD ** -0.5
    outs = []
    for h in range(X_HEADS):
        k = kv_ref[:, h * HEAD:(h + 1) * HEAD]
        v = kv_ref[:, X_WIDTH + h * HEAD:X_WIDTH + (h + 1) * HEAD]
        s = _dot_nt(q[:, h * HEAD:(h + 1) * HEAD], k) * scale
        p = jnp.exp(s - jnp.max(s, axis=1, keepdims=True))
        l = jnp.sum(p, axis=1, keepdims=True)
        outs.append(_dot(p.astype(BF16), v) / l)
    o = jnp.concatenate(outs, axis=1).astype(BF16)
    o_ref[...] = x + _dot(o, wo_ref[...])


def _xattn(x, g, wq, kv, wo, *, s, n_mem, tm=512):
    t, d = x.shape
    per_seq = s // tm
    return pl.pallas_call(
        _xattn_kernel,
        out_shape=jax.ShapeDtypeStruct((t, d), F32),
        grid=(t // tm,),
        in_specs=[
            pl.BlockSpec((tm, d), lambda i: (i, 0)),
            pl.BlockSpec((1, d), lambda i: (0, 0)),
            pl.BlockSpec((d, X_WIDTH), lambda i: (0, 0)),
            pl.BlockSpec((n_mem, 2 * X_WIDTH), lambda i: (i // per_seq, 0)),
            pl.BlockSpec((X_WIDTH, d), lambda i: (0, 0)),
        ],
        out_specs=pl.BlockSpec((tm, d), lambda i: (i, 0)),
        compiler_params=_params("parallel"),
        name="xattn",
    )(x, g, wq, kv, wo)


def _rope_table(s):
    inv = ROPE_THETA ** (-jnp.arange(0, QK_ROPE, 2, dtype=F32) / QK_ROPE)
    ang = jnp.arange(s, dtype=F32)[:, None] * inv[None, :]
    cos, sin = jnp.cos(ang), jnp.sin(ang)
    return jnp.concatenate([cos, cos, -sin, sin], axis=1)


def _swap_halves(w):
    half = w.shape[-1] // 2
    return jnp.concatenate([w[..., half:], w[..., :half]], axis=-1)


def _prep_layer(p, l):
    row = lambda v: v.reshape(1, -1)
    w_in = p['w_in'][l]
    qkv_end, cq_end, ckv_end, kr_end, u_end = 2304, 2816, 3328, 3392, 3904
    w_rope = w_in[:, ckv_end:kr_end]
    w_rest = jnp.concatenate([w_in[:, qkv_end:ckv_end], w_in[:, kr_end:], w_rope, _swap_halves(w_rope)], axis=1)
    wq = p['w_q_up'][l].reshape(-1, B_HEADS, HEAD + QK_ROPE)
    wq = jnp.concatenate([wq, _swap_halves(wq[..., HEAD:])], axis=-1).reshape(-1, B_HEADS * 2 * HEAD)
    eye = jnp.eye(C_BLOCKS, dtype=F32)
    dense = lambda w: jnp.einsum('ncd,nm->ncmd', w, eye).reshape(C_WIDTH, C_WIDTH)
    w_r, w_i = p['w_rg_r'][l], p['w_rg_i'][l]
    w_gates = jnp.concatenate([dense(w_r[0]), dense(w_i[0]), dense(w_r[1]), dense(w_i[1])], axis=1)
    b_r, b_i = p['b_rg_r'][l], p['b_rg_i'][l]
    b_gates = jnp.concatenate([b_r[0], b_i[0], b_r[1], b_i[1]]).reshape(1, -1)
    bf = lambda w: w.astype(BF16)
    return dict(
        g_ffn1=row(p['g_ffn1'][l]), w1_gate=bf(p['w1_gate'][l]), w1_up=bf(p['w1_up'][l]), w1_down=bf(p['w1_down'][l]),
        g_mix=row(p['g_mix'][l]), w_qkv=bf(w_in[:, :qkv_end]), w_rest=bf(w_rest),
        g_q_lat=row(p['g_q_lat'][l]), g_kv_lat=row(p['g_kv_lat'][l]), w_q=bf(wq), w_kv=bf(p['w_kv_up'][l]),
        conv_w=p['conv_w'][l], conv_b=row(p['conv_b'][l]), w_gates=bf(w_gates), b_gates=b_gates,
        lam=p['rg_lambda'][l],
        g_out_a=row(p['g_out_a'][l]), g_out_b=row(p['g_out_b'][l]), g_out_c=row(p['g_out_c'][l]),
        w_out=bf(p['w_out'][l]),
        g_xattn=row(p['g_xattn'][l]), g_mem=row(p['g_mem'][l]), w_xq=bf(p['w_xq'][l]),
        w_xkv=bf(jnp.concatenate([p['w_xk'][l], p['w_xv'][l]], axis=1)), w_xo=bf(p['w_xo'][l]),
        g_ffn2=row(p['g_ffn2'][l]), w2_gate=bf(p['w2_gate'][l]), w2_up=bf(p['w2_up'][l]), w2_down=bf(p['w2_down'][l]),
    )


def _trunk(x, mem, layers, g_final):
    b, s, d = x.shape
    n_mem = mem.shape[1]
    x = x.reshape(b * s, d)
    mem = mem.reshape(b * n_mem, d)
    tab = _rope_table(s)
    for l, w in enumerate(layers):
        x = _ffn(x, w['g_ffn1'], w['w1_gate'], w['w1_up'], w['w1_down'], g_final, final_norm=False)
        *qkv_by_dil, rest = _inproj(x, w['g_mix'], w['w_qkv'], w['w_rest'], b=b, s=s)
        branches = [_dilated_branch(qkv, dil) for qkv, (_, dil) in zip(qkv_by_dil, DILATED_CONFIGS)]
        q, k, v = _mla_proj(rest, tab, w['g_q_lat'], w['g_kv_lat'], w['w_q'], w['w_kv'], s=s)
        yb = _mla_attn(q, k, v, b=b, s=s)
        yc = _rglru(rest, w['conv_w'], w['conv_b'], w['w_gates'], w['b_gates'], w['lam'], b=b, s=s)
        x = _outproj(x, [o for o, _ in branches], [lse for _, lse in branches], yb, yc,
                     w['g_out_a'], w['g_out_b'], w['g_out_c'], w['w_out'])
        kv = _mem_kv(mem, w['g_mem'], w['w_xkv'])
        x = _xattn(x, w['g_xattn'], w['w_xq'], kv, w['w_xo'], s=s, n_mem=n_mem)
        x = _ffn(x, w['g_ffn2'], w['w2_gate'], w['w2_up'], w['w2_down'], g_final,
                 final_norm=(l == len(layers) - 1))
    return x.reshape(b, s, d)


def kernel(x_prompt, x_sample, mem_prompt, mem_sample, g_ffn1, w1_gate, w1_up, w1_down, g_mix, w_in, g_q_lat, w_q_up, g_kv_lat, w_kv_up, conv_w, conv_b, w_rg_r, b_rg_r, w_rg_i, b_rg_i, rg_lambda, g_out_a, g_out_b, g_out_c, w_out, g_xattn, g_mem, w_xq, w_xk, w_xv, w_xo, g_ffn2, w2_gate, w2_up, w2_down, g_final):
    p = dict(g_ffn1=g_ffn1, w1_gate=w1_gate, w1_up=w1_up, w1_down=w1_down,
             g_mix=g_mix, w_in=w_in, g_q_lat=g_q_lat, w_q_up=w_q_up, g_kv_lat=g_kv_lat, w_kv_up=w_kv_up,
             conv_w=conv_w, conv_b=conv_b, w_rg_r=w_rg_r, b_rg_r=b_rg_r, w_rg_i=w_rg_i, b_rg_i=b_rg_i,
             rg_lambda=rg_lambda, g_out_a=g_out_a, g_out_b=g_out_b, g_out_c=g_out_c, w_out=w_out,
             g_xattn=g_xattn, g_mem=g_mem, w_xq=w_xq, w_xk=w_xk, w_xv=w_xv, w_xo=w_xo,
             g_ffn2=g_ffn2, w2_gate=w2_gate, w2_up=w2_up, w2_down=w2_down)
    layers = [_prep_layer(p, l) for l in range(g_ffn1.shape[0])]
    gf = g_final.reshape(1, -1)
    return (_trunk(x_prompt, mem_prompt, layers, gf), _trunk(x_sample, mem_sample, layers, gf))
```

```python
import functools
import math

import jax
import jax.numpy as jnp
from jax import lax
from jax.experimental import pallas as pl
from jax.experimental.pallas import tpu as pltpu

BF16 = jnp.bfloat16
F32 = jnp.float32

D_MODEL = 2048
A_HEADS = 6
HEAD = 128
A_WIDTH = A_HEADS * HEAD
DILATED_CONFIGS = ((128, 1), (512, 4), (2048, 16))
B_HEADS = 6
QK_ROPE = 64
ROPE_THETA = 10000.0
C_WIDTH = 512
C_BLOCKS = 8
C_BLOCK_W = 64
RG_C = 8.0
X_HEADS = 4
X_WIDTH = 512
D_FF = 5632
EPS = 1e-6
NEG_INF = -1e30

V7X_VMEM_LIMIT_BYTES = 56 * 1024 * 1024
LANES = 128
SUBLANES = 8

REST_WIDTH = 4 * 512 + LANES
BAND_HALF = 64


def _params(*sem):
    return pltpu.CompilerParams(dimension_semantics=sem, vmem_limit_bytes=V7X_VMEM_LIMIT_BYTES)


def _rms(x, g):
    return x * lax.rsqrt(jnp.mean(x * x, axis=-1, keepdims=True) + EPS) * g


def _dot(a, b):
    return jnp.dot(a, b, preferred_element_type=F32)


def _dot_nt(a, b):
    return lax.dot_general(a, b, (((1,), (1,)), ((), ())), preferred_element_type=F32)


def _ffn_kernel(x_ref, g_ref, wg_ref, wu_ref, wd_ref, gf_ref, o_ref, h_scr, *, final_norm):
    j = pl.program_id(1)

    @pl.when(j == 0)
    def _():
        h_scr[...] = _rms(x_ref[...], g_ref[...]).astype(BF16)
        o_ref[...] = jnp.zeros_like(o_ref)

    h = h_scr[...]
    a = _dot(h, wg_ref[...])
    u = _dot(h, wu_ref[...])
    o_ref[...] += _dot((jax.nn.silu(a) * u).astype(BF16), wd_ref[...])

    @pl.when(j == pl.num_programs(1) - 1)
    def _():
        y = x_ref[...] + 0.5 * o_ref[...]
        if final_norm:
            y = _rms(y, gf_ref[...])
        o_ref[...] = y


def _ffn(x, g, wg, wu, wd, gf, *, final_norm, tf=512):
    t, d = x.shape
    tm = 512 if final_norm else 1024
    f = wg.shape[1]
    return pl.pallas_call(
        functools.partial(_ffn_kernel, final_norm=final_norm),
        out_shape=jax.ShapeDtypeStruct((t, d), F32),
        grid=(t // tm, f // tf),
        in_specs=[
            pl.BlockSpec((tm, d), lambda i, j: (i, 0)),
            pl.BlockSpec((1, d), lambda i, j: (0, 0)),
            pl.BlockSpec((d, tf), lambda i, j: (0, j)),
            pl.BlockSpec((d, tf), lambda i, j: (0, j)),
            pl.BlockSpec((tf, d), lambda i, j: (j, 0)),
            pl.BlockSpec((1, d), lambda i, j: (0, 0)),
        ],
        out_specs=pl.BlockSpec((tm, d), lambda i, j: (i, 0)),
        scratch_shapes=[pltpu.VMEM((tm, d), BF16)],
        compiler_params=_params("parallel", "arbitrary"),
        name="ffn",
    )(x, g, wg, wu, wd, gf)


def _inproj_kernel(x_ref, g_ref, w1_ref, w2_ref, o1_ref, o4_ref, o16_ref, o2_ref, y_scr, c4_scr, *, tm):
    h = _rms(x_ref[...], g_ref[...]).astype(BF16)
    y = _dot(h, w1_ref[...])
    o1_ref[0] = y.astype(BF16)
    n4 = tm // 4
    for cb in range(y.shape[1] // LANES):
        cols = slice(cb * LANES, (cb + 1) * LANES)
        y_scr[cb] = y[:, cols]
        for r4 in range(4):
            c4 = y_scr[cb, pl.ds(r4, n4, stride=4), :]
            o4_ref[r4, :, cols] = c4.astype(BF16)
            c4_scr[cb, r4] = c4
            for j in range(4):
                o16_ref[r4 + 4 * j, :, cols] = c4_scr[cb, r4, pl.ds(j, n4 // 4, stride=4), :].astype(BF16)
    o2_ref[...] = _dot(h, w2_ref[...])


def _inproj(x, g, w_qkv, w_rest, *, b, s, tm=256):
    t, d = x.shape
    n1, n2 = w_qkv.shape[1], w_rest.shape[1]
    per_seq = s // tm
    dils = [dil for _, dil in DILATED_CONFIGS]
    cls_shape = lambda dil: jax.ShapeDtypeStruct((b, dil, s // dil, n1), BF16)
    cls_spec = lambda dil: pl.BlockSpec((None, dil, tm // dil, n1), lambda i: (i // per_seq, 0, i % per_seq, 0))
    resident = dict(pipeline_mode=pl.Buffered(1))
    return pl.pallas_call(
        functools.partial(_inproj_kernel, tm=tm),
        out_shape=(*[cls_shape(dil) for dil in dils], jax.ShapeDtypeStruct((t, n2), F32)),
        grid=(t // tm,),
        in_specs=[
            pl.BlockSpec((tm, d), lambda i: (i, 0)),
            pl.BlockSpec((1, d), lambda i: (0, 0)),
            pl.BlockSpec((d, n1), lambda i: (0, 0), **resident),
            pl.BlockSpec((d, n2), lambda i: (0, 0), **resident),
        ],
        out_specs=(*[cls_spec(dil) for dil in dils], pl.BlockSpec((tm, n2), lambda i: (i, 0))),
        scratch_shapes=[pltpu.VMEM((n1 // LANES, tm, LANES), F32),
                        pltpu.VMEM((n1 // LANES, 4, tm // 4, LANES), F32)],
        compiler_params=_params("parallel"),
        name="inproj",
    )(x, g, w_qkv, w_rest)


def _dilated_kernel(q_ref, kp_ref, kc_ref, kn_ref, vp_ref, vc_ref, vn_ref, bias_ref, o_ref, lse_ref,
                    k_scr, v_scr, *, tq, n_cls, dil, rg):
    i = pl.program_id(1)
    scale = HEAD ** -0.5
    sub = 2 * BAND_HALF
    win = sub + 2 * BAND_HALF
    lane = lax.broadcasted_iota(jnp.int32, (sub, LANES), 1)
    for rr in range(rg):
        r = pl.program_id(2) * rg + rr
        k_scr[0:BAND_HALF] = kp_ref[rr]
        k_scr[BAND_HALF:BAND_HALF + tq] = kc_ref[rr]
        k_scr[BAND_HALF + tq:] = kn_ref[rr]
        v_scr[0:BAND_HALF] = vp_ref[rr]
        v_scr[BAND_HALF:BAND_HALF + tq] = vc_ref[rr]
        v_scr[BAND_HALF + tq:] = vn_ref[rr]
        for a in range(tq // sub):
            rows = pl.ds(a * sub, sub) if dil == 1 else pl.ds(a * sub * dil + r, sub, stride=dil)
            edge = a == 0 or a == tq // sub - 1
            kidx = i * tq + (a * sub - BAND_HALF) + lax.broadcasted_iota(jnp.int32, (1, win), 1)
            valid = (kidx >= 0) & (kidx < n_cls)
            lse_all = jnp.zeros((sub, LANES), F32)
            for h in range(A_HEADS):
                cols = slice(h * HEAD, (h + 1) * HEAD)
                q = q_ref[rr, a * sub:(a + 1) * sub, cols]
                k = k_scr[a * sub:a * sub + win, cols]
                v = v_scr[a * sub:a * sub + win, cols]
                s = _dot_nt(q, k) * scale + bias_ref[h]
                if edge:
                    s = jnp.where(valid, s, NEG_INF)
                m = jnp.max(s, axis=1, keepdims=True)
                p = jnp.exp(s - m)
                l = jnp.sum(p, axis=1, keepdims=True)
                o_ref[h, rows, :] = _dot(p.astype(BF16), v) / l
                lse_all = jnp.where(lane == h, m + jnp.log(l), lse_all)
            lse_ref[rows, :] = lse_all


def _band_bias(dil):
    slopes = 2.0 ** (-8.0 * jnp.arange(1, A_HEADS + 1, dtype=F32) / A_HEADS)
    sub, win = 2 * BAND_HALF, 4 * BAND_HALF
    rel = jnp.abs(BAND_HALF + jnp.arange(sub)[:, None] - jnp.arange(win)[None, :])
    bias = -slopes[:, None, None] * (dil * rel).astype(F32)[None]
    return jnp.where((rel <= BAND_HALF)[None], bias, NEG_INF)


def _dilated_branch(qkv, dil):
    b, _, n_cls, _ = qkv.shape
    tq = min(512, n_cls, 2048 // dil)
    nq = n_cls // tq
    nh = tq // BAND_HALF
    last_halo = n_cls // BAND_HALF - 1
    rg = min(dil, 512 // tq)

    def cur(which):
        return pl.BlockSpec((None, rg, tq, A_WIDTH), lambda bi, i, r: (bi, r, i, which))

    def prev(which):
        return pl.BlockSpec((None, rg, BAND_HALF, A_WIDTH),
                            lambda bi, i, r: (bi, r, jnp.maximum(i * nh - 1, 0), which))

    def nxt(which):
        return pl.BlockSpec((None, rg, BAND_HALF, A_WIDTH),
                            lambda bi, i, r: (bi, r, jnp.minimum((i + 1) * nh, last_halo), which))

    t = b * n_cls * dil
    return pl.pallas_call(
        functools.partial(_dilated_kernel, tq=tq, n_cls=n_cls, dil=dil, rg=rg),
        out_shape=(jax.ShapeDtypeStruct((A_HEADS, t, HEAD), F32), jax.ShapeDtypeStruct((t, LANES), F32)),
        grid=(b, nq, dil // rg),
        in_specs=[cur(0), prev(1), cur(1), nxt(1), prev(2), cur(2), nxt(2),
                  pl.BlockSpec((A_HEADS, 2 * BAND_HALF, 4 * BAND_HALF), lambda bi, i, r: (0, 0, 0))],
        out_specs=(pl.BlockSpec((A_HEADS, tq * dil, HEAD), lambda bi, i, r: (0, bi * nq + i, 0)),
                   pl.BlockSpec((tq * dil, LANES), lambda bi, i, r: (bi * nq + i, 0))),
        scratch_shapes=[pltpu.VMEM((tq + 2 * BAND_HALF, A_WIDTH), BF16),
                        pltpu.VMEM((tq + 2 * BAND_HALF, A_WIDTH), BF16)],
        compiler_params=_params("parallel", "parallel", "arbitrary"),
        name=f"dilated{dil}",
    )(qkv, qkv, qkv, qkv, qkv, qkv, qkv, _band_bias(dil))


def _rotate(y, tab):
    z = y * tab
    r = z + pltpu.roll(z, QK_ROPE, axis=1)
    lane = lax.broadcasted_iota(jnp.int32, r.shape, 1)
    return jnp.where(lane < QK_ROPE, r, 0.0)


def _mla_proj_kernel(cq_ref, ckv_ref, kr_ref, tab_ref, gq_ref, gkv_ref, wq_ref, wkv_ref,
                     q_ref, k_ref, v_ref):
    cq = _rms(cq_ref[...], gq_ref[...]).astype(BF16)
    ckv = _rms(ckv_ref[...], gkv_ref[...]).astype(BF16)
    tab = tab_ref[...]
    kr = _rotate(kr_ref[...], tab).astype(BF16)
    ones = jnp.ones((v_ref.shape[0], HEAD), BF16)
    for h in range(B_HEADS):
        lo, mid, hi = 2 * h * HEAD, (2 * h + 1) * HEAD, (2 * h + 2) * HEAD
        q = _dot(cq, wq_ref[:, lo:hi])
        q_ref[:, lo:mid] = q[:, :HEAD].astype(BF16)
        q_ref[:, mid:hi] = _rotate(q[:, HEAD:], tab).astype(BF16)
        kv = _dot(ckv, wkv_ref[:, lo:hi])
        k_ref[:, lo:mid] = kv[:, :HEAD].astype(BF16)
        k_ref[:, mid:hi] = kr
        v_ref[:, lo:mid] = kv[:, HEAD:].astype(BF16)
        v_ref[:, mid:hi] = ones


def _mla_proj(rest, tab, gq, gkv, wq, wkv, *, s, tm=512):
    t = rest.shape[0]
    lat = 512
    width = B_HEADS * 2 * HEAD
    pos_blocks = s // tm
    return pl.pallas_call(
        _mla_proj_kernel,
        out_shape=tuple(jax.ShapeDtypeStruct((t, width), BF16) for _ in range(3)),
        grid=(t // tm,),
        in_specs=[
            pl.BlockSpec((tm, lat), lambda i: (i, 0)),
            pl.BlockSpec((tm, lat), lambda i: (i, 1)),
            pl.BlockSpec((tm, LANES), lambda i: (i, 4 * lat // LANES)),
            pl.BlockSpec((tm, LANES), lambda i: (i % pos_blocks, 0)),
            pl.BlockSpec((1, lat), lambda i: (0, 0)),
            pl.BlockSpec((1, lat), lambda i: (0, 0)),
            pl.BlockSpec((lat, width), lambda i: (0, 0)),
            pl.BlockSpec((lat, width), lambda i: (0, 0)),
        ],
        out_specs=tuple(pl.BlockSpec((tm, width), lambda i: (i, 0)) for _ in range(3)),
        compiler_params=_params("parallel"),
        name="mla_proj",
    )(rest, rest, rest, tab, gq, gkv, wq, wkv)


def _mla_attn_kernel(q_ref, k_ref, v_ref, o_ref, m_scr, a_scr, acc_scr, s_scr, p_scr, mx_scr, *, tq, tk, nk):
    c = (HEAD + QK_ROPE) ** -0.5 * math.log2(math.e)
    q = q_ref[...]
    nl = tk // LANES

    def chunk(kk):
        return pl.ds(pl.multiple_of(kk * tk, tk), tk)

    def scores(kk, slot):
        s = _dot_nt(q, k_ref[chunk(kk), :])
        s_scr[slot] = s
        mx = s[:, :LANES]
        for j in range(1, nl):
            mx = jnp.maximum(mx, s[:, j * LANES:(j + 1) * LANES])
        mx_scr[slot] = mx

    def softmax(slot, first):
        s = s_scr[slot]
        m_new = jnp.broadcast_to(jnp.max(mx_scr[slot], axis=1, keepdims=True), (tq, LANES))
        if not first:
            m_old = m_scr[...]
            m_new = jnp.maximum(m_old, m_new)
            a_scr[slot] = jnp.exp2((m_old - m_new) * c)
        for j in range(nl):
            cols = slice(j * LANES, (j + 1) * LANES)
            p_scr[slot, :, cols] = jnp.exp2((s[:, cols] - m_new) * c).astype(BF16)
        m_scr[...] = m_new

    def values(kk, slot, first):
        pv = _dot(p_scr[slot], v_ref[chunk(kk), :])
        if first:
            acc_scr[...] = pv
        else:
            alpha = a_scr[slot]
            acc_scr[...] = jnp.concatenate([alpha, alpha], axis=1) * acc_scr[...] + pv

    scores(0, 0)
    softmax(0, True)
    scores(1, 1)
    softmax(1, False)
    values(0, 0, True)
    scores(2, 0)

    def body(g, carry):
        kk = 2 * g
        softmax(0, False)
        values(kk - 1, 1, False)
        scores(kk + 1, 1)
        softmax(1, False)
        values(kk, 0, False)
        scores(kk + 2, 0)
        return carry

    lax.fori_loop(1, nk // 2 - 1, body, 0)
    softmax(0, False)
    values(nk - 3, 1, False)
    scores(nk - 1, 1)
    softmax(1, False)
    values(nk - 2, 0, False)
    values(nk - 1, 1, False)
    o_ref[...] = acc_scr[:, :HEAD] / acc_scr[:, HEAD:]


def _mla_attn(q, k, v, *, b, s, tq=1024):
    t = b * s
    nq = s // tq
    tk = 512
    nk = s // tk
    assert nk % 2 == 0 and nk >= 4
    return pl.pallas_call(
        functools.partial(_mla_attn_kernel, tq=tq, tk=tk, nk=nk),
        out_shape=jax.ShapeDtypeStruct((t, B_HEADS * HEAD), F32),
        grid=(b, B_HEADS, nq),
        in_specs=[
            pl.BlockSpec((tq, 2 * HEAD), lambda bi, h, i: (bi * nq + i, h)),
            pl.BlockSpec((s, 2 * HEAD), lambda bi, h, i: (bi, h)),
            pl.BlockSpec((s, 2 * HEAD), lambda bi, h, i: (bi, h)),
        ],
        out_specs=pl.BlockSpec((tq, HEAD), lambda bi, h, i: (bi * nq + i, h)),
        scratch_shapes=[pltpu.VMEM((tq, LANES), F32), pltpu.VMEM((2, tq, LANES), F32),
                        pltpu.VMEM((tq, 2 * HEAD), F32), pltpu.VMEM((2, tq, tk), F32),
                        pltpu.VMEM((2, tq, tk), BF16), pltpu.VMEM((2, tq, LANES), F32)],
        compiler_params=_params("parallel", "parallel", "arbitrary"),
        name="mla_attn",
    )(q, k, v)


def _softplus(x):
    return jnp.maximum(x, 0.0) + jnp.log1p(jnp.exp(-jnp.abs(x)))


def _scan_tile(a, b, row, reverse):
    for sh in (1, 2, 4):
        if reverse:
            keep = row < SUBLANES - sh
            a_s = pltpu.roll(a, SUBLANES - sh, axis=0)
            b_s = pltpu.roll(b, SUBLANES - sh, axis=0)
        else:
            keep = row >= sh
            a_s = pltpu.roll(a, sh, axis=0)
            b_s = pltpu.roll(b, sh, axis=0)
        b = b + a * jnp.where(keep, b_s, 0.0)
        a = a * jnp.where(keep, a_s, 1.0)
    return a, b


def _rglru_fwd_kernel(uc_ref, up_ref, un_ref, cw_ref, cb_ref, wg_ref, bg_ref, lam_ref,
                      hf_ref, ab_ref, bb_ref, ubuf, af_scr, bf_scr, carry, *, ts):
    i = pl.program_id(1)
    ns = pl.num_programs(1)
    ubuf[0:SUBLANES] = jnp.where(i > 0, up_ref[...], 0.0)
    ubuf[SUBLANES:SUBLANES + ts] = uc_ref[...]
    ubuf[SUBLANES + ts:] = jnp.where(i < ns - 1, un_ref[...], 0.0)
    u = cb_ref[...] + cw_ref[0:1, :] * ubuf[SUBLANES - 2:SUBLANES - 2 + ts]
    u = u + cw_ref[1:2, :] * ubuf[SUBLANES - 1:SUBLANES - 1 + ts]
    u = u + cw_ref[2:3, :] * ubuf[SUBLANES:SUBLANES + ts]
    u = u + cw_ref[3:4, :] * ubuf[SUBLANES + 1:SUBLANES + 1 + ts]
    gates = _dot(u.astype(BF16), wg_ref[...]) + bg_ref[...]
    sp = _softplus(-lam_ref[...])
    for d in range(2):
        r = jax.nn.sigmoid(gates[:, (2 * d) * C_WIDTH:(2 * d + 1) * C_WIDTH])
        ig = jax.nn.sigmoid(gates[:, (2 * d + 1) * C_WIDTH:(2 * d + 2) * C_WIDTH])
        log_a = -RG_C * r * sp[d:d + 1, :]
        a = jnp.exp(log_a)
        bterm = jnp.sqrt(-jnp.tanh(log_a) * (a * a + 1.0)) * (ig * u)
        if d == 0:
            af_scr[...] = a
            bf_scr[...] = bterm
        else:
            ab_ref[...] = a
            bb_ref[...] = bterm

    @pl.when(i == 0)
    def _():
        carry[...] = jnp.zeros_like(carry)

    row = lax.broadcasted_iota(jnp.int32, (SUBLANES, C_WIDTH), 0)

    def body(t, c):
        rows = pl.ds(pl.multiple_of(t * SUBLANES, SUBLANES), SUBLANES)
        a, b = _scan_tile(af_scr[rows, :], bf_scr[rows, :], row, reverse=False)
        h = b + a * carry[...]
        hf_ref[rows, :] = h
        carry[...] = h[SUBLANES - 1:SUBLANES, :]
        return c

    lax.fori_loop(0, ts // SUBLANES, body, 0)


def _rglru_bwd_kernel(a_ref, b_ref, hf_ref, g_ref, y_ref, carry, *, ts):
    @pl.when(pl.program_id(1) == 0)
    def _():
        carry[...] = jnp.zeros_like(carry)

    row = lax.broadcasted_iota(jnp.int32, (SUBLANES, C_WIDTH), 0)
    nt = ts // SUBLANES

    def body(t, c):
        rows = pl.ds(pl.multiple_of((nt - 1 - t) * SUBLANES, SUBLANES), SUBLANES)
        a, b = _scan_tile(a_ref[rows, :], b_ref[rows, :], row, reverse=True)
        h = b + a * carry[...]
        carry[...] = h[0:1, :]
        y_ref[rows, :] = jax.nn.gelu(g_ref[rows, :]) * (hf_ref[rows, :] + h)
        return c

    lax.fori_loop(0, nt, body, 0)


def _rglru(rest, conv_w, conv_b, w_gates, b_gates, lam, *, b, s, ts=512):
    t = b * s
    ns = s // ts
    hb = ts // SUBLANES
    last = s // SUBLANES - 1
    u_col, g_col = 2, 3
    tile = lambda bi, i: (bi * ns + i, 0)
    hf, ab, bb = pl.pallas_call(
        functools.partial(_rglru_fwd_kernel, ts=ts),
        out_shape=tuple(jax.ShapeDtypeStruct((t, C_WIDTH), F32) for _ in range(3)),
        grid=(b, ns),
        in_specs=[
            pl.BlockSpec((ts, C_WIDTH), lambda bi, i: (bi * ns + i, u_col)),
            pl.BlockSpec((SUBLANES, C_WIDTH),
                         lambda bi, i: (bi * ns * hb + jnp.maximum(i * hb - 1, 0), u_col)),
            pl.BlockSpec((SUBLANES, C_WIDTH),
                         lambda bi, i: (bi * ns * hb + jnp.minimum((i + 1) * hb, last), u_col)),
            pl.BlockSpec((4, C_WIDTH), lambda bi, i: (0, 0)),
            pl.BlockSpec((1, C_WIDTH), lambda bi, i: (0, 0)),
            pl.BlockSpec((C_WIDTH, 4 * C_WIDTH), lambda bi, i: (0, 0)),
            pl.BlockSpec((1, 4 * C_WIDTH), lambda bi, i: (0, 0)),
            pl.BlockSpec((2, C_WIDTH), lambda bi, i: (0, 0)),
        ],
        out_specs=tuple(pl.BlockSpec((ts, C_WIDTH), tile) for _ in range(3)),
        scratch_shapes=[pltpu.VMEM((ts + 2 * SUBLANES, C_WIDTH), F32),
                        pltpu.VMEM((ts, C_WIDTH), F32), pltpu.VMEM((ts, C_WIDTH), F32),
                        pltpu.VMEM((1, C_WIDTH), F32)],
        compiler_params=_params("parallel", "arbitrary"),
        name="rglru_fwd",
    )(rest, rest, rest, conv_w, conv_b, w_gates, b_gates, lam)
    rev = lambda bi, i: (bi * ns + ns - 1 - i, 0)
    return pl.pallas_call(
        functools.partial(_rglru_bwd_kernel, ts=ts),
        out_shape=jax.ShapeDtypeStruct((t, C_WIDTH), F32),
        grid=(b, ns),
        in_specs=[
            pl.BlockSpec((ts, C_WIDTH), rev),
            pl.BlockSpec((ts, C_WIDTH), rev),
            pl.BlockSpec((ts, C_WIDTH), rev),
            pl.BlockSpec((ts, C_WIDTH), lambda bi, i: (bi * ns + ns - 1 - i, g_col)),
        ],
        out_specs=pl.BlockSpec((ts, C_WIDTH), rev),
        scratch_shapes=[pltpu.VMEM((1, C_WIDTH), F32)],
        compiler_params=_params("parallel", "arbitrary"),
        name="rglru_bwd",
    )(ab, bb, hf, rest)


def _outproj_kernel(x_ref, o1_ref, o2_ref, o3_ref, l1_ref, l2_ref, l3_ref, yb_ref, yc_ref,
                    ga_ref, gb_ref, gc_ref, w_ref, out_ref):
    l1, l2, l3 = l1_ref[...], l2_ref[...], l3_ref[...]
    m = jnp.maximum(jnp.maximum(l1, l2), l3)
    e1, e2, e3 = jnp.exp(l1 - m), jnp.exp(l2 - m), jnp.exp(l3 - m)
    z = e1 + e2 + e3
    w1, w2, w3 = e1 / z, e2 / z, e3 / z
    parts = []
    for h in range(A_HEADS):
        parts.append(w1[:, h:h + 1] * o1_ref[h] + w2[:, h:h + 1] * o2_ref[h] + w3[:, h:h + 1] * o3_ref[h])
    ya = jnp.concatenate(parts, axis=1)
    b0, c0 = A_WIDTH, 2 * A_WIDTH
    y = _dot(_rms(ya, ga_ref[...]).astype(BF16), w_ref[0:b0, :])
    y = y + _dot(_rms(yb_ref[...], gb_ref[...]).astype(BF16), w_ref[b0:c0, :])
    y = y + _dot(_rms(yc_ref[...], gc_ref[...]).astype(BF16), w_ref[c0:, :])
    out_ref[...] = x_ref[...] + y


def _outproj(x, o_branches, lse_branches, yb, yc, ga, gb, gc, w, *, tm=256):
    t, d = x.shape
    row = lambda n: pl.BlockSpec((tm, n), lambda i: (i, 0))
    full = lambda r, n: pl.BlockSpec((r, n), lambda i: (0, 0))
    heads = pl.BlockSpec((A_HEADS, tm, HEAD), lambda i: (0, i, 0))
    return pl.pallas_call(
        _outproj_kernel,
        out_shape=jax.ShapeDtypeStruct((t, d), F32),
        grid=(t // tm,),
        in_specs=[row(d), heads, heads, heads, row(LANES), row(LANES), row(LANES),
                  row(A_WIDTH), row(C_WIDTH), full(1, A_WIDTH), full(1, A_WIDTH), full(1, C_WIDTH),
                  full(d, d)],
        out_specs=row(d),
        compiler_params=_params("parallel"),
        name="outproj",
    )(x, *o_branches, *lse_branches, yb, yc, ga, gb, gc, w)


def _mem_kv_kernel(m_ref, g_ref, w_ref, o_ref):
    o_ref[...] = _dot(_rms(m_ref[...], g_ref[...]).astype(BF16), w_ref[...]).astype(BF16)


def _mem_kv(mem, g, w_kv):
    t, d = mem.shape
    n = w_kv.shape[1]
    tm = 256
    return pl.pallas_call(
        _mem_kv_kernel,
        out_shape=jax.ShapeDtypeStruct((t, n), BF16),
        grid=(t // tm,),
        in_specs=[pl.BlockSpec((tm, d), lambda i: (i, 0)), pl.BlockSpec((1, d), lambda i: (0, 0)),
                  pl.BlockSpec((d, n), lambda i: (0, 0))],
        out_specs=pl.BlockSpec((tm, n), lambda i: (i, 0)),
        compiler_params=_params("parallel"),
        name="mem_kv",
    )(mem, g, w_kv)


def _xattn_kernel(x_ref, g_ref, wq_ref, kv_ref, wo_ref, o_ref):
    x = x_ref[...]
    q = _dot(_rms(x, g_ref[...]).astype(BF16), wq_ref[...]).astype(BF16)
    scale = HEAD ** -0.5
    outs = []
    for h in range(X_HEADS):
        k = kv_ref[:, h * HEAD:(h + 1) * HEAD]
        v = kv_ref[:, X_WIDTH + h * HEAD:X_WIDTH + (h + 1) * HEAD]
        s = _dot_nt(q[:, h * HEAD:(h + 1) * HEAD], k) * scale
        p = jnp.exp(s - jnp.max(s, axis=1, keepdims=True))
        l = jnp.sum(p, axis=1, keepdims=True)
        outs.append(_dot(p.astype(BF16), v) / l)
    o = jnp.concatenate(outs, axis=1).astype(BF16)
    o_ref[...] = x + _dot(o, wo_ref[...])


def _xattn(x, g, wq, kv, wo, *, s, n_mem, tm=512):
    t, d = x.shape
    per_seq = s // tm
    return pl.pallas_call(
        _xattn_kernel,
        out_shape=jax.ShapeDtypeStruct((t, d), F32),
        grid=(t // tm,),
        in_specs=[
            pl.BlockSpec((tm, d), lambda i: (i, 0)),
            pl.BlockSpec((1, d), lambda i: (0, 0)),
            pl.BlockSpec((d, X_WIDTH), lambda i: (0, 0)),
            pl.BlockSpec((n_mem, 2 * X_WIDTH), lambda i: (i // per_seq, 0)),
            pl.BlockSpec((X_WIDTH, d), lambda i: (0, 0)),
        ],
        out_specs=pl.BlockSpec((tm, d), lambda i: (i, 0)),
        compiler_params=_params("parallel"),
        name="xattn",
    )(x, g, wq, kv, wo)


def _rope_table(s):
    inv = ROPE_THETA ** (-jnp.arange(0, QK_ROPE, 2, dtype=F32) / QK_ROPE)
    ang = jnp.arange(s, dtype=F32)[:, None] * inv[None, :]
    cos, sin = jnp.cos(ang), jnp.sin(ang)
    return jnp.concatenate([cos, cos, -sin, sin], axis=1)


def _swap_halves(w):
    half = w.shape[-1] // 2
    return jnp.concatenate([w[..., half:], w[..., :half]], axis=-1)


def _prep_layer(p, l):
    row = lambda v: v.reshape(1, -1)
    w_in = p['w_in'][l]
    qkv_end, cq_end, ckv_end, kr_end, u_end = 2304, 2816, 3328, 3392, 3904
    w_rope = w_in[:, ckv_end:kr_end]
    w_rest = jnp.concatenate([w_in[:, qkv_end:ckv_end], w_in[:, kr_end:], w_rope, _swap_halves(w_rope)], axis=1)
    wq = p['w_q_up'][l].reshape(-1, B_HEADS, HEAD + QK_ROPE)
    wq = jnp.concatenate([wq, _swap_halves(wq[..., HEAD:])], axis=-1).reshape(-1, B_HEADS * 2 * HEAD)
    eye = jnp.eye(C_BLOCKS, dtype=F32)
    dense = lambda w: jnp.einsum('ncd,nm->ncmd', w, eye).reshape(C_WIDTH, C_WIDTH)
    w_r, w_i = p['w_rg_r'][l], p['w_rg_i'][l]
    w_gates = jnp.concatenate([dense(w_r[0]), dense(w_i[0]), dense(w_r[1]), dense(w_i[1])], axis=1)
    b_r, b_i = p['b_rg_r'][l], p['b_rg_i'][l]
    b_gates = jnp.concatenate([b_r[0], b_i[0], b_r[1], b_i[1]]).reshape(1, -1)
    bf = lambda w: w.astype(BF16)
    return dict(
        g_ffn1=row(p['g_ffn1'][l]), w1_gate=bf(p['w1_gate'][l]), w1_up=bf(p['w1_up'][l]), w1_down=bf(p['w1_down'][l]),
        g_mix=row(p['g_mix'][l]), w_qkv=bf(w_in[:, :qkv_end]), w_rest=bf(w_rest),
        g_q_lat=row(p['g_q_lat'][l]), g_kv_lat=row(p['g_kv_lat'][l]), w_q=bf(wq), w_kv=bf(p['w_kv_up'][l]),
        conv_w=p['conv_w'][l], conv_b=row(p['conv_b'][l]), w_gates=bf(w_gates), b_gates=b_gates,
        lam=p['rg_lambda'][l],
        g_out_a=row(p['g_out_a'][l]), g_out_b=row(p['g_out_b'][l]), g_out_c=row(p['g_out_c'][l]),
        w_out=bf(p['w_out'][l]),
        g_xattn=row(p['g_xattn'][l]), g_mem=row(p['g_mem'][l]), w_xq=bf(p['w_xq'][l]),
        w_xkv=bf(jnp.concatenate([p['w_xk'][l], p['w_xv'][l]], axis=1)), w_xo=bf(p['w_xo'][l]),
        g_ffn2=row(p['g_ffn2'][l]), w2_gate=bf(p['w2_gate'][l]), w2_up=bf(p['w2_up'][l]), w2_down=bf(p['w2_down'][l]),
    )


def _trunk(x, mem, layers, g_final):
    b, s, d = x.shape
    n_mem = mem.shape[1]
    x = x.reshape(b * s, d)
    mem = mem.reshape(b * n_mem, d)
    tab = _rope_table(s)
    for l, w in enumerate(layers):
        x = _ffn(x, w['g_ffn1'], w['w1_gate'], w['w1_up'], w['w1_down'], g_final, final_norm=False)
        *qkv_by_dil, rest = _inproj(x, w['g_mix'], w['w_qkv'], w['w_rest'], b=b, s=s)
        branches = [_dilated_branch(qkv, dil) for qkv, (_, dil) in zip(qkv_by_dil, DILATED_CONFIGS)]
        q, k, v = _mla_proj(rest, tab, w['g_q_lat'], w['g_kv_lat'], w['w_q'], w['w_kv'], s=s)
        yb = _mla_attn(q, k, v, b=b, s=s)
        yc = _rglru(rest, w['conv_w'], w['conv_b'], w['w_gates'], w['b_gates'], w['lam'], b=b, s=s)
        x = _outproj(x, [o for o, _ in branches], [lse for _, lse in branches], yb, yc,
                     w['g_out_a'], w['g_out_b'], w['g_out_c'], w['w_out'])
        kv = _mem_kv(mem, w['g_mem'], w['w_xkv'])
        x = _xattn(x, w['g_xattn'], w['w_xq'], kv, w['w_xo'], s=s, n_mem=n_mem)
        x = _ffn(x, w['g_ffn2'], w['w2_gate'], w['w2_up'], w['w2_down'], g_final,
                 final_norm=(l == len(layers) - 1))
    return x.reshape(b, s, d)


def kernel(x_prompt, x_sample, mem_prompt, mem_sample, g_ffn1, w1_gate, w1_up, w1_down, g_mix, w_in, g_q_lat, w_q_up, g_kv_lat, w_kv_up, conv_w, conv_b, w_rg_r, b_rg_r, w_rg_i, b_rg_i, rg_lambda, g_out_a, g_out_b, g_out_c, w_out, g_xattn, g_mem, w_xq, w_xk, w_xv, w_xo, g_ffn2, w2_gate, w2_up, w2_down, g_final):
    p = dict(g_ffn1=g_ffn1, w1_gate=w1_gate, w1_up=w1_up, w1_down=w1_down,
             g_mix=g_mix, w_in=w_in, g_q_lat=g_q_lat, w_q_up=w_q_up, g_kv_lat=g_kv_lat, w_kv_up=w_kv_up,
             conv_w=conv_w, conv_b=conv_b, w_rg_r=w_rg_r, b_rg_r=b_rg_r, w_rg_i=w_rg_i, b_rg_i=b_rg_i,
             rg_lambda=rg_lambda, g_out_a=g_out_a, g_out_b=g_out_b, g_out_c=g_out_c, w_out=w_out,
             g_xattn=g_xattn, g_mem=g_mem, w_xq=w_xq, w_xk=w_xk, w_xv=w_xv, w_xo=w_xo,
             g_ffn2=g_ffn2, w2_gate=w2_gate, w2_up=w2_up, w2_down=w2_down)
    layers = [_prep_layer(p, l) for l in range(g_ffn1.shape[0])]
    gf = g_final.reshape(1, -1)
    return (_trunk(x_prompt, mem_prompt, layers, gf), _trunk(x_sample, mem_sample, layers, gf))
```

```python
import functools
import math

import jax
import jax.numpy as jnp
from jax import lax
from jax.experimental import pallas as pl
from jax.experimental.pallas import tpu as pltpu

BF16 = jnp.bfloat16
F32 = jnp.float32

D_MODEL = 2048
A_HEADS = 6
HEAD = 128
A_WIDTH = A_HEADS * HEAD
DILATED_CONFIGS = ((128, 1), (512, 4), (2048, 16))
B_HEADS = 6
QK_ROPE = 64
ROPE_THETA = 10000.0
C_WIDTH = 512
C_BLOCKS = 8
C_BLOCK_W = 64
RG_C = 8.0
X_HEADS = 4
X_WIDTH = 512
D_FF = 5632
EPS = 1e-6
NEG_INF = -1e30

V7X_VMEM_LIMIT_BYTES = 56 * 1024 * 1024
LANES = 128
SUBLANES = 8

REST_WIDTH = 4 * 512 + LANES
BAND_HALF = 64


def _params(*sem):
    return pltpu.CompilerParams(dimension_semantics=sem, vmem_limit_bytes=V7X_VMEM_LIMIT_BYTES)


def _rms(x, g):
    return x * lax.rsqrt(jnp.mean(x * x, axis=-1, keepdims=True) + EPS) * g


def _dot(a, b):
    return jnp.dot(a, b, preferred_element_type=F32)


def _dot_nt(a, b):
    return lax.dot_general(a, b, (((1,), (1,)), ((), ())), preferred_element_type=F32)


def _ffn_kernel(x_ref, g_ref, wg_ref, wu_ref, wd_ref, gf_ref, o_ref, h_scr, *, final_norm):
    j = pl.program_id(1)

    @pl.when(j == 0)
    def _():
        h_scr[...] = _rms(x_ref[...], g_ref[...]).astype(BF16)
        o_ref[...] = jnp.zeros_like(o_ref)

    h = h_scr[...]
    a = _dot(h, wg_ref[...])
    u = _dot(h, wu_ref[...])
    o_ref[...] += _dot((jax.nn.silu(a) * u).astype(BF16), wd_ref[...])

    @pl.when(j == pl.num_programs(1) - 1)
    def _():
        rows_per_pass = 256

        def finish(r, carry):
            rows = pl.ds(pl.multiple_of(r * rows_per_pass, rows_per_pass), rows_per_pass)
            y = x_ref[rows, :] + 0.5 * o_ref[rows, :]
            if final_norm:
                y = _rms(y, gf_ref[...])
            o_ref[rows, :] = y
            return carry

        lax.fori_loop(0, o_ref.shape[0] // rows_per_pass, finish, 0)


def _ffn(x, g, wg, wu, wd, gf, *, final_norm, tm=1024, tf=512):
    t, d = x.shape
    f = wg.shape[1]
    return pl.pallas_call(
        functools.partial(_ffn_kernel, final_norm=final_norm),
        out_shape=jax.ShapeDtypeStruct((t, d), F32),
        grid=(t // tm, f // tf),
        in_specs=[
            pl.BlockSpec((tm, d), lambda i, j: (i, 0)),
            pl.BlockSpec((1, d), lambda i, j: (0, 0)),
            pl.BlockSpec((d, tf), lambda i, j: (0, j)),
            pl.BlockSpec((d, tf), lambda i, j: (0, j)),
            pl.BlockSpec((tf, d), lambda i, j: (j, 0)),
            pl.BlockSpec((1, d), lambda i, j: (0, 0)),
        ],
        out_specs=pl.BlockSpec((tm, d), lambda i, j: (i, 0)),
        scratch_shapes=[pltpu.VMEM((tm, d), BF16)],
        compiler_params=_params("parallel", "arbitrary"),
        name="ffn",
    )(x, g, wg, wu, wd, gf)


def _inproj_kernel(x_ref, g_ref, w1_ref, w2_ref, o1_ref, o4_ref, o16_ref, o2_ref, y_scr, c4_scr, *, tm):
    h = _rms(x_ref[...], g_ref[...]).astype(BF16)
    y = _dot(h, w1_ref[...])
    o1_ref[0] = y.astype(BF16)
    n4 = tm // 4
    for cb in range(y.shape[1] // LANES):
        cols = slice(cb * LANES, (cb + 1) * LANES)
        y_scr[cb] = y[:, cols]
        for r4 in range(4):
            c4 = y_scr[cb, pl.ds(r4, n4, stride=4), :]
            o4_ref[r4, :, cols] = c4.astype(BF16)
            c4_scr[cb, r4] = c4
            for j in range(4):
                o16_ref[r4 + 4 * j, :, cols] = c4_scr[cb, r4, pl.ds(j, n4 // 4, stride=4), :].astype(BF16)
    o2_ref[...] = _dot(h, w2_ref[...])


def _inproj(x, g, w_qkv, w_rest, *, b, s, tm=256):
    t, d = x.shape
    n1, n2 = w_qkv.shape[1], w_rest.shape[1]
    per_seq = s // tm
    dils = [dil for _, dil in DILATED_CONFIGS]
    cls_shape = lambda dil: jax.ShapeDtypeStruct((b, dil, s // dil, n1), BF16)
    cls_spec = lambda dil: pl.BlockSpec((None, dil, tm // dil, n1), lambda i: (i // per_seq, 0, i % per_seq, 0))
    resident = dict(pipeline_mode=pl.Buffered(1))
    return pl.pallas_call(
        functools.partial(_inproj_kernel, tm=tm),
        out_shape=(*[cls_shape(dil) for dil in dils], jax.ShapeDtypeStruct((t, n2), F32)),
        grid=(t // tm,),
        in_specs=[
            pl.BlockSpec((tm, d), lambda i: (i, 0)),
            pl.BlockSpec((1, d), lambda i: (0, 0)),
            pl.BlockSpec((d, n1), lambda i: (0, 0), **resident),
            pl.BlockSpec((d, n2), lambda i: (0, 0), **resident),
        ],
        out_specs=(*[cls_spec(dil) for dil in dils], pl.BlockSpec((tm, n2), lambda i: (i, 0))),
        scratch_shapes=[pltpu.VMEM((n1 // LANES, tm, LANES), F32),
                        pltpu.VMEM((n1 // LANES, 4, tm // 4, LANES), F32)],
        compiler_params=_params("parallel"),
        name="inproj",
    )(x, g, w_qkv, w_rest)


def _dilated_kernel(q_ref, kp_ref, kc_ref, kn_ref, vp_ref, vc_ref, vn_ref, bias_ref, o_ref, lse_ref,
                    k_scr, v_scr, *, tq, n_cls, dil, rg):
    i = pl.program_id(1)
    scale = HEAD ** -0.5
    sub = 2 * BAND_HALF
    win = sub + 2 * BAND_HALF
    lane = lax.broadcasted_iota(jnp.int32, (sub, LANES), 1)
    for rr in range(rg):
        r = pl.program_id(2) * rg + rr
        k_scr[0:BAND_HALF] = kp_ref[rr]
        k_scr[BAND_HALF:BAND_HALF + tq] = kc_ref[rr]
        k_scr[BAND_HALF + tq:] = kn_ref[rr]
        v_scr[0:BAND_HALF] = vp_ref[rr]
        v_scr[BAND_HALF:BAND_HALF + tq] = vc_ref[rr]
        v_scr[BAND_HALF + tq:] = vn_ref[rr]
        for a in range(tq // sub):
            rows = pl.ds(a * sub, sub) if dil == 1 else pl.ds(a * sub * dil + r, sub, stride=dil)
            edge = a == 0 or a == tq // sub - 1
            kidx = i * tq + (a * sub - BAND_HALF) + lax.broadcasted_iota(jnp.int32, (1, win), 1)
            valid = (kidx >= 0) & (kidx < n_cls)
            lse_all = jnp.zeros((sub, LANES), F32)
            for h in range(A_HEADS):
                cols = slice(h * HEAD, (h + 1) * HEAD)
                q = q_ref[rr, a * sub:(a + 1) * sub, cols]
                k = k_scr[a * sub:a * sub + win, cols]
                v = v_scr[a * sub:a * sub + win, cols]
                s = _dot_nt(q, k) * scale + bias_ref[h]
                if edge:
                    s = jnp.where(valid, s, NEG_INF)
                m = jnp.max(s, axis=1, keepdims=True)
                p = jnp.exp(s - m)
                l = jnp.sum(p, axis=1, keepdims=True)
                o_ref[h, rows, :] = _dot(p.astype(BF16), v) / l
                lse_all = jnp.where(lane == h, m + jnp.log(l), lse_all)
            lse_ref[rows, :] = lse_all


def _band_bias(dil):
    slopes = 2.0 ** (-8.0 * jnp.arange(1, A_HEADS + 1, dtype=F32) / A_HEADS)
    sub, win = 2 * BAND_HALF, 4 * BAND_HALF
    rel = jnp.abs(BAND_HALF + jnp.arange(sub)[:, None] - jnp.arange(win)[None, :])
    bias = -slopes[:, None, None] * (dil * rel).astype(F32)[None]
    return jnp.where((rel <= BAND_HALF)[None], bias, NEG_INF)


def _dilated_branch(qkv, dil):
    b, _, n_cls, _ = qkv.shape
    tq = min(512, n_cls, 2048 // dil)
    nq = n_cls // tq
    nh = tq // BAND_HALF
    last_halo = n_cls // BAND_HALF - 1
    rg = min(dil, 512 // tq)

    def cur(which):
        return pl.BlockSpec((None, rg, tq, A_WIDTH), lambda bi, i, r: (bi, r, i, which))

    def prev(which):
        return pl.BlockSpec((None, rg, BAND_HALF, A_WIDTH),
                            lambda bi, i, r: (bi, r, jnp.maximum(i * nh - 1, 0), which))

    def nxt(which):
        return pl.BlockSpec((None, rg, BAND_HALF, A_WIDTH),
                            lambda bi, i, r: (bi, r, jnp.minimum((i + 1) * nh, last_halo), which))

    t = b * n_cls * dil
    return pl.pallas_call(
        functools.partial(_dilated_kernel, tq=tq, n_cls=n_cls, dil=dil, rg=rg),
        out_shape=(jax.ShapeDtypeStruct((A_HEADS, t, HEAD), F32), jax.ShapeDtypeStruct((t, LANES), F32)),
        grid=(b, nq, dil // rg),
        in_specs=[cur(0), prev(1), cur(1), nxt(1), prev(2), cur(2), nxt(2),
                  pl.BlockSpec((A_HEADS, 2 * BAND_HALF, 4 * BAND_HALF), lambda bi, i, r: (0, 0, 0))],
        out_specs=(pl.BlockSpec((A_HEADS, tq * dil, HEAD), lambda bi, i, r: (0, bi * nq + i, 0)),
                   pl.BlockSpec((tq * dil, LANES), lambda bi, i, r: (bi * nq + i, 0))),
        scratch_shapes=[pltpu.VMEM((tq + 2 * BAND_HALF, A_WIDTH), BF16),
                        pltpu.VMEM((tq + 2 * BAND_HALF, A_WIDTH), BF16)],
        compiler_params=_params("parallel", "parallel", "arbitrary"),
        name=f"dilated{dil}",
    )(qkv, qkv, qkv, qkv, qkv, qkv, qkv, _band_bias(dil))


def _rotate(y, tab):
    z = y * tab
    r = z + pltpu.roll(z, QK_ROPE, axis=1)
    lane = lax.broadcasted_iota(jnp.int32, r.shape, 1)
    return jnp.where(lane < QK_ROPE, r, 0.0)


def _mla_proj_kernel(cq_ref, ckv_ref, kr_ref, tab_ref, gq_ref, gkv_ref, wq_ref, wkv_ref,
                     q_ref, kt_ref, v_ref):
    cq = _rms(cq_ref[...], gq_ref[...]).astype(BF16)
    ckv = _rms(ckv_ref[...], gkv_ref[...]).astype(BF16)
    tab = tab_ref[...]
    kr_t = _rotate(kr_ref[...], tab).T.astype(BF16)
    ones = jnp.ones((v_ref.shape[0], HEAD), BF16)
    for h in range(B_HEADS):
        lo, mid, hi = 2 * h * HEAD, (2 * h + 1) * HEAD, (2 * h + 2) * HEAD
        q = _dot(cq, wq_ref[:, lo:hi])
        q_ref[:, lo:mid] = q[:, :HEAD].astype(BF16)
        q_ref[:, mid:hi] = _rotate(q[:, HEAD:], tab).astype(BF16)
        kv = _dot(ckv, wkv_ref[:, lo:hi])
        kt_ref[h, :HEAD, :] = kv[:, :HEAD].T.astype(BF16)
        kt_ref[h, HEAD:, :] = kr_t
        v_ref[:, lo:mid] = kv[:, HEAD:].astype(BF16)
        v_ref[:, mid:hi] = ones


def _mla_proj(rest, tab, gq, gkv, wq, wkv, *, s, tm=512):
    t = rest.shape[0]
    lat = 512
    width = B_HEADS * 2 * HEAD
    pos_blocks = s // tm
    return pl.pallas_call(
        _mla_proj_kernel,
        out_shape=(jax.ShapeDtypeStruct((t, width), BF16),
                   jax.ShapeDtypeStruct((B_HEADS, t // tm, 2 * HEAD, tm), BF16),
                   jax.ShapeDtypeStruct((t, width), BF16)),
        grid=(t // tm,),
        in_specs=[
            pl.BlockSpec((tm, lat), lambda i: (i, 0)),
            pl.BlockSpec((tm, lat), lambda i: (i, 1)),
            pl.BlockSpec((tm, LANES), lambda i: (i, 4 * lat // LANES)),
            pl.BlockSpec((tm, LANES), lambda i: (i % pos_blocks, 0)),
            pl.BlockSpec((1, lat), lambda i: (0, 0)),
            pl.BlockSpec((1, lat), lambda i: (0, 0)),
            pl.BlockSpec((lat, width), lambda i: (0, 0)),
            pl.BlockSpec((lat, width), lambda i: (0, 0)),
        ],
        out_specs=(pl.BlockSpec((tm, width), lambda i: (i, 0)),
                   pl.BlockSpec((B_HEADS, None, 2 * HEAD, tm), lambda i: (0, i, 0, 0)),
                   pl.BlockSpec((tm, width), lambda i: (i, 0))),
        compiler_params=_params("parallel"),
        name="mla_proj",
    )(rest, rest, rest, tab, gq, gkv, wq, wkv)


def _mla_attn_kernel(q_ref, kt_ref, v_ref, o_ref, m_scr, a_scr, acc_scr, s_scr, p_scr, mx_scr, *, tq, tk, nk):
    c = (HEAD + QK_ROPE) ** -0.5 * math.log2(math.e)
    q = q_ref[...]
    nl = tk // LANES

    def chunk(kk):
        return pl.ds(pl.multiple_of(kk * tk, tk), tk)

    def scores(kk, slot):
        s = _dot(q, kt_ref[kk])
        s_scr[slot] = s
        mx = s[:, :LANES]
        for j in range(1, nl):
            mx = jnp.maximum(mx, s[:, j * LANES:(j + 1) * LANES])
        mx_scr[slot] = mx

    def softmax(slot, first):
        s = s_scr[slot]
        m_new = jnp.broadcast_to(jnp.max(mx_scr[slot], axis=1, keepdims=True), (tq, LANES))
        if not first:
            m_old = m_scr[...]
            m_new = jnp.maximum(m_old, m_new)
            a_scr[slot] = jnp.exp2((m_old - m_new) * c)
        for j in range(nl):
            cols = slice(j * LANES, (j + 1) * LANES)
            p_scr[slot, :, cols] = jnp.exp2((s[:, cols] - m_new) * c).astype(BF16)
        m_scr[...] = m_new

    def values(kk, slot, first):
        pv = _dot(p_scr[slot], v_ref[chunk(kk), :])
        if first:
            acc_scr[...] = pv
        else:
            alpha = a_scr[slot]
            acc_scr[...] = jnp.concatenate([alpha, alpha], axis=1) * acc_scr[...] + pv

    scores(0, 0)
    softmax(0, True)
    scores(1, 1)
    softmax(1, False)
    values(0, 0, True)
    scores(2, 0)

    def body(g, carry):
        kk = 2 * g
        softmax(0, False)
        values(kk - 1, 1, False)
        scores(kk + 1, 1)
        softmax(1, False)
        values(kk, 0, False)
        scores(kk + 2, 0)
        return carry

    lax.fori_loop(1, nk // 2 - 1, body, 0)
    softmax(0, False)
    values(nk - 3, 1, False)
    scores(nk - 1, 1)
    softmax(1, False)
    values(nk - 2, 0, False)
    values(nk - 1, 1, False)
    o_ref[...] = acc_scr[:, :HEAD] / acc_scr[:, HEAD:]


def _mla_attn(q, kt, v, *, b, s, tq=1024):
    t = b * s
    nq = s // tq
    tk = kt.shape[-1]
    nk = s // tk
    assert nk % 2 == 0 and nk >= 4
    return pl.pallas_call(
        functools.partial(_mla_attn_kernel, tq=tq, tk=tk, nk=nk),
        out_shape=jax.ShapeDtypeStruct((t, B_HEADS * HEAD), F32),
        grid=(b, B_HEADS, nq),
        in_specs=[
            pl.BlockSpec((tq, 2 * HEAD), lambda bi, h, i: (bi * nq + i, h)),
            pl.BlockSpec((None, nk, 2 * HEAD, tk), lambda bi, h, i: (h, bi, 0, 0)),
            pl.BlockSpec((s, 2 * HEAD), lambda bi, h, i: (bi, h)),
        ],
        out_specs=pl.BlockSpec((tq, HEAD), lambda bi, h, i: (bi * nq + i, h)),
        scratch_shapes=[pltpu.VMEM((tq, LANES), F32), pltpu.VMEM((2, tq, LANES), F32),
                        pltpu.VMEM((tq, 2 * HEAD), F32), pltpu.VMEM((2, tq, tk), F32),
                        pltpu.VMEM((2, tq, tk), BF16), pltpu.VMEM((2, tq, LANES), F32)],
        compiler_params=_params("parallel", "parallel", "arbitrary"),
        name="mla_attn",
    )(q, kt, v)


def _softplus(x):
    return jnp.maximum(x, 0.0) + jnp.log1p(jnp.exp(-jnp.abs(x)))


def _scan_tile(a, b, row, reverse):
    for sh in (1, 2, 4):
        if reverse:
            keep = row < SUBLANES - sh
            a_s = pltpu.roll(a, SUBLANES - sh, axis=0)
            b_s = pltpu.roll(b, SUBLANES - sh, axis=0)
        else:
            keep = row >= sh
            a_s = pltpu.roll(a, sh, axis=0)
            b_s = pltpu.roll(b, sh, axis=0)
        b = b + a * jnp.where(keep, b_s, 0.0)
        a = a * jnp.where(keep, a_s, 1.0)
    return a, b


def _rglru_fwd_kernel(uc_ref, up_ref, un_ref, cw_ref, cb_ref, wg_ref, bg_ref, lam_ref,
                      hf_ref, ab_ref, bb_ref, ubuf, af_scr, bf_scr, carry, *, ts):
    i = pl.program_id(1)
    ns = pl.num_programs(1)
    ubuf[0:SUBLANES] = jnp.where(i > 0, up_ref[...], 0.0)
    ubuf[SUBLANES:SUBLANES + ts] = uc_ref[...]
    ubuf[SUBLANES + ts:] = jnp.where(i < ns - 1, un_ref[...], 0.0)
    u = cb_ref[...] + cw_ref[0:1, :] * ubuf[SUBLANES - 2:SUBLANES - 2 + ts]
    u = u + cw_ref[1:2, :] * ubuf[SUBLANES - 1:SUBLANES - 1 + ts]
    u = u + cw_ref[2:3, :] * ubuf[SUBLANES:SUBLANES + ts]
    u = u + cw_ref[3:4, :] * ubuf[SUBLANES + 1:SUBLANES + 1 + ts]
    gates = _dot(u.astype(BF16), wg_ref[...]) + bg_ref[...]
    sp = _softplus(-lam_ref[...])
    for d in range(2):
        r = jax.nn.sigmoid(gates[:, (2 * d) * C_WIDTH:(2 * d + 1) * C_WIDTH])
        ig = jax.nn.sigmoid(gates[:, (2 * d + 1) * C_WIDTH:(2 * d + 2) * C_WIDTH])
        log_a = -RG_C * r * sp[d:d + 1, :]
        a = jnp.exp(log_a)
        bterm = jnp.sqrt(-jnp.tanh(log_a) * (a * a + 1.0)) * (ig * u)
        if d == 0:
            af_scr[...] = a
            bf_scr[...] = bterm
        else:
            ab_ref[...] = a
            bb_ref[...] = bterm

    @pl.when(i == 0)
    def _():
        carry[...] = jnp.zeros_like(carry)

    row = lax.broadcasted_iota(jnp.int32, (SUBLANES, C_WIDTH), 0)

    def body(t, c):
        rows = pl.ds(pl.multiple_of(t * SUBLANES, SUBLANES), SUBLANES)
        a, b = _scan_tile(af_scr[rows, :], bf_scr[rows, :], row, reverse=False)
        h = b + a * carry[...]
        hf_ref[rows, :] = h
        carry[...] = h[SUBLANES - 1:SUBLANES, :]
        return c

    lax.fori_loop(0, ts // SUBLANES, body, 0)


def _rglru_bwd_kernel(a_ref, b_ref, hf_ref, g_ref, y_ref, carry, *, ts):
    @pl.when(pl.program_id(1) == 0)
    def _():
        carry[...] = jnp.zeros_like(carry)

    row = lax.broadcasted_iota(jnp.int32, (SUBLANES, C_WIDTH), 0)
    nt = ts // SUBLANES

    def body(t, c):
        rows = pl.ds(pl.multiple_of((nt - 1 - t) * SUBLANES, SUBLANES), SUBLANES)
        a, b = _scan_tile(a_ref[rows, :], b_ref[rows, :], row, reverse=True)
        h = b + a * carry[...]
        carry[...] = h[0:1, :]
        y_ref[rows, :] = jax.nn.gelu(g_ref[rows, :]) * (hf_ref[rows, :] + h)
        return c

    lax.fori_loop(0, nt, body, 0)


def _rglru(rest, conv_w, conv_b, w_gates, b_gates, lam, *, b, s, ts=512):
    t = b * s
    ns = s // ts
    hb = ts // SUBLANES
    last = s // SUBLANES - 1
    u_col, g_col = 2, 3
    tile = lambda bi, i: (bi * ns + i, 0)
    hf, ab, bb = pl.pallas_call(
        functools.partial(_rglru_fwd_kernel, ts=ts),
        out_shape=tuple(jax.ShapeDtypeStruct((t, C_WIDTH), F32) for _ in range(3)),
        grid=(b, ns),
        in_specs=[
            pl.BlockSpec((ts, C_WIDTH), lambda bi, i: (bi * ns + i, u_col)),
            pl.BlockSpec((SUBLANES, C_WIDTH),
                         lambda bi, i: (bi * ns * hb + jnp.maximum(i * hb - 1, 0), u_col)),
            pl.BlockSpec((SUBLANES, C_WIDTH),
                         lambda bi, i: (bi * ns * hb + jnp.minimum((i + 1) * hb, last), u_col)),
            pl.BlockSpec((4, C_WIDTH), lambda bi, i: (0, 0)),
            pl.BlockSpec((1, C_WIDTH), lambda bi, i: (0, 0)),
            pl.BlockSpec((C_WIDTH, 4 * C_WIDTH), lambda bi, i: (0, 0)),
            pl.BlockSpec((1, 4 * C_WIDTH), lambda bi, i: (0, 0)),
            pl.BlockSpec((2, C_WIDTH), lambda bi, i: (0, 0)),
        ],
        out_specs=tuple(pl.BlockSpec((ts, C_WIDTH), tile) for _ in range(3)),
        scratch_shapes=[pltpu.VMEM((ts + 2 * SUBLANES, C_WIDTH), F32),
                        pltpu.VMEM((ts, C_WIDTH), F32), pltpu.VMEM((ts, C_WIDTH), F32),
                        pltpu.VMEM((1, C_WIDTH), F32)],
        compiler_params=_params("parallel", "arbitrary"),
        name="rglru_fwd",
    )(rest, rest, rest, conv_w, conv_b, w_gates, b_gates, lam)
    rev = lambda bi, i: (bi * ns + ns - 1 - i, 0)
    return pl.pallas_call(
        functools.partial(_rglru_bwd_kernel, ts=ts),
        out_shape=jax.ShapeDtypeStruct((t, C_WIDTH), F32),
        grid=(b, ns),
        in_specs=[
            pl.BlockSpec((ts, C_WIDTH), rev),
            pl.BlockSpec((ts, C_WIDTH), rev),
            pl.BlockSpec((ts, C_WIDTH), rev),
            pl.BlockSpec((ts, C_WIDTH), lambda bi, i: (bi * ns + ns - 1 - i, g_col)),
        ],
        out_specs=pl.BlockSpec((ts, C_WIDTH), rev),
        scratch_shapes=[pltpu.VMEM((1, C_WIDTH), F32)],
        compiler_params=_params("parallel", "arbitrary"),
        name="rglru_bwd",
    )(ab, bb, hf, rest)


def _outproj_kernel(x_ref, o1_ref, o2_ref, o3_ref, l1_ref, l2_ref, l3_ref, yb_ref, yc_ref,
                    ga_ref, gb_ref, gc_ref, w_ref, out_ref):
    l1, l2, l3 = l1_ref[...], l2_ref[...], l3_ref[...]
    m = jnp.maximum(jnp.maximum(l1, l2), l3)
    e1, e2, e3 = jnp.exp(l1 - m), jnp.exp(l2 - m), jnp.exp(l3 - m)
    z = e1 + e2 + e3
    w1, w2, w3 = e1 / z, e2 / z, e3 / z
    parts = []
    for h in range(A_HEADS):
        parts.append(w1[:, h:h + 1] * o1_ref[h] + w2[:, h:h + 1] * o2_ref[h] + w3[:, h:h + 1] * o3_ref[h])
    ya = jnp.concatenate(parts, axis=1)
    b0, c0 = A_WIDTH, 2 * A_WIDTH
    y = _dot(_rms(ya, ga_ref[...]).astype(BF16), w_ref[0:b0, :])
    y = y + _dot(_rms(yb_ref[...], gb_ref[...]).astype(BF16), w_ref[b0:c0, :])
    y = y + _dot(_rms(yc_ref[...], gc_ref[...]).astype(BF16), w_ref[c0:, :])
    out_ref[...] = x_ref[...] + y


def _outproj(x, o_branches, lse_branches, yb, yc, ga, gb, gc, w, *, tm=256):
    t, d = x.shape
    row = lambda n: pl.BlockSpec((tm, n), lambda i: (i, 0))
    full = lambda r, n: pl.BlockSpec((r, n), lambda i: (0, 0))
    heads = pl.BlockSpec((A_HEADS, tm, HEAD), lambda i: (0, i, 0))
    return pl.pallas_call(
        _outproj_kernel,
        out_shape=jax.ShapeDtypeStruct((t, d), F32),
        grid=(t // tm,),
        in_specs=[row(d), heads, heads, heads, row(LANES), row(LANES), row(LANES),
                  row(A_WIDTH), row(C_WIDTH), full(1, A_WIDTH), full(1, A_WIDTH), full(1, C_WIDTH),
                  full(d, d)],
        out_specs=row(d),
        compiler_params=_params("parallel"),
        name="outproj",
    )(x, *o_branches, *lse_branches, yb, yc, ga, gb, gc, w)


def _mem_kv_kernel(m_ref, g_ref, w_ref, o_ref):
    o_ref[...] = _dot(_rms(m_ref[...], g_ref[...]).astype(BF16), w_ref[...]).astype(BF16)


def _mem_kv(mem, g, w_kv):
    t, d = mem.shape
    n = w_kv.shape[1]
    tm = 256
    return pl.pallas_call(
        _mem_kv_kernel,
        out_shape=jax.ShapeDtypeStruct((t, n), BF16),
        grid=(t // tm,),
        in_specs=[pl.BlockSpec((tm, d), lambda i: (i, 0)), pl.BlockSpec((1, d), lambda i: (0, 0)),
                  pl.BlockSpec((d, n), lambda i: (0, 0))],
        out_specs=pl.BlockSpec((tm, n), lambda i: (i, 0)),
        compiler_params=_params("parallel"),
        name="mem_kv",
    )(mem, g, w_kv)


def _xattn_kernel(x_ref, g_ref, wq_ref, kv_ref, wo_ref, o_ref):
    x = x_ref[...]
    q = _dot(_rms(x, g_ref[...]).astype(BF16), wq_ref[...]).astype(BF16)
    scale = HEAD ** -0.5
    outs = []
    for h in range(X_HEADS):
        k = kv_ref[:, h * HEAD:(h + 1) * HEAD]
        v = kv_ref[:, X_WIDTH + h * HEAD:X_WIDTH + (h + 1) * HEAD]
        s = _dot_nt(q[:, h * HEAD:(h + 1) * HEAD], k) * scale
        p = jnp.exp(s - jnp.max(s, axis=1, keepdims=True))
        l = jnp.sum(p, axis=1, keepdims=True)
        outs.append(_dot(p.astype(BF16), v) / l)
    o = jnp.concatenate(outs, axis=1).astype(BF16)
    o_ref[...] = x + _dot(o, wo_ref[...])


def _xattn(x, g, wq, kv, wo, *, s, n_mem, tm=512):
    t, d = x.shape
    per_seq = s // tm
    return pl.pallas_call(
        _xattn_kernel,
        out_shape=jax.ShapeDtypeStruct((t, d), F32),
        grid=(t // tm,),
        in_specs=[
            pl.BlockSpec((tm, d), lambda i: (i, 0)),
            pl.BlockSpec((1, d), lambda i: (0, 0)),
            pl.BlockSpec((d, X_WIDTH), lambda i: (0, 0)),
            pl.BlockSpec((n_mem, 2 * X_WIDTH), lambda i: (i // per_seq, 0)),
            pl.BlockSpec((X_WIDTH, d), lambda i: (0, 0)),
        ],
        out_specs=pl.BlockSpec((tm, d), lambda i: (i, 0)),
        compiler_params=_params("parallel"),
        name="xattn",
    )(x, g, wq, kv, wo)


def _rope_table(s):
    inv = ROPE_THETA ** (-jnp.arange(0, QK_ROPE, 2, dtype=F32) / QK_ROPE)
    ang = jnp.arange(s, dtype=F32)[:, None] * inv[None, :]
    cos, sin = jnp.cos(ang), jnp.sin(ang)
    return jnp.concatenate([cos, cos, -sin, sin], axis=1)


def _swap_halves(w):
    half = w.shape[-1] // 2
    return jnp.concatenate([w[..., half:], w[..., :half]], axis=-1)


def _prep_layer(p, l):
    row = lambda v: v.reshape(1, -1)
    w_in = p['w_in'][l]
    qkv_end, cq_end, ckv_end, kr_end, u_end = 2304, 2816, 3328, 3392, 3904
    w_rope = w_in[:, ckv_end:kr_end]
    w_rest = jnp.concatenate([w_in[:, qkv_end:ckv_end], w_in[:, kr_end:], w_rope, _swap_halves(w_rope)], axis=1)
    wq = p['w_q_up'][l].reshape(-1, B_HEADS, HEAD + QK_ROPE)
    wq = jnp.concatenate([wq, _swap_halves(wq[..., HEAD:])], axis=-1).reshape(-1, B_HEADS * 2 * HEAD)
    eye = jnp.eye(C_BLOCKS, dtype=F32)
    dense = lambda w: jnp.einsum('ncd,nm->ncmd', w, eye).reshape(C_WIDTH, C_WIDTH)
    w_r, w_i = p['w_rg_r'][l], p['w_rg_i'][l]
    w_gates = jnp.concatenate([dense(w_r[0]), dense(w_i[0]), dense(w_r[1]), dense(w_i[1])], axis=1)
    b_r, b_i = p['b_rg_r'][l], p['b_rg_i'][l]
    b_gates = jnp.concatenate([b_r[0], b_i[0], b_r[1], b_i[1]]).reshape(1, -1)
    bf = lambda w: w.astype(BF16)
    return dict(
        g_ffn1=row(p['g_ffn1'][l]), w1_gate=bf(p['w1_gate'][l]), w1_up=bf(p['w1_up'][l]), w1_down=bf(p['w1_down'][l]),
        g_mix=row(p['g_mix'][l]), w_qkv=bf(w_in[:, :qkv_end]), w_rest=bf(w_rest),
        g_q_lat=row(p['g_q_lat'][l]), g_kv_lat=row(p['g_kv_lat'][l]), w_q=bf(wq), w_kv=bf(p['w_kv_up'][l]),
        conv_w=p['conv_w'][l], conv_b=row(p['conv_b'][l]), w_gates=bf(w_gates), b_gates=b_gates,
        lam=p['rg_lambda'][l],
        g_out_a=row(p['g_out_a'][l]), g_out_b=row(p['g_out_b'][l]), g_out_c=row(p['g_out_c'][l]),
        w_out=bf(p['w_out'][l]),
        g_xattn=row(p['g_xattn'][l]), g_mem=row(p['g_mem'][l]), w_xq=bf(p['w_xq'][l]),
        w_xkv=bf(jnp.concatenate([p['w_xk'][l], p['w_xv'][l]], axis=1)), w_xo=bf(p['w_xo'][l]),
        g_ffn2=row(p['g_ffn2'][l]), w2_gate=bf(p['w2_gate'][l]), w2_up=bf(p['w2_up'][l]), w2_down=bf(p['w2_down'][l]),
    )


def _trunk(x, mem, layers, g_final):
    b, s, d = x.shape
    n_mem = mem.shape[1]
    x = x.reshape(b * s, d)
    mem = mem.reshape(b * n_mem, d)
    tab = _rope_table(s)
    for l, w in enumerate(layers):
        x = _ffn(x, w['g_ffn1'], w['w1_gate'], w['w1_up'], w['w1_down'], g_final, final_norm=False)
        *qkv_by_dil, rest = _inproj(x, w['g_mix'], w['w_qkv'], w['w_rest'], b=b, s=s)
        branches = [_dilated_branch(qkv, dil) for qkv, (_, dil) in zip(qkv_by_dil, DILATED_CONFIGS)]
        q, kt, v = _mla_proj(rest, tab, w['g_q_lat'], w['g_kv_lat'], w['w_q'], w['w_kv'], s=s)
        yb = _mla_attn(q, kt, v, b=b, s=s)
        yc = _rglru(rest, w['conv_w'], w['conv_b'], w['w_gates'], w['b_gates'], w['lam'], b=b, s=s)
        x = _outproj(x, [o for o, _ in branches], [lse for _, lse in branches], yb, yc,
                     w['g_out_a'], w['g_out_b'], w['g_out_c'], w['w_out'])
        kv = _mem_kv(mem, w['g_mem'], w['w_xkv'])
        x = _xattn(x, w['g_xattn'], w['w_xq'], kv, w['w_xo'], s=s, n_mem=n_mem)
        x = _ffn(x, w['g_ffn2'], w['w2_gate'], w['w2_up'], w['w2_down'], g_final,
                 final_norm=(l == len(layers) - 1))
    return x.reshape(b, s, d)


def kernel(x_prompt, x_sample, mem_prompt, mem_sample, g_ffn1, w1_gate, w1_up, w1_down, g_mix, w_in, g_q_lat, w_q_up, g_kv_lat, w_kv_up, conv_w, conv_b, w_rg_r, b_rg_r, w_rg_i, b_rg_i, rg_lambda, g_out_a, g_out_b, g_out_c, w_out, g_xattn, g_mem, w_xq, w_xk, w_xv, w_xo, g_ffn2, w2_gate, w2_up, w2_down, g_final):
    p = dict(g_ffn1=g_ffn1, w1_gate=w1_gate, w1_up=w1_up, w1_down=w1_down,
             g_mix=g_mix, w_in=w_in, g_q_lat=g_q_lat, w_q_up=w_q_up, g_kv_lat=g_kv_lat, w_kv_up=w_kv_up,
             conv_w=conv_w, conv_b=conv_b, w_rg_r=w_rg_r, b_rg_r=b_rg_r, w_rg_i=w_rg_i, b_rg_i=b_rg_i,
             rg_lambda=rg_lambda, g_out_a=g_out_a, g_out_b=g_out_b, g_out_c=g_out_c, w_out=w_out,
             g_xattn=g_xattn, g_mem=g_mem, w_xq=w_xq, w_xk=w_xk, w_xv=w_xv, w_xo=w_xo,
             g_ffn2=g_ffn2, w2_gate=w2_gate, w2_up=w2_up, w2_down=w2_down)
    layers = [_prep_layer(p, l) for l in range(g_ffn1.shape[0])]
    gf = g_final.reshape(1, -1)
    return (_trunk(x_prompt, mem_prompt, layers, gf), _trunk(x_sample, mem_sample, layers, gf))
```

```python
import functools
import math

import jax
import jax.numpy as jnp
from jax import lax
from jax.experimental import pallas as pl
from jax.experimental.pallas import tpu as pltpu

BF16 = jnp.bfloat16
F32 = jnp.float32

D_MODEL = 2048
A_HEADS = 6
HEAD = 128
A_WIDTH = A_HEADS * HEAD
DILATED_CONFIGS = ((128, 1), (512, 4), (2048, 16))
B_HEADS = 6
QK_ROPE = 64
ROPE_THETA = 10000.0
C_WIDTH = 512
C_BLOCKS = 8
C_BLOCK_W = 64
RG_C = 8.0
X_HEADS = 4
X_WIDTH = 512
D_FF = 5632
EPS = 1e-6
NEG_INF = -1e30

V7X_VMEM_LIMIT_BYTES = 56 * 1024 * 1024
LANES = 128
SUBLANES = 8

REST_WIDTH = 4 * 512 + LANES
BAND_HALF = 64


def _params(*sem):
    return pltpu.CompilerParams(dimension_semantics=sem, vmem_limit_bytes=V7X_VMEM_LIMIT_BYTES)


def _rms(x, g):
    return x * lax.rsqrt(jnp.mean(x * x, axis=-1, keepdims=True) + EPS) * g


def _dot(a, b):
    return jnp.dot(a, b, preferred_element_type=F32)


def _dot_nt(a, b):
    return lax.dot_general(a, b, (((1,), (1,)), ((), ())), preferred_element_type=F32)


def _ffn_kernel(x_ref, g_ref, wg_ref, wu_ref, wd_ref, gf_ref, o_ref, h_scr, *, final_norm):
    j = pl.program_id(1)

    @pl.when(j == 0)
    def _():
        h_scr[...] = _rms(x_ref[...], g_ref[...]).astype(BF16)
        o_ref[...] = jnp.zeros_like(o_ref)

    h = h_scr[...]
    a = _dot(h, wg_ref[...])
    u = _dot(h, wu_ref[...])
    o_ref[...] += _dot((jax.nn.silu(a) * u).astype(BF16), wd_ref[...])

    @pl.when(j == pl.num_programs(1) - 1)
    def _():
        rows_per_pass = 256

        def finish(r, carry):
            rows = pl.ds(pl.multiple_of(r * rows_per_pass, rows_per_pass), rows_per_pass)
            y = x_ref[rows, :] + 0.5 * o_ref[rows, :]
            if final_norm:
                y = _rms(y, gf_ref[...])
            o_ref[rows, :] = y
            return carry

        lax.fori_loop(0, o_ref.shape[0] // rows_per_pass, finish, 0)


def _ffn(x, g, wg, wu, wd, gf, *, final_norm, tm=1024, tf=512):
    t, d = x.shape
    f = wg.shape[1]
    return pl.pallas_call(
        functools.partial(_ffn_kernel, final_norm=final_norm),
        out_shape=jax.ShapeDtypeStruct((t, d), F32),
        grid=(t // tm, f // tf),
        in_specs=[
            pl.BlockSpec((tm, d), lambda i, j: (i, 0)),
            pl.BlockSpec((1, d), lambda i, j: (0, 0)),
            pl.BlockSpec((d, tf), lambda i, j: (0, j)),
            pl.BlockSpec((d, tf), lambda i, j: (0, j)),
            pl.BlockSpec((tf, d), lambda i, j: (j, 0)),
            pl.BlockSpec((1, d), lambda i, j: (0, 0)),
        ],
        out_specs=pl.BlockSpec((tm, d), lambda i, j: (i, 0)),
        scratch_shapes=[pltpu.VMEM((tm, d), BF16)],
        compiler_params=_params("parallel", "arbitrary"),
        name="ffn",
    )(x, g, wg, wu, wd, gf)


def _inproj_kernel(x_ref, g_ref, w1_ref, w2_ref, o1_ref, o4_ref, o16_ref, o2_ref, y_scr, c4_scr, *, tm):
    h = _rms(x_ref[...], g_ref[...]).astype(BF16)
    y = _dot(h, w1_ref[...])
    o1_ref[0] = y.astype(BF16)
    n4 = tm // 4
    for cb in range(y.shape[1] // LANES):
        cols = slice(cb * LANES, (cb + 1) * LANES)
        y_scr[cb] = y[:, cols]
        for r4 in range(4):
            c4 = y_scr[cb, pl.ds(r4, n4, stride=4), :]
            o4_ref[r4, :, cols] = c4.astype(BF16)
            c4_scr[cb, r4] = c4
            for j in range(4):
                o16_ref[r4 + 4 * j, :, cols] = c4_scr[cb, r4, pl.ds(j, n4 // 4, stride=4), :].astype(BF16)
    o2_ref[...] = _dot(h, w2_ref[...])


def _inproj(x, g, w_qkv, w_rest, *, b, s, tm=256):
    t, d = x.shape
    n1, n2 = w_qkv.shape[1], w_rest.shape[1]
    per_seq = s // tm
    dils = [dil for _, dil in DILATED_CONFIGS]
    cls_shape = lambda dil: jax.ShapeDtypeStruct((b, dil, s // dil, n1), BF16)
    cls_spec = lambda dil: pl.BlockSpec((None, dil, tm // dil, n1), lambda i: (i // per_seq, 0, i % per_seq, 0))
    resident = dict(pipeline_mode=pl.Buffered(1))
    return pl.pallas_call(
        functools.partial(_inproj_kernel, tm=tm),
        out_shape=(*[cls_shape(dil) for dil in dils], jax.ShapeDtypeStruct((t, n2), F32)),
        grid=(t // tm,),
        in_specs=[
            pl.BlockSpec((tm, d), lambda i: (i, 0)),
            pl.BlockSpec((1, d), lambda i: (0, 0)),
            pl.BlockSpec((d, n1), lambda i: (0, 0), **resident),
            pl.BlockSpec((d, n2), lambda i: (0, 0), **resident),
        ],
        out_specs=(*[cls_spec(dil) for dil in dils], pl.BlockSpec((tm, n2), lambda i: (i, 0))),
        scratch_shapes=[pltpu.VMEM((n1 // LANES, tm, LANES), F32),
                        pltpu.VMEM((n1 // LANES, 4, tm // 4, LANES), F32)],
        compiler_params=_params("parallel"),
        name="inproj",
    )(x, g, w_qkv, w_rest)


def _dilated_kernel(q_ref, kp_ref, kc_ref, kn_ref, vp_ref, vc_ref, vn_ref, bias_ref, o_ref, lse_ref,
                    k_scr, v_scr, *, tq, n_cls, dil, rg):
    i = pl.program_id(1)
    scale = HEAD ** -0.5
    sub = 2 * BAND_HALF
    win = sub + 2 * BAND_HALF
    lane = lax.broadcasted_iota(jnp.int32, (sub, LANES), 1)
    for rr in range(rg):
        r = pl.program_id(2) * rg + rr
        k_scr[0:BAND_HALF] = kp_ref[rr]
        k_scr[BAND_HALF:BAND_HALF + tq] = kc_ref[rr]
        k_scr[BAND_HALF + tq:] = kn_ref[rr]
        v_scr[0:BAND_HALF] = vp_ref[rr]
        v_scr[BAND_HALF:BAND_HALF + tq] = vc_ref[rr]
        v_scr[BAND_HALF + tq:] = vn_ref[rr]
        for a in range(tq // sub):
            rows = pl.ds(a * sub, sub) if dil == 1 else pl.ds(a * sub * dil + r, sub, stride=dil)
            edge = a == 0 or a == tq // sub - 1
            kidx = i * tq + (a * sub - BAND_HALF) + lax.broadcasted_iota(jnp.int32, (1, win), 1)
            valid = (kidx >= 0) & (kidx < n_cls)
            lse_all = jnp.zeros((sub, LANES), F32)
            for h in range(A_HEADS):
                cols = slice(h * HEAD, (h + 1) * HEAD)
                q = q_ref[rr, a * sub:(a + 1) * sub, cols]
                k = k_scr[a * sub:a * sub + win, cols]
                v = v_scr[a * sub:a * sub + win, cols]
                s = _dot_nt(q, k) * scale + bias_ref[h]
                if edge:
                    s = jnp.where(valid, s, NEG_INF)
                m = jnp.max(s, axis=1, keepdims=True)
                p = jnp.exp(s - m)
                l = jnp.sum(p, axis=1, keepdims=True)
                o_ref[h, rows, :] = _dot(p.astype(BF16), v) / l
                lse_all = jnp.where(lane == h, m + jnp.log(l), lse_all)
            lse_ref[rows, :] = lse_all


def _band_bias(dil):
    slopes = 2.0 ** (-8.0 * jnp.arange(1, A_HEADS + 1, dtype=F32) / A_HEADS)
    sub, win = 2 * BAND_HALF, 4 * BAND_HALF
    rel = jnp.abs(BAND_HALF + jnp.arange(sub)[:, None] - jnp.arange(win)[None, :])
    bias = -slopes[:, None, None] * (dil * rel).astype(F32)[None]
    return jnp.where((rel <= BAND_HALF)[None], bias, NEG_INF)


def _dilated_branch(qkv, dil):
    b, _, n_cls, _ = qkv.shape
    tq = min(512, n_cls, 2048 // dil)
    nq = n_cls // tq
    nh = tq // BAND_HALF
    last_halo = n_cls // BAND_HALF - 1
    rg = min(dil, 512 // tq)

    def cur(which):
        return pl.BlockSpec((None, rg, tq, A_WIDTH), lambda bi, i, r: (bi, r, i, which))

    def prev(which):
        return pl.BlockSpec((None, rg, BAND_HALF, A_WIDTH),
                            lambda bi, i, r: (bi, r, jnp.maximum(i * nh - 1, 0), which))

    def nxt(which):
        return pl.BlockSpec((None, rg, BAND_HALF, A_WIDTH),
                            lambda bi, i, r: (bi, r, jnp.minimum((i + 1) * nh, last_halo), which))

    t = b * n_cls * dil
    return pl.pallas_call(
        functools.partial(_dilated_kernel, tq=tq, n_cls=n_cls, dil=dil, rg=rg),
        out_shape=(jax.ShapeDtypeStruct((A_HEADS, t, HEAD), F32), jax.ShapeDtypeStruct((t, LANES), F32)),
        grid=(b, nq, dil // rg),
        in_specs=[cur(0), prev(1), cur(1), nxt(1), prev(2), cur(2), nxt(2),
                  pl.BlockSpec((A_HEADS, 2 * BAND_HALF, 4 * BAND_HALF), lambda bi, i, r: (0, 0, 0))],
        out_specs=(pl.BlockSpec((A_HEADS, tq * dil, HEAD), lambda bi, i, r: (0, bi * nq + i, 0)),
                   pl.BlockSpec((tq * dil, LANES), lambda bi, i, r: (bi * nq + i, 0))),
        scratch_shapes=[pltpu.VMEM((tq + 2 * BAND_HALF, A_WIDTH), BF16),
                        pltpu.VMEM((tq + 2 * BAND_HALF, A_WIDTH), BF16)],
        compiler_params=_params("parallel", "parallel", "arbitrary"),
        name=f"dilated{dil}",
    )(qkv, qkv, qkv, qkv, qkv, qkv, qkv, _band_bias(dil))


def _rotate(y, tab):
    z = y * tab
    r = z + pltpu.roll(z, QK_ROPE, axis=1)
    lane = lax.broadcasted_iota(jnp.int32, r.shape, 1)
    return jnp.where(lane < QK_ROPE, r, 0.0)


def _mla_proj_kernel(cq_ref, ckv_ref, kr_ref, tab_ref, gq_ref, gkv_ref, wq_ref, wkv_ref,
                     q_ref, kt_ref, v_ref):
    cq = _rms(cq_ref[...], gq_ref[...]).astype(BF16)
    ckv = _rms(ckv_ref[...], gkv_ref[...]).astype(BF16)
    tab = tab_ref[...]
    kr_t = _rotate(kr_ref[...], tab).T.astype(BF16)
    ones = jnp.ones((v_ref.shape[0], HEAD), BF16)
    for h in range(B_HEADS):
        lo, mid, hi = 2 * h * HEAD, (2 * h + 1) * HEAD, (2 * h + 2) * HEAD
        q = _dot(cq, wq_ref[:, lo:hi])
        q_ref[:, lo:mid] = q[:, :HEAD].astype(BF16)
        q_ref[:, mid:hi] = _rotate(q[:, HEAD:], tab).astype(BF16)
        kv = _dot(ckv, wkv_ref[:, lo:hi])
        kt_ref[h, :HEAD, :] = kv[:, :HEAD].T.astype(BF16)
        kt_ref[h, HEAD:, :] = kr_t
        v_ref[:, lo:mid] = kv[:, HEAD:].astype(BF16)
        v_ref[:, mid:hi] = ones


def _mla_proj(rest, tab, gq, gkv, wq, wkv, *, s, tm=512):
    t = rest.shape[0]
    lat = 512
    width = B_HEADS * 2 * HEAD
    pos_blocks = s // tm
    return pl.pallas_call(
        _mla_proj_kernel,
        out_shape=(jax.ShapeDtypeStruct((t, width), BF16),
                   jax.ShapeDtypeStruct((B_HEADS, t // tm, 2 * HEAD, tm), BF16),
                   jax.ShapeDtypeStruct((t, width), BF16)),
        grid=(t // tm,),
        in_specs=[
            pl.BlockSpec((tm, lat), lambda i: (i, 0)),
            pl.BlockSpec((tm, lat), lambda i: (i, 1)),
            pl.BlockSpec((tm, LANES), lambda i: (i, 4 * lat // LANES)),
            pl.BlockSpec((tm, LANES), lambda i: (i % pos_blocks, 0)),
            pl.BlockSpec((1, lat), lambda i: (0, 0)),
            pl.BlockSpec((1, lat), lambda i: (0, 0)),
            pl.BlockSpec((lat, width), lambda i: (0, 0)),
            pl.BlockSpec((lat, width), lambda i: (0, 0)),
        ],
        out_specs=(pl.BlockSpec((tm, width), lambda i: (i, 0)),
                   pl.BlockSpec((B_HEADS, None, 2 * HEAD, tm), lambda i: (0, i, 0, 0)),
                   pl.BlockSpec((tm, width), lambda i: (i, 0))),
        compiler_params=_params("parallel"),
        name="mla_proj",
    )(rest, rest, rest, tab, gq, gkv, wq, wkv)


def _mla_attn_kernel(q_ref, kt_ref, v_ref, o_ref, m_scr, a_scr, acc_scr, s_scr, p_scr, mx_scr, *, tq, tk, nk):
    c = (HEAD + QK_ROPE) ** -0.5 * math.log2(math.e)
    q = q_ref[...]
    nl = tk // LANES

    def chunk(kk):
        return pl.ds(pl.multiple_of(kk * tk, tk), tk)

    def scores(kk, slot):
        s = _dot(q, kt_ref[kk])
        s_scr[slot] = s
        mx = s[:, :LANES]
        for j in range(1, nl):
            mx = jnp.maximum(mx, s[:, j * LANES:(j + 1) * LANES])
        mx_scr[slot] = mx

    def softmax(slot, first):
        s = s_scr[slot]
        m_new = jnp.broadcast_to(jnp.max(mx_scr[slot], axis=1, keepdims=True), (tq, LANES))
        if not first:
            m_old = m_scr[...]
            m_new = jnp.maximum(m_old, m_new)
            a_scr[slot] = jnp.exp2((m_old - m_new) * c)
        for j in range(nl):
            cols = slice(j * LANES, (j + 1) * LANES)
            p_scr[slot, :, cols] = jnp.exp2((s[:, cols] - m_new) * c).astype(BF16)
        m_scr[...] = m_new

    def values(kk, slot, first):
        pv = _dot(p_scr[slot], v_ref[chunk(kk), :])
        if first:
            acc_scr[...] = pv
        else:
            alpha = a_scr[slot]
            acc_scr[...] = jnp.concatenate([alpha, alpha], axis=1) * acc_scr[...] + pv

    scores(0, 0)
    softmax(0, True)
    scores(1, 1)
    softmax(1, False)
    values(0, 0, True)
    scores(2, 0)

    pairs = 2 if (nk - 4) % 4 == 0 else 1

    def body(g, carry):
        for u in range(pairs):
            kk = 2 * (pairs * g + u + 1)
            softmax(0, False)
            values(kk - 1, 1, False)
            scores(kk + 1, 1)
            softmax(1, False)
            values(kk, 0, False)
            scores(kk + 2, 0)
        return carry

    lax.fori_loop(0, (nk - 4) // (2 * pairs), body, 0)
    softmax(0, False)
    values(nk - 3, 1, False)
    scores(nk - 1, 1)
    softmax(1, False)
    values(nk - 2, 0, False)
    values(nk - 1, 1, False)
    o_ref[...] = acc_scr[:, :HEAD] / acc_scr[:, HEAD:]


def _mla_attn(q, kt, v, *, b, s, tq=1024):
    t = b * s
    nq = s // tq
    tk = kt.shape[-1]
    nk = s // tk
    assert nk % 2 == 0 and nk >= 4
    return pl.pallas_call(
        functools.partial(_mla_attn_kernel, tq=tq, tk=tk, nk=nk),
        out_shape=jax.ShapeDtypeStruct((t, B_HEADS * HEAD), F32),
        grid=(b, B_HEADS, nq),
        in_specs=[
            pl.BlockSpec((tq, 2 * HEAD), lambda bi, h, i: (bi * nq + i, h)),
            pl.BlockSpec((None, nk, 2 * HEAD, tk), lambda bi, h, i: (h, bi, 0, 0)),
            pl.BlockSpec((s, 2 * HEAD), lambda bi, h, i: (bi, h)),
        ],
        out_specs=pl.BlockSpec((tq, HEAD), lambda bi, h, i: (bi * nq + i, h)),
        scratch_shapes=[pltpu.VMEM((tq, LANES), F32), pltpu.VMEM((2, tq, LANES), F32),
                        pltpu.VMEM((tq, 2 * HEAD), F32), pltpu.VMEM((2, tq, tk), F32),
                        pltpu.VMEM((2, tq, tk), BF16), pltpu.VMEM((2, tq, LANES), F32)],
        compiler_params=_params("parallel", "parallel", "arbitrary"),
        name="mla_attn",
    )(q, kt, v)


def _softplus(x):
    return jnp.maximum(x, 0.0) + jnp.log1p(jnp.exp(-jnp.abs(x)))


def _scan_tile(a, b, row, reverse):
    for sh in (1, 2, 4):
        if reverse:
            keep = row < SUBLANES - sh
            a_s = pltpu.roll(a, SUBLANES - sh, axis=0)
            b_s = pltpu.roll(b, SUBLANES - sh, axis=0)
        else:
            keep = row >= sh
            a_s = pltpu.roll(a, sh, axis=0)
            b_s = pltpu.roll(b, sh, axis=0)
        b = b + a * jnp.where(keep, b_s, 0.0)
        a = a * jnp.where(keep, a_s, 1.0)
    return a, b


def _rglru_fwd_kernel(uc_ref, up_ref, un_ref, cw_ref, cb_ref, wg_ref, bg_ref, lam_ref,
                      hf_ref, ab_ref, bb_ref, ubuf, af_scr, bf_scr, carry, *, ts):
    i = pl.program_id(1)
    ns = pl.num_programs(1)
    ubuf[0:SUBLANES] = jnp.where(i > 0, up_ref[...], 0.0)
    ubuf[SUBLANES:SUBLANES + ts] = uc_ref[...]
    ubuf[SUBLANES + ts:] = jnp.where(i < ns - 1, un_ref[...], 0.0)
    u = cb_ref[...] + cw_ref[0:1, :] * ubuf[SUBLANES - 2:SUBLANES - 2 + ts]
    u = u + cw_ref[1:2, :] * ubuf[SUBLANES - 1:SUBLANES - 1 + ts]
    u = u + cw_ref[2:3, :] * ubuf[SUBLANES:SUBLANES + ts]
    u = u + cw_ref[3:4, :] * ubuf[SUBLANES + 1:SUBLANES + 1 + ts]
    gates = _dot(u.astype(BF16), wg_ref[...]) + bg_ref[...]
    sp = _softplus(-lam_ref[...])
    for d in range(2):
        r = jax.nn.sigmoid(gates[:, (2 * d) * C_WIDTH:(2 * d + 1) * C_WIDTH])
        ig = jax.nn.sigmoid(gates[:, (2 * d + 1) * C_WIDTH:(2 * d + 2) * C_WIDTH])
        log_a = -RG_C * r * sp[d:d + 1, :]
        a = jnp.exp(log_a)
        bterm = jnp.sqrt(-jnp.tanh(log_a) * (a * a + 1.0)) * (ig * u)
        if d == 0:
            af_scr[...] = a
            bf_scr[...] = bterm
        else:
            ab_ref[...] = a
            bb_ref[...] = bterm

    @pl.when(i == 0)
    def _():
        carry[...] = jnp.zeros_like(carry)

    row = lax.broadcasted_iota(jnp.int32, (SUBLANES, C_WIDTH), 0)

    def body(t, c):
        rows = pl.ds(pl.multiple_of(t * SUBLANES, SUBLANES), SUBLANES)
        a, b = _scan_tile(af_scr[rows, :], bf_scr[rows, :], row, reverse=False)
        h = b + a * carry[...]
        hf_ref[rows, :] = h
        carry[...] = h[SUBLANES - 1:SUBLANES, :]
        return c

    lax.fori_loop(0, ts // SUBLANES, body, 0)


def _rglru_bwd_kernel(a_ref, b_ref, hf_ref, g_ref, y_ref, carry, *, ts):
    @pl.when(pl.program_id(1) == 0)
    def _():
        carry[...] = jnp.zeros_like(carry)

    row = lax.broadcasted_iota(jnp.int32, (SUBLANES, C_WIDTH), 0)
    nt = ts // SUBLANES

    def body(t, c):
        rows = pl.ds(pl.multiple_of((nt - 1 - t) * SUBLANES, SUBLANES), SUBLANES)
        a, b = _scan_tile(a_ref[rows, :], b_ref[rows, :], row, reverse=True)
        h = b + a * carry[...]
        carry[...] = h[0:1, :]
        y_ref[rows, :] = jax.nn.gelu(g_ref[rows, :]) * (hf_ref[rows, :] + h)
        return c

    lax.fori_loop(0, nt, body, 0)


def _rglru(rest, conv_w, conv_b, w_gates, b_gates, lam, *, b, s, ts=512):
    t = b * s
    ns = s // ts
    hb = ts // SUBLANES
    last = s // SUBLANES - 1
    u_col, g_col = 2, 3
    tile = lambda bi, i: (bi * ns + i, 0)
    hf, ab, bb = pl.pallas_call(
        functools.partial(_rglru_fwd_kernel, ts=ts),
        out_shape=tuple(jax.ShapeDtypeStruct((t, C_WIDTH), F32) for _ in range(3)),
        grid=(b, ns),
        in_specs=[
            pl.BlockSpec((ts, C_WIDTH), lambda bi, i: (bi * ns + i, u_col)),
            pl.BlockSpec((SUBLANES, C_WIDTH),
                         lambda bi, i: (bi * ns * hb + jnp.maximum(i * hb - 1, 0), u_col)),
            pl.BlockSpec((SUBLANES, C_WIDTH),
                         lambda bi, i: (bi * ns * hb + jnp.minimum((i + 1) * hb, last), u_col)),
            pl.BlockSpec((4, C_WIDTH), lambda bi, i: (0, 0)),
            pl.BlockSpec((1, C_WIDTH), lambda bi, i: (0, 0)),
            pl.BlockSpec((C_WIDTH, 4 * C_WIDTH), lambda bi, i: (0, 0)),
            pl.BlockSpec((1, 4 * C_WIDTH), lambda bi, i: (0, 0)),
            pl.BlockSpec((2, C_WIDTH), lambda bi, i: (0, 0)),
        ],
        out_specs=tuple(pl.BlockSpec((ts, C_WIDTH), tile) for _ in range(3)),
        scratch_shapes=[pltpu.VMEM((ts + 2 * SUBLANES, C_WIDTH), F32),
                        pltpu.VMEM((ts, C_WIDTH), F32), pltpu.VMEM((ts, C_WIDTH), F32),
                        pltpu.VMEM((1, C_WIDTH), F32)],
        compiler_params=_params("parallel", "arbitrary"),
        name="rglru_fwd",
    )(rest, rest, rest, conv_w, conv_b, w_gates, b_gates, lam)
    rev = lambda bi, i: (bi * ns + ns - 1 - i, 0)
    return pl.pallas_call(
        functools.partial(_rglru_bwd_kernel, ts=ts),
        out_shape=jax.ShapeDtypeStruct((t, C_WIDTH), F32),
        grid=(b, ns),
        in_specs=[
            pl.BlockSpec((ts, C_WIDTH), rev),
            pl.BlockSpec((ts, C_WIDTH), rev),
            pl.BlockSpec((ts, C_WIDTH), rev),
            pl.BlockSpec((ts, C_WIDTH), lambda bi, i: (bi * ns + ns - 1 - i, g_col)),
        ],
        out_specs=pl.BlockSpec((ts, C_WIDTH), rev),
        scratch_shapes=[pltpu.VMEM((1, C_WIDTH), F32)],
        compiler_params=_params("parallel", "arbitrary"),
        name="rglru_bwd",
    )(ab, bb, hf, rest)


def _outproj_kernel(x_ref, o1_ref, o2_ref, o3_ref, l1_ref, l2_ref, l3_ref, yb_ref, yc_ref,
                    ga_ref, gb_ref, gc_ref, w_ref, out_ref):
    l1, l2, l3 = l1_ref[...], l2_ref[...], l3_ref[...]
    m = jnp.maximum(jnp.maximum(l1, l2), l3)
    e1, e2, e3 = jnp.exp(l1 - m), jnp.exp(l2 - m), jnp.exp(l3 - m)
    z = e1 + e2 + e3
    w1, w2, w3 = e1 / z, e2 / z, e3 / z
    parts = []
    for h in range(A_HEADS):
        parts.append(w1[:, h:h + 1] * o1_ref[h] + w2[:, h:h + 1] * o2_ref[h] + w3[:, h:h + 1] * o3_ref[h])
    ya = jnp.concatenate(parts, axis=1)
    b0, c0 = A_WIDTH, 2 * A_WIDTH
    y = _dot(_rms(yb_ref[...], gb_ref[...]).astype(BF16), w_ref[b0:c0, :])
    y = y + _dot(_rms(yc_ref[...], gc_ref[...]).astype(BF16), w_ref[c0:, :])
    y = y + _dot(_rms(ya, ga_ref[...]).astype(BF16), w_ref[0:b0, :])
    out_ref[...] = x_ref[...] + y


def _outproj(x, o_branches, lse_branches, yb, yc, ga, gb, gc, w, *, tm=512):
    t, d = x.shape
    row = lambda n: pl.BlockSpec((tm, n), lambda i: (i, 0))
    full = lambda r, n: pl.BlockSpec((r, n), lambda i: (0, 0))
    heads = pl.BlockSpec((A_HEADS, tm, HEAD), lambda i: (0, i, 0))
    return pl.pallas_call(
        _outproj_kernel,
        out_shape=jax.ShapeDtypeStruct((t, d), F32),
        grid=(t // tm,),
        in_specs=[row(d), heads, heads, heads, row(LANES), row(LANES), row(LANES),
                  row(A_WIDTH), row(C_WIDTH), full(1, A_WIDTH), full(1, A_WIDTH), full(1, C_WIDTH),
                  full(d, d)],
        out_specs=row(d),
        compiler_params=_params("parallel"),
        name="outproj",
    )(x, *o_branches, *lse_branches, yb, yc, ga, gb, gc, w)


def _mem_kv_kernel(m_ref, g_ref, w_ref, o_ref):
    o_ref[...] = _dot(_rms(m_ref[...], g_ref[...]).astype(BF16), w_ref[...]).astype(BF16)


def _mem_kv(mem, g, w_kv):
    t, d = mem.shape
    n = w_kv.shape[1]
    tm = 256
    return pl.pallas_call(
        _mem_kv_kernel,
        out_shape=jax.ShapeDtypeStruct((t, n), BF16),
        grid=(t // tm,),
        in_specs=[pl.BlockSpec((tm, d), lambda i: (i, 0)), pl.BlockSpec((1, d), lambda i: (0, 0)),
                  pl.BlockSpec((d, n), lambda i: (0, 0))],
        out_specs=pl.BlockSpec((tm, n), lambda i: (i, 0)),
        compiler_params=_params("parallel"),
        name="mem_kv",
    )(mem, g, w_kv)


def _xattn_kernel(x_ref, g_ref, wq_ref, kv_ref, wo_ref, o_ref):
    x = x_ref[...]
    q = _dot(_rms(x, g_ref[...]).astype(BF16), wq_ref[...]).astype(BF16)
    scale = HEAD ** -0.5
    outs = []
    for h in range(X_HEADS):
        k = kv_ref[:, h * HEAD:(h + 1) * HEAD]
        v = kv_ref[:, X_WIDTH + h * HEAD:X_WIDTH + (h + 1) * HEAD]
        s = _dot_nt(q[:, h * HEAD:(h + 1) * HEAD], k) * scale
        p = jnp.exp(s - jnp.max(s, axis=1, keepdims=True))
        l = jnp.sum(p, axis=1, keepdims=True)
        outs.append(_dot(p.astype(BF16), v) / l)
    o = jnp.concatenate(outs, axis=1).astype(BF16)
    o_ref[...] = x + _dot(o, wo_ref[...])


def _xattn(x, g, wq, kv, wo, *, s, n_mem, tm=512):
    t, d = x.shape
    per_seq = s // tm
    return pl.pallas_call(
        _xattn_kernel,
        out_shape=jax.ShapeDtypeStruct((t, d), F32),
        grid=(t // tm,),
        in_specs=[
            pl.BlockSpec((tm, d), lambda i: (i, 0)),
            pl.BlockSpec((1, d), lambda i: (0, 0)),
            pl.BlockSpec((d, X_WIDTH), lambda i: (0, 0)),
            pl.BlockSpec((n_mem, 2 * X_WIDTH), lambda i: (i // per_seq, 0)),
            pl.BlockSpec((X_WIDTH, d), lambda i: (0, 0)),
        ],
        out_specs=pl.BlockSpec((tm, d), lambda i: (i, 0)),
        compiler_params=_params("parallel"),
        name="xattn",
    )(x, g, wq, kv, wo)


def _cast_kernel(w_ref, o_ref):
    o_ref[...] = w_ref[...].astype(BF16)


def _layer_to_bf16(w, l):
    _, rows, cols = w.shape
    block_rows = next(c for c in (512, 256, 128, 64, 32, 16) if rows % c == 0 and c * cols * 4 <= (4 << 20))
    return pl.pallas_call(
        _cast_kernel,
        out_shape=jax.ShapeDtypeStruct((rows, cols), BF16),
        grid=(rows // block_rows,),
        in_specs=[pl.BlockSpec((None, block_rows, cols), lambda i: (l, i, 0))],
        out_specs=pl.BlockSpec((block_rows, cols), lambda i: (i, 0)),
        compiler_params=_params("parallel"),
        name="cast_bf16",
    )(w)


def _rope_table(s):
    inv = ROPE_THETA ** (-jnp.arange(0, QK_ROPE, 2, dtype=F32) / QK_ROPE)
    ang = jnp.arange(s, dtype=F32)[:, None] * inv[None, :]
    cos, sin = jnp.cos(ang), jnp.sin(ang)
    return jnp.concatenate([cos, cos, -sin, sin], axis=1)


def _swap_halves(w):
    half = w.shape[-1] // 2
    return jnp.concatenate([w[..., half:], w[..., :half]], axis=-1)


def _prep_layer(p, l):
    row = lambda v: v.reshape(1, -1)
    w_in = p['w_in'][l]
    qkv_end, cq_end, ckv_end, kr_end, u_end = 2304, 2816, 3328, 3392, 3904
    w_rope = w_in[:, ckv_end:kr_end]
    w_rest = jnp.concatenate([w_in[:, qkv_end:ckv_end], w_in[:, kr_end:], w_rope, _swap_halves(w_rope)], axis=1)
    wq = p['w_q_up'][l].reshape(-1, B_HEADS, HEAD + QK_ROPE)
    wq = jnp.concatenate([wq, _swap_halves(wq[..., HEAD:])], axis=-1).reshape(-1, B_HEADS * 2 * HEAD)
    eye = jnp.eye(C_BLOCKS, dtype=F32)
    dense = lambda w: jnp.einsum('ncd,nm->ncmd', w, eye).reshape(C_WIDTH, C_WIDTH)
    w_r, w_i = p['w_rg_r'][l], p['w_rg_i'][l]
    w_gates = jnp.concatenate([dense(w_r[0]), dense(w_i[0]), dense(w_r[1]), dense(w_i[1])], axis=1)
    b_r, b_i = p['b_rg_r'][l], p['b_rg_i'][l]
    b_gates = jnp.concatenate([b_r[0], b_i[0], b_r[1], b_i[1]]).reshape(1, -1)
    bf = lambda w: w.astype(BF16)
    big = lambda name: _layer_to_bf16(p[name], l)
    return dict(
        g_ffn1=row(p['g_ffn1'][l]), w1_gate=big('w1_gate'), w1_up=big('w1_up'), w1_down=big('w1_down'),
        g_mix=row(p['g_mix'][l]), w_qkv=bf(w_in[:, :qkv_end]), w_rest=bf(w_rest),
        g_q_lat=row(p['g_q_lat'][l]), g_kv_lat=row(p['g_kv_lat'][l]), w_q=bf(wq), w_kv=bf(p['w_kv_up'][l]),
        conv_w=p['conv_w'][l], conv_b=row(p['conv_b'][l]), w_gates=bf(w_gates), b_gates=b_gates,
        lam=p['rg_lambda'][l],
        g_out_a=row(p['g_out_a'][l]), g_out_b=row(p['g_out_b'][l]), g_out_c=row(p['g_out_c'][l]),
        w_out=big('w_out'),
        g_xattn=row(p['g_xattn'][l]), g_mem=row(p['g_mem'][l]), w_xq=bf(p['w_xq'][l]),
        w_xkv=bf(jnp.concatenate([p['w_xk'][l], p['w_xv'][l]], axis=1)), w_xo=bf(p['w_xo'][l]),
        g_ffn2=row(p['g_ffn2'][l]), w2_gate=big('w2_gate'), w2_up=big('w2_up'), w2_down=big('w2_down'),
    )


def _trunk(x, mem, layers, g_final):
    b, s, d = x.shape
    n_mem = mem.shape[1]
    x = x.reshape(b * s, d)
    mem = mem.reshape(b * n_mem, d)
    tab = _rope_table(s)
    for l, w in enumerate(layers):
        x = _ffn(x, w['g_ffn1'], w['w1_gate'], w['w1_up'], w['w1_down'], g_final, final_norm=False)
        *qkv_by_dil, rest = _inproj(x, w['g_mix'], w['w_qkv'], w['w_rest'], b=b, s=s)
        branches = [_dilated_branch(qkv, dil) for qkv, (_, dil) in zip(qkv_by_dil, DILATED_CONFIGS)]
        q, kt, v = _mla_proj(rest, tab, w['g_q_lat'], w['g_kv_lat'], w['w_q'], w['w_kv'], s=s)
        yb = _mla_attn(q, kt, v, b=b, s=s)
        yc = _rglru(rest, w['conv_w'], w['conv_b'], w['w_gates'], w['b_gates'], w['lam'], b=b, s=s)
        x = _outproj(x, [o for o, _ in branches], [lse for _, lse in branches], yb, yc,
                     w['g_out_a'], w['g_out_b'], w['g_out_c'], w['w_out'])
        kv = _mem_kv(mem, w['g_mem'], w['w_xkv'])
        x = _xattn(x, w['g_xattn'], w['w_xq'], kv, w['w_xo'], s=s, n_mem=n_mem)
        x = _ffn(x, w['g_ffn2'], w['w2_gate'], w['w2_up'], w['w2_down'], g_final,
                 final_norm=(l == len(layers) - 1))
    return x.reshape(b, s, d)


def kernel(x_prompt, x_sample, mem_prompt, mem_sample, g_ffn1, w1_gate, w1_up, w1_down, g_mix, w_in, g_q_lat, w_q_up, g_kv_lat, w_kv_up, conv_w, conv_b, w_rg_r, b_rg_r, w_rg_i, b_rg_i, rg_lambda, g_out_a, g_out_b, g_out_c, w_out, g_xattn, g_mem, w_xq, w_xk, w_xv, w_xo, g_ffn2, w2_gate, w2_up, w2_down, g_final):
    p = dict(g_ffn1=g_ffn1, w1_gate=w1_gate, w1_up=w1_up, w1_down=w1_down,
             g_mix=g_mix, w_in=w_in, g_q_lat=g_q_lat, w_q_up=w_q_up, g_kv_lat=g_kv_lat, w_kv_up=w_kv_up,
             conv_w=conv_w, conv_b=conv_b, w_rg_r=w_rg_r, b_rg_r=b_rg_r, w_rg_i=w_rg_i, b_rg_i=b_rg_i,
             rg_lambda=rg_lambda, g_out_a=g_out_a, g_out_b=g_out_b, g_out_c=g_out_c, w_out=w_out,
             g_xattn=g_xattn, g_mem=g_mem, w_xq=w_xq, w_xk=w_xk, w_xv=w_xv, w_xo=w_xo,
             g_ffn2=g_ffn2, w2_gate=w2_gate, w2_up=w2_up, w2_down=w2_down)
    layers = [_prep_layer(p, l) for l in range(g_ffn1.shape[0])]
    gf = g_final.reshape(1, -1)
    return (_trunk(x_prompt, mem_prompt, layers, gf), _trunk(x_sample, mem_sample, layers, gf))
```

```python
import functools
import math

import jax
import jax.numpy as jnp
from jax import lax
from jax.experimental import pallas as pl
from jax.experimental.pallas import tpu as pltpu

BF16 = jnp.bfloat16
F32 = jnp.float32

D_MODEL = 2048
A_HEADS = 6
HEAD = 128
A_WIDTH = A_HEADS * HEAD
DILATED_CONFIGS = ((128, 1), (512, 4), (2048, 16))
B_HEADS = 6
QK_ROPE = 64
ROPE_THETA = 10000.0
C_WIDTH = 512
C_BLOCKS = 8
C_BLOCK_W = 64
RG_C = 8.0
X_HEADS = 4
X_WIDTH = 512
D_FF = 5632
EPS = 1e-6
NEG_INF = -1e30
LOG2E = math.log2(math.e)
LN2 = math.log(2.0)

V7X_VMEM_LIMIT_BYTES = 56 * 1024 * 1024
LANES = 128
SUBLANES = 8

REST_WIDTH = 4 * 512 + LANES
BAND_HALF = 64


def _params(*sem):
    return pltpu.CompilerParams(dimension_semantics=sem, vmem_limit_bytes=V7X_VMEM_LIMIT_BYTES)


def _rms(x, g):
    return x * lax.rsqrt(jnp.mean(x * x, axis=-1, keepdims=True) + EPS) * g


def _dot(a, b):
    return jnp.dot(a, b, preferred_element_type=F32)


def _dot_nt(a, b):
    return lax.dot_general(a, b, (((1,), (1,)), ((), ())), preferred_element_type=F32)


def _ffn_kernel(x_ref, g_ref, wg_ref, wu_ref, wd_ref, gf_ref, o_ref, h_scr, *, final_norm):
    j = pl.program_id(1)

    @pl.when(j == 0)
    def _():
        h_scr[...] = _rms(x_ref[...], g_ref[...]).astype(BF16)
        o_ref[...] = jnp.zeros_like(o_ref)

    h = h_scr[...]
    a = _dot(h, wg_ref[...])
    u = _dot(h, wu_ref[...])
    o_ref[...] += _dot((jax.nn.silu(a) * u).astype(BF16), wd_ref[...])

    @pl.when(j == pl.num_programs(1) - 1)
    def _():
        rows_per_pass = 256

        def finish(r, carry):
            rows = pl.ds(pl.multiple_of(r * rows_per_pass, rows_per_pass), rows_per_pass)
            y = x_ref[rows, :] + 0.5 * o_ref[rows, :]
            if final_norm:
                y = _rms(y, gf_ref[...])
            o_ref[rows, :] = y
            return carry

        lax.fori_loop(0, o_ref.shape[0] // rows_per_pass, finish, 0)


def _ffn(x, g, wg, wu, wd, gf, *, final_norm, tm=1024, tf=512):
    t, d = x.shape
    f = wg.shape[1]
    return pl.pallas_call(
        functools.partial(_ffn_kernel, final_norm=final_norm),
        out_shape=jax.ShapeDtypeStruct((t, d), F32),
        grid=(t // tm, f // tf),
        in_specs=[
            pl.BlockSpec((tm, d), lambda i, j: (i, 0)),
            pl.BlockSpec((1, d), lambda i, j: (0, 0)),
            pl.BlockSpec((d, tf), lambda i, j: (0, j)),
            pl.BlockSpec((d, tf), lambda i, j: (0, j)),
            pl.BlockSpec((tf, d), lambda i, j: (j, 0)),
            pl.BlockSpec((1, d), lambda i, j: (0, 0)),
        ],
        out_specs=pl.BlockSpec((tm, d), lambda i, j: (i, 0)),
        scratch_shapes=[pltpu.VMEM((tm, d), BF16)],
        compiler_params=_params("parallel", "arbitrary"),
        name="ffn",
    )(x, g, wg, wu, wd, gf)


def _inproj_kernel(x_ref, g_ref, w1_ref, w2_ref, o1_ref, o4_ref, o16_ref, o2_ref, y_scr, c4_scr, *, tm):
    h = _rms(x_ref[...], g_ref[...]).astype(BF16)
    y = _dot(h, w1_ref[...])
    o1_ref[0] = y.astype(BF16)
    n4 = tm // 4
    for cb in range(y.shape[1] // LANES):
        cols = slice(cb * LANES, (cb + 1) * LANES)
        y_scr[cb] = y[:, cols]
        for r4 in range(4):
            c4 = y_scr[cb, pl.ds(r4, n4, stride=4), :]
            o4_ref[r4, :, cols] = c4.astype(BF16)
            c4_scr[cb, r4] = c4
            for j in range(4):
                o16_ref[r4 + 4 * j, :, cols] = c4_scr[cb, r4, pl.ds(j, n4 // 4, stride=4), :].astype(BF16)
    o2_ref[...] = _dot(h, w2_ref[...])


def _inproj(x, g, w_qkv, w_rest, *, b, s, tm=256):
    t, d = x.shape
    n1, n2 = w_qkv.shape[1], w_rest.shape[1]
    per_seq = s // tm
    dils = [dil for _, dil in DILATED_CONFIGS]
    cls_shape = lambda dil: jax.ShapeDtypeStruct((b, dil, s // dil, n1), BF16)
    cls_spec = lambda dil: pl.BlockSpec((None, dil, tm // dil, n1), lambda i: (i // per_seq, 0, i % per_seq, 0))
    resident = dict(pipeline_mode=pl.Buffered(1))
    return pl.pallas_call(
        functools.partial(_inproj_kernel, tm=tm),
        out_shape=(*[cls_shape(dil) for dil in dils], jax.ShapeDtypeStruct((t, n2), F32)),
        grid=(t // tm,),
        in_specs=[
            pl.BlockSpec((tm, d), lambda i: (i, 0)),
            pl.BlockSpec((1, d), lambda i: (0, 0)),
            pl.BlockSpec((d, n1), lambda i: (0, 0), **resident),
            pl.BlockSpec((d, n2), lambda i: (0, 0), **resident),
        ],
        out_specs=(*[cls_spec(dil) for dil in dils], pl.BlockSpec((tm, n2), lambda i: (i, 0))),
        scratch_shapes=[pltpu.VMEM((n1 // LANES, tm, LANES), F32),
                        pltpu.VMEM((n1 // LANES, 4, tm // 4, LANES), F32)],
        compiler_params=_params("parallel"),
        name="inproj",
    )(x, g, w_qkv, w_rest)


def _dilated_kernel(q_ref, kp_ref, kc_ref, kn_ref, vp_ref, vc_ref, vn_ref, bias_ref, o_ref, lse_ref,
                    k_scr, v_scr, *, tq, n_cls, dil, rg):
    i = pl.program_id(1)
    scale = HEAD ** -0.5
    sub = 2 * BAND_HALF
    win = sub + 2 * BAND_HALF
    lane = lax.broadcasted_iota(jnp.int32, (sub, LANES), 1)
    scale2 = scale * LOG2E
    ones = jnp.ones((win, HEAD), BF16)
    for rr in range(rg):
        r = pl.program_id(2) * rg + rr
        k_scr[0:BAND_HALF] = kp_ref[rr]
        k_scr[BAND_HALF:BAND_HALF + tq] = kc_ref[rr]
        k_scr[BAND_HALF + tq:] = kn_ref[rr]
        v_scr[0:BAND_HALF] = vp_ref[rr]
        v_scr[BAND_HALF:BAND_HALF + tq] = vc_ref[rr]
        v_scr[BAND_HALF + tq:] = vn_ref[rr]
        for a in range(tq // sub):
            rows = pl.ds(a * sub, sub) if dil == 1 else pl.ds(a * sub * dil + r, sub, stride=dil)
            edge = a == 0 or a == tq // sub - 1
            kidx = i * tq + (a * sub - BAND_HALF) + lax.broadcasted_iota(jnp.int32, (1, win), 1)
            valid = (kidx >= 0) & (kidx < n_cls)
            lse_all = jnp.zeros((sub, LANES), F32)
            for h in range(A_HEADS):
                cols = slice(h * HEAD, (h + 1) * HEAD)
                q = q_ref[rr, a * sub:(a + 1) * sub, cols]
                k = k_scr[a * sub:a * sub + win, cols]
                v = v_scr[a * sub:a * sub + win, cols]
                s2 = _dot_nt(q, k) * scale2 + bias_ref[h]
                if edge:
                    s2 = jnp.where(valid, s2, NEG_INF)
                m2 = jnp.max(s2, axis=1, keepdims=True)
                p = jnp.exp2(s2 - m2).astype(BF16)
                pv = _dot(p, jnp.concatenate([v, ones], axis=1))
                l = pv[:, HEAD:]
                o_ref[h, rows, :] = pv[:, :HEAD] / l
                lse_all = jnp.where(lane == h, m2 * LN2 + jnp.log(l), lse_all)
            lse_ref[rows, :] = lse_all


def _band_bias(dil):
    slopes = 2.0 ** (-8.0 * jnp.arange(1, A_HEADS + 1, dtype=F32) / A_HEADS)
    sub, win = 2 * BAND_HALF, 4 * BAND_HALF
    rel = jnp.abs(BAND_HALF + jnp.arange(sub)[:, None] - jnp.arange(win)[None, :])
    bias = -slopes[:, None, None] * (dil * rel).astype(F32)[None]
    return jnp.where((rel <= BAND_HALF)[None], bias * LOG2E, NEG_INF)


def _dilated_branch(qkv, dil):
    b, _, n_cls, _ = qkv.shape
    tq = min(512, n_cls, 2048 // dil)
    nq = n_cls // tq
    nh = tq // BAND_HALF
    last_halo = n_cls // BAND_HALF - 1
    rg = min(dil, 512 // tq)

    def cur(which):
        return pl.BlockSpec((None, rg, tq, A_WIDTH), lambda bi, i, r: (bi, r, i, which))

    def prev(which):
        return pl.BlockSpec((None, rg, BAND_HALF, A_WIDTH),
                            lambda bi, i, r: (bi, r, jnp.maximum(i * nh - 1, 0), which))

    def nxt(which):
        return pl.BlockSpec((None, rg, BAND_HALF, A_WIDTH),
                            lambda bi, i, r: (bi, r, jnp.minimum((i + 1) * nh, last_halo), which))

    t = b * n_cls * dil
    return pl.pallas_call(
        functools.partial(_dilated_kernel, tq=tq, n_cls=n_cls, dil=dil, rg=rg),
        out_shape=(jax.ShapeDtypeStruct((A_HEADS, t, HEAD), F32), jax.ShapeDtypeStruct((t, LANES), F32)),
        grid=(b, nq, dil // rg),
        in_specs=[cur(0), prev(1), cur(1), nxt(1), prev(2), cur(2), nxt(2),
                  pl.BlockSpec((A_HEADS, 2 * BAND_HALF, 4 * BAND_HALF), lambda bi, i, r: (0, 0, 0))],
        out_specs=(pl.BlockSpec((A_HEADS, tq * dil, HEAD), lambda bi, i, r: (0, bi * nq + i, 0)),
                   pl.BlockSpec((tq * dil, LANES), lambda bi, i, r: (bi * nq + i, 0))),
        scratch_shapes=[pltpu.VMEM((tq + 2 * BAND_HALF, A_WIDTH), BF16),
                        pltpu.VMEM((tq + 2 * BAND_HALF, A_WIDTH), BF16)],
        compiler_params=_params("parallel", "parallel", "arbitrary"),
        name=f"dilated{dil}",
    )(qkv, qkv, qkv, qkv, qkv, qkv, qkv, _band_bias(dil))


def _rotate(y, tab):
    z = y * tab
    r = z + pltpu.roll(z, QK_ROPE, axis=1)
    lane = lax.broadcasted_iota(jnp.int32, r.shape, 1)
    return jnp.where(lane < QK_ROPE, r, 0.0)


def _mla_proj_kernel(cq_ref, ckv_ref, kr_ref, tab_ref, gq_ref, gkv_ref, wq_ref, wkv_ref,
                     q_ref, kt_ref, v_ref):
    cq = _rms(cq_ref[...], gq_ref[...]).astype(BF16)
    ckv = _rms(ckv_ref[...], gkv_ref[...]).astype(BF16)
    tab = tab_ref[...]
    kr_t = _rotate(kr_ref[...], tab).T.astype(BF16)
    ones = jnp.ones((v_ref.shape[0], HEAD), BF16)
    for h in range(B_HEADS):
        lo, mid, hi = 2 * h * HEAD, (2 * h + 1) * HEAD, (2 * h + 2) * HEAD
        q = _dot(cq, wq_ref[:, lo:hi])
        q_ref[:, lo:mid] = q[:, :HEAD].astype(BF16)
        q_ref[:, mid:hi] = _rotate(q[:, HEAD:], tab).astype(BF16)
        kv = _dot(ckv, wkv_ref[:, lo:hi])
        kt_ref[h, :HEAD, :] = kv[:, :HEAD].T.astype(BF16)
        kt_ref[h, HEAD:, :] = kr_t
        v_ref[:, lo:mid] = kv[:, HEAD:].astype(BF16)
        v_ref[:, mid:hi] = ones


def _mla_proj(rest, tab, gq, gkv, wq, wkv, *, s, tm=512):
    t = rest.shape[0]
    lat = 512
    width = B_HEADS * 2 * HEAD
    pos_blocks = s // tm
    return pl.pallas_call(
        _mla_proj_kernel,
        out_shape=(jax.ShapeDtypeStruct((t, width), BF16),
                   jax.ShapeDtypeStruct((B_HEADS, t // tm, 2 * HEAD, tm), BF16),
                   jax.ShapeDtypeStruct((t, width), BF16)),
        grid=(t // tm,),
        in_specs=[
            pl.BlockSpec((tm, lat), lambda i: (i, 0)),
            pl.BlockSpec((tm, lat), lambda i: (i, 1)),
            pl.BlockSpec((tm, LANES), lambda i: (i, 4 * lat // LANES)),
            pl.BlockSpec((tm, LANES), lambda i: (i % pos_blocks, 0)),
            pl.BlockSpec((1, lat), lambda i: (0, 0)),
            pl.BlockSpec((1, lat), lambda i: (0, 0)),
            pl.BlockSpec((lat, width), lambda i: (0, 0)),
            pl.BlockSpec((lat, width), lambda i: (0, 0)),
        ],
        out_specs=(pl.BlockSpec((tm, width), lambda i: (i, 0)),
                   pl.BlockSpec((B_HEADS, None, 2 * HEAD, tm), lambda i: (0, i, 0, 0)),
                   pl.BlockSpec((tm, width), lambda i: (i, 0))),
        compiler_params=_params("parallel"),
        name="mla_proj",
    )(rest, rest, rest, tab, gq, gkv, wq, wkv)


def _mla_attn_kernel(q_ref, kt_ref, v_ref, o_ref, m_scr, a_scr, acc_scr, s_scr, p_scr, mx_scr, *, tq, tk, nk):
    c = (HEAD + QK_ROPE) ** -0.5 * math.log2(math.e)
    q = q_ref[...]
    nl = tk // LANES

    def chunk(kk):
        return pl.ds(pl.multiple_of(kk * tk, tk), tk)

    def scores(kk, slot):
        s = _dot(q, kt_ref[kk])
        s_scr[slot] = s
        mx = s[:, :LANES]
        for j in range(1, nl):
            mx = jnp.maximum(mx, s[:, j * LANES:(j + 1) * LANES])
        mx_scr[slot] = mx

    def softmax(slot, first):
        s = s_scr[slot]
        m_new = jnp.broadcast_to(jnp.max(mx_scr[slot], axis=1, keepdims=True), (tq, LANES))
        if not first:
            m_old = m_scr[...]
            m_new = jnp.maximum(m_old, m_new)
            a_scr[slot] = jnp.exp2((m_old - m_new) * c)
        for j in range(nl):
            cols = slice(j * LANES, (j + 1) * LANES)
            p_scr[slot, :, cols] = jnp.exp2((s[:, cols] - m_new) * c).astype(BF16)
        m_scr[...] = m_new

    def values(kk, slot, first):
        pv = _dot(p_scr[slot], v_ref[chunk(kk), :])
        if first:
            acc_scr[...] = pv
        else:
            alpha = a_scr[slot]
            acc_scr[...] = jnp.concatenate([alpha, alpha], axis=1) * acc_scr[...] + pv

    scores(0, 0)
    softmax(0, True)
    scores(1, 1)
    softmax(1, False)
    values(0, 0, True)
    scores(2, 0)

    pairs = next(n for n in (3, 2, 1) if (nk - 4) % (2 * n) == 0)

    def body(g, carry):
        for u in range(pairs):
            kk = 2 * (pairs * g + u + 1)
            softmax(0, False)
            values(kk - 1, 1, False)
            scores(kk + 1, 1)
            softmax(1, False)
            values(kk, 0, False)
            scores(kk + 2, 0)
        return carry

    lax.fori_loop(0, (nk - 4) // (2 * pairs), body, 0)
    softmax(0, False)
    values(nk - 3, 1, False)
    scores(nk - 1, 1)
    softmax(1, False)
    values(nk - 2, 0, False)
    values(nk - 1, 1, False)
    o_ref[...] = acc_scr[:, :HEAD] / acc_scr[:, HEAD:]


def _mla_attn(q, kt, v, *, b, s, tq=1024):
    t = b * s
    nq = s // tq
    tk = kt.shape[-1]
    nk = s // tk
    assert nk % 2 == 0 and nk >= 4
    return pl.pallas_call(
        functools.partial(_mla_attn_kernel, tq=tq, tk=tk, nk=nk),
        out_shape=jax.ShapeDtypeStruct((t, B_HEADS * HEAD), F32),
        grid=(b, B_HEADS, nq),
        in_specs=[
            pl.BlockSpec((tq, 2 * HEAD), lambda bi, h, i: (bi * nq + i, h)),
            pl.BlockSpec((None, nk, 2 * HEAD, tk), lambda bi, h, i: (h, bi, 0, 0)),
            pl.BlockSpec((s, 2 * HEAD), lambda bi, h, i: (bi, h)),
        ],
        out_specs=pl.BlockSpec((tq, HEAD), lambda bi, h, i: (bi * nq + i, h)),
        scratch_shapes=[pltpu.VMEM((tq, LANES), F32), pltpu.VMEM((2, tq, LANES), F32),
                        pltpu.VMEM((tq, 2 * HEAD), F32), pltpu.VMEM((2, tq, tk), F32),
                        pltpu.VMEM((2, tq, tk), BF16), pltpu.VMEM((2, tq, LANES), F32)],
        compiler_params=_params("parallel", "parallel", "arbitrary"),
        name="mla_attn",
    )(q, kt, v)


def _softplus(x):
    return jnp.maximum(x, 0.0) + jnp.log1p(jnp.exp(-jnp.abs(x)))


def _scan_tile(a, b, row, reverse):
    for sh in (1, 2, 4):
        if reverse:
            keep = row < SUBLANES - sh
            a_s = pltpu.roll(a, SUBLANES - sh, axis=0)
            b_s = pltpu.roll(b, SUBLANES - sh, axis=0)
        else:
            keep = row >= sh
            a_s = pltpu.roll(a, sh, axis=0)
            b_s = pltpu.roll(b, sh, axis=0)
        b = b + a * jnp.where(keep, b_s, 0.0)
        a = a * jnp.where(keep, a_s, 1.0)
    return a, b


def _rglru_fwd_kernel(uc_ref, up_ref, un_ref, cw_ref, cb_ref, wg_ref, bg_ref, lam_ref,
                      hf_ref, ab_ref, bb_ref, ubuf, af_scr, bf_scr, carry, *, ts):
    i = pl.program_id(1)
    ns = pl.num_programs(1)
    ubuf[0:SUBLANES] = jnp.where(i > 0, up_ref[...], 0.0)
    ubuf[SUBLANES:SUBLANES + ts] = uc_ref[...]
    ubuf[SUBLANES + ts:] = jnp.where(i < ns - 1, un_ref[...], 0.0)
    u = cb_ref[...] + cw_ref[0:1, :] * ubuf[SUBLANES - 2:SUBLANES - 2 + ts]
    u = u + cw_ref[1:2, :] * ubuf[SUBLANES - 1:SUBLANES - 1 + ts]
    u = u + cw_ref[2:3, :] * ubuf[SUBLANES:SUBLANES + ts]
    u = u + cw_ref[3:4, :] * ubuf[SUBLANES + 1:SUBLANES + 1 + ts]
    gates = _dot(u.astype(BF16), wg_ref[...]) + bg_ref[...]
    sp = _softplus(-lam_ref[...])
    for d in range(2):
        r = jax.nn.sigmoid(gates[:, (2 * d) * C_WIDTH:(2 * d + 1) * C_WIDTH])
        ig = jax.nn.sigmoid(gates[:, (2 * d + 1) * C_WIDTH:(2 * d + 2) * C_WIDTH])
        log_a = -RG_C * r * sp[d:d + 1, :]
        a = jnp.exp(log_a)
        bterm = jnp.sqrt(-jnp.tanh(log_a) * (a * a + 1.0)) * (ig * u)
        if d == 0:
            af_scr[...] = a
            bf_scr[...] = bterm
        else:
            ab_ref[...] = a
            bb_ref[...] = bterm

    @pl.when(i == 0)
    def _():
        carry[...] = jnp.zeros_like(carry)

    row = lax.broadcasted_iota(jnp.int32, (SUBLANES, C_WIDTH), 0)

    def body(t, c):
        rows = pl.ds(pl.multiple_of(t * SUBLANES, SUBLANES), SUBLANES)
        a, b = _scan_tile(af_scr[rows, :], bf_scr[rows, :], row, reverse=False)
        h = b + a * carry[...]
        hf_ref[rows, :] = h
        carry[...] = h[SUBLANES - 1:SUBLANES, :]
        return c

    lax.fori_loop(0, ts // SUBLANES, body, 0)


def _rglru_bwd_kernel(a_ref, b_ref, hf_ref, g_ref, y_ref, carry, *, ts):
    @pl.when(pl.program_id(1) == 0)
    def _():
        carry[...] = jnp.zeros_like(carry)

    row = lax.broadcasted_iota(jnp.int32, (SUBLANES, C_WIDTH), 0)
    nt = ts // SUBLANES

    def body(t, c):
        rows = pl.ds(pl.multiple_of((nt - 1 - t) * SUBLANES, SUBLANES), SUBLANES)
        a, b = _scan_tile(a_ref[rows, :], b_ref[rows, :], row, reverse=True)
        h = b + a * carry[...]
        carry[...] = h[0:1, :]
        y_ref[rows, :] = jax.nn.gelu(g_ref[rows, :]) * (hf_ref[rows, :] + h)
        return c

    lax.fori_loop(0, nt, body, 0)


def _rglru(rest, conv_w, conv_b, w_gates, b_gates, lam, *, b, s, ts=512):
    t = b * s
    ns = s // ts
    hb = ts // SUBLANES
    last = s // SUBLANES - 1
    u_col, g_col = 2, 3
    tile = lambda bi, i: (bi * ns + i, 0)
    hf, ab, bb = pl.pallas_call(
        functools.partial(_rglru_fwd_kernel, ts=ts),
        out_shape=tuple(jax.ShapeDtypeStruct((t, C_WIDTH), F32) for _ in range(3)),
        grid=(b, ns),
        in_specs=[
            pl.BlockSpec((ts, C_WIDTH), lambda bi, i: (bi * ns + i, u_col)),
            pl.BlockSpec((SUBLANES, C_WIDTH),
                         lambda bi, i: (bi * ns * hb + jnp.maximum(i * hb - 1, 0), u_col)),
            pl.BlockSpec((SUBLANES, C_WIDTH),
                         lambda bi, i: (bi * ns * hb + jnp.minimum((i + 1) * hb, last), u_col)),
            pl.BlockSpec((4, C_WIDTH), lambda bi, i: (0, 0)),
            pl.BlockSpec((1, C_WIDTH), lambda bi, i: (0, 0)),
            pl.BlockSpec((C_WIDTH, 4 * C_WIDTH), lambda bi, i: (0, 0)),
            pl.BlockSpec((1, 4 * C_WIDTH), lambda bi, i: (0, 0)),
            pl.BlockSpec((2, C_WIDTH), lambda bi, i: (0, 0)),
        ],
        out_specs=tuple(pl.BlockSpec((ts, C_WIDTH), tile) for _ in range(3)),
        scratch_shapes=[pltpu.VMEM((ts + 2 * SUBLANES, C_WIDTH), F32),
                        pltpu.VMEM((ts, C_WIDTH), F32), pltpu.VMEM((ts, C_WIDTH), F32),
                        pltpu.VMEM((1, C_WIDTH), F32)],
        compiler_params=_params("parallel", "arbitrary"),
        name="rglru_fwd",
    )(rest, rest, rest, conv_w, conv_b, w_gates, b_gates, lam)
    rev = lambda bi, i: (bi * ns + ns - 1 - i, 0)
    return pl.pallas_call(
        functools.partial(_rglru_bwd_kernel, ts=ts),
        out_shape=jax.ShapeDtypeStruct((t, C_WIDTH), F32),
        grid=(b, ns),
        in_specs=[
            pl.BlockSpec((ts, C_WIDTH), rev),
            pl.BlockSpec((ts, C_WIDTH), rev),
            pl.BlockSpec((ts, C_WIDTH), rev),
            pl.BlockSpec((ts, C_WIDTH), lambda bi, i: (bi * ns + ns - 1 - i, g_col)),
        ],
        out_specs=pl.BlockSpec((ts, C_WIDTH), rev),
        scratch_shapes=[pltpu.VMEM((1, C_WIDTH), F32)],
        compiler_params=_params("parallel", "arbitrary"),
        name="rglru_bwd",
    )(ab, bb, hf, rest)


def _outproj_kernel(x_ref, o1_ref, o2_ref, o3_ref, l1_ref, l2_ref, l3_ref, yb_ref, yc_ref,
                    ga_ref, gb_ref, gc_ref, w_ref, out_ref):
    l1, l2, l3 = l1_ref[...], l2_ref[...], l3_ref[...]
    m = jnp.maximum(jnp.maximum(l1, l2), l3)
    e1, e2, e3 = jnp.exp(l1 - m), jnp.exp(l2 - m), jnp.exp(l3 - m)
    z = e1 + e2 + e3
    w1, w2, w3 = e1 / z, e2 / z, e3 / z
    parts = []
    for h in range(A_HEADS):
        parts.append(w1[:, h:h + 1] * o1_ref[h] + w2[:, h:h + 1] * o2_ref[h] + w3[:, h:h + 1] * o3_ref[h])
    ya = jnp.concatenate(parts, axis=1)
    b0, c0 = A_WIDTH, 2 * A_WIDTH
    y = _dot(_rms(yb_ref[...], gb_ref[...]).astype(BF16), w_ref[b0:c0, :])
    y = y + _dot(_rms(yc_ref[...], gc_ref[...]).astype(BF16), w_ref[c0:, :])
    y = y + _dot(_rms(ya, ga_ref[...]).astype(BF16), w_ref[0:b0, :])
    out_ref[...] = x_ref[...] + y


def _outproj(x, o_branches, lse_branches, yb, yc, ga, gb, gc, w, *, tm=512):
    t, d = x.shape
    row = lambda n: pl.BlockSpec((tm, n), lambda i: (i, 0))
    full = lambda r, n: pl.BlockSpec((r, n), lambda i: (0, 0))
    heads = pl.BlockSpec((A_HEADS, tm, HEAD), lambda i: (0, i, 0))
    return pl.pallas_call(
        _outproj_kernel,
        out_shape=jax.ShapeDtypeStruct((t, d), F32),
        grid=(t // tm,),
        in_specs=[row(d), heads, heads, heads, row(LANES), row(LANES), row(LANES),
                  row(A_WIDTH), row(C_WIDTH), full(1, A_WIDTH), full(1, A_WIDTH), full(1, C_WIDTH),
                  full(d, d)],
        out_specs=row(d),
        compiler_params=_params("parallel"),
        name="outproj",
    )(x, *o_branches, *lse_branches, yb, yc, ga, gb, gc, w)


def _mem_kv_kernel(m_ref, g_ref, w_ref, o_ref):
    o_ref[...] = _dot(_rms(m_ref[...], g_ref[...]).astype(BF16), w_ref[...]).astype(BF16)


def _mem_kv(mem, g, w_kv):
    t, d = mem.shape
    n = w_kv.shape[1]
    tm = 256
    return pl.pallas_call(
        _mem_kv_kernel,
        out_shape=jax.ShapeDtypeStruct((t, n), BF16),
        grid=(t // tm,),
        in_specs=[pl.BlockSpec((tm, d), lambda i: (i, 0)), pl.BlockSpec((1, d), lambda i: (0, 0)),
                  pl.BlockSpec((d, n), lambda i: (0, 0))],
        out_specs=pl.BlockSpec((tm, n), lambda i: (i, 0)),
        compiler_params=_params("parallel"),
        name="mem_kv",
    )(mem, g, w_kv)


def _xattn_kernel(x_ref, g_ref, wq_ref, kv_ref, wo_ref, o_ref):
    x = x_ref[...]
    q = _dot(_rms(x, g_ref[...]).astype(BF16), wq_ref[...]).astype(BF16)
    c = HEAD ** -0.5 * LOG2E
    ones = jnp.ones((kv_ref.shape[0], HEAD), BF16)
    outs = []
    for h in range(X_HEADS):
        k = kv_ref[:, h * HEAD:(h + 1) * HEAD]
        v = kv_ref[:, X_WIDTH + h * HEAD:X_WIDTH + (h + 1) * HEAD]
        s = _dot_nt(q[:, h * HEAD:(h + 1) * HEAD], k)
        p = jnp.exp2((s - jnp.max(s, axis=1, keepdims=True)) * c).astype(BF16)
        pv = _dot(p, jnp.concatenate([v, ones], axis=1))
        outs.append(pv[:, :HEAD] / pv[:, HEAD:])
    o = jnp.concatenate(outs, axis=1).astype(BF16)
    o_ref[...] = x + _dot(o, wo_ref[...])


def _xattn(x, g, wq, kv, wo, *, s, n_mem, tm=512):
    t, d = x.shape
    per_seq = s // tm
    return pl.pallas_call(
        _xattn_kernel,
        out_shape=jax.ShapeDtypeStruct((t, d), F32),
        grid=(t // tm,),
        in_specs=[
            pl.BlockSpec((tm, d), lambda i: (i, 0)),
            pl.BlockSpec((1, d), lambda i: (0, 0)),
            pl.BlockSpec((d, X_WIDTH), lambda i: (0, 0)),
            pl.BlockSpec((n_mem, 2 * X_WIDTH), lambda i: (i // per_seq, 0)),
            pl.BlockSpec((X_WIDTH, d), lambda i: (0, 0)),
        ],
        out_specs=pl.BlockSpec((tm, d), lambda i: (i, 0)),
        compiler_params=_params("parallel"),
        name="xattn",
    )(x, g, wq, kv, wo)


def _cast_kernel(w_ref, o_ref):
    o_ref[...] = w_ref[...].astype(BF16)


def _layer_to_bf16(w, l):
    _, rows, cols = w.shape
    block_rows = next(c for c in (512, 256, 128, 64, 32, 16) if rows % c == 0 and c * cols * 4 <= (4 << 20))
    return pl.pallas_call(
        _cast_kernel,
        out_shape=jax.ShapeDtypeStruct((rows, cols), BF16),
        grid=(rows // block_rows,),
        in_specs=[pl.BlockSpec((None, block_rows, cols), lambda i: (l, i, 0))],
        out_specs=pl.BlockSpec((block_rows, cols), lambda i: (i, 0)),
        compiler_params=_params("parallel"),
        name="cast_bf16",
    )(w)


def _rope_table(s):
    inv = ROPE_THETA ** (-jnp.arange(0, QK_ROPE, 2, dtype=F32) / QK_ROPE)
    ang = jnp.arange(s, dtype=F32)[:, None] * inv[None, :]
    cos, sin = jnp.cos(ang), jnp.sin(ang)
    return jnp.concatenate([cos, cos, -sin, sin], axis=1)


def _swap_halves(w):
    half = w.shape[-1] // 2
    return jnp.concatenate([w[..., half:], w[..., :half]], axis=-1)


def _prep_layer(p, l):
    row = lambda v: v.reshape(1, -1)
    w_in = p['w_in'][l]
    qkv_end, cq_end, ckv_end, kr_end, u_end = 2304, 2816, 3328, 3392, 3904
    w_rope = w_in[:, ckv_end:kr_end]
    w_rest = jnp.concatenate([w_in[:, qkv_end:ckv_end], w_in[:, kr_end:], w_rope, _swap_halves(w_rope)], axis=1)
    wq = p['w_q_up'][l].reshape(-1, B_HEADS, HEAD + QK_ROPE)
    wq = jnp.concatenate([wq, _swap_halves(wq[..., HEAD:])], axis=-1).reshape(-1, B_HEADS * 2 * HEAD)
    eye = jnp.eye(C_BLOCKS, dtype=F32)
    dense = lambda w: jnp.einsum('ncd,nm->ncmd', w, eye).reshape(C_WIDTH, C_WIDTH)
    w_r, w_i = p['w_rg_r'][l], p['w_rg_i'][l]
    w_gates = jnp.concatenate([dense(w_r[0]), dense(w_i[0]), dense(w_r[1]), dense(w_i[1])], axis=1)
    b_r, b_i = p['b_rg_r'][l], p['b_rg_i'][l]
    b_gates = jnp.concatenate([b_r[0], b_i[0], b_r[1], b_i[1]]).reshape(1, -1)
    bf = lambda w: w.astype(BF16)
    big = lambda name: _layer_to_bf16(p[name], l)
    return dict(
        g_ffn1=row(p['g_ffn1'][l]), w1_gate=big('w1_gate'), w1_up=big('w1_up'), w1_down=big('w1_down'),
        g_mix=row(p['g_mix'][l]), w_qkv=bf(w_in[:, :qkv_end]), w_rest=bf(w_rest),
        g_q_lat=row(p['g_q_lat'][l]), g_kv_lat=row(p['g_kv_lat'][l]), w_q=bf(wq), w_kv=bf(p['w_kv_up'][l]),
        conv_w=p['conv_w'][l], conv_b=row(p['conv_b'][l]), w_gates=bf(w_gates), b_gates=b_gates,
        lam=p['rg_lambda'][l],
        g_out_a=row(p['g_out_a'][l]), g_out_b=row(p['g_out_b'][l]), g_out_c=row(p['g_out_c'][l]),
        w_out=big('w_out'),
        g_xattn=row(p['g_xattn'][l]), g_mem=row(p['g_mem'][l]), w_xq=bf(p['w_xq'][l]),
        w_xkv=bf(jnp.concatenate([p['w_xk'][l], p['w_xv'][l]], axis=1)), w_xo=bf(p['w_xo'][l]),
        g_ffn2=row(p['g_ffn2'][l]), w2_gate=big('w2_gate'), w2_up=big('w2_up'), w2_down=big('w2_down'),
    )


def _trunk(x, mem, layers, g_final):
    b, s, d = x.shape
    n_mem = mem.shape[1]
    x = x.reshape(b * s, d)
    mem = mem.reshape(b * n_mem, d)
    tab = _rope_table(s)
    for l, w in enumerate(layers):
        x = _ffn(x, w['g_ffn1'], w['w1_gate'], w['w1_up'], w['w1_down'], g_final, final_norm=False)
        *qkv_by_dil, rest = _inproj(x, w['g_mix'], w['w_qkv'], w['w_rest'], b=b, s=s)
        branches = [_dilated_branch(qkv, dil) for qkv, (_, dil) in zip(qkv_by_dil, DILATED_CONFIGS)]
        q, kt, v = _mla_proj(rest, tab, w['g_q_lat'], w['g_kv_lat'], w['w_q'], w['w_kv'], s=s)
        yb = _mla_attn(q, kt, v, b=b, s=s)
        yc = _rglru(rest, w['conv_w'], w['conv_b'], w['w_gates'], w['b_gates'], w['lam'], b=b, s=s)
        x = _outproj(x, [o for o, _ in branches], [lse for _, lse in branches], yb, yc,
                     w['g_out_a'], w['g_out_b'], w['g_out_c'], w['w_out'])
        kv = _mem_kv(mem, w['g_mem'], w['w_xkv'])
        x = _xattn(x, w['g_xattn'], w['w_xq'], kv, w['w_xo'], s=s, n_mem=n_mem)
        x = _ffn(x, w['g_ffn2'], w['w2_gate'], w['w2_up'], w['w2_down'], g_final,
                 final_norm=(l == len(layers) - 1))
    return x.reshape(b, s, d)


def kernel(x_prompt, x_sample, mem_prompt, mem_sample, g_ffn1, w1_gate, w1_up, w1_down, g_mix, w_in, g_q_lat, w_q_up, g_kv_lat, w_kv_up, conv_w, conv_b, w_rg_r, b_rg_r, w_rg_i, b_rg_i, rg_lambda, g_out_a, g_out_b, g_out_c, w_out, g_xattn, g_mem, w_xq, w_xk, w_xv, w_xo, g_ffn2, w2_gate, w2_up, w2_down, g_final):
    p = dict(g_ffn1=g_ffn1, w1_gate=w1_gate, w1_up=w1_up, w1_down=w1_down,
             g_mix=g_mix, w_in=w_in, g_q_lat=g_q_lat, w_q_up=w_q_up, g_kv_lat=g_kv_lat, w_kv_up=w_kv_up,
             conv_w=conv_w, conv_b=conv_b, w_rg_r=w_rg_r, b_rg_r=b_rg_r, w_rg_i=w_rg_i, b_rg_i=b_rg_i,
             rg_lambda=rg_lambda, g_out_a=g_out_a, g_out_b=g_out_b, g_out_c=g_out_c, w_out=w_out,
             g_xattn=g_xattn, g_mem=g_mem, w_xq=w_xq, w_xk=w_xk, w_xv=w_xv, w_xo=w_xo,
             g_ffn2=g_ffn2, w2_gate=w2_gate, w2_up=w2_up, w2_down=w2_down)
    layers = [_prep_layer(p, l) for l in range(g_ffn1.shape[0])]
    gf = g_final.reshape(1, -1)
    return (_trunk(x_prompt, mem_prompt, layers, gf), _trunk(x_sample, mem_sample, layers, gf))
```

```python
import functools
import math

import jax
import jax.numpy as jnp
from jax import lax
from jax.experimental import pallas as pl
from jax.experimental.pallas import tpu as pltpu

BF16 = jnp.bfloat16
F32 = jnp.float32

D_MODEL = 2048
A_HEADS = 6
HEAD = 128
A_WIDTH = A_HEADS * HEAD
DILATED_CONFIGS = ((128, 1), (512, 4), (2048, 16))
B_HEADS = 6
QK_ROPE = 64
ROPE_THETA = 10000.0
C_WIDTH = 512
C_BLOCKS = 8
C_BLOCK_W = 64
RG_C = 8.0
X_HEADS = 4
X_WIDTH = 512
D_FF = 5632
EPS = 1e-6
NEG_INF = -1e30
LOG2E = math.log2(math.e)
LN2 = math.log(2.0)

V7X_VMEM_LIMIT_BYTES = 56 * 1024 * 1024
LANES = 128
SUBLANES = 8

REST_WIDTH = 4 * 512 + LANES
BAND_HALF = 64
SCAN_UNROLL = 4


def _params(*sem):
    return pltpu.CompilerParams(dimension_semantics=sem, vmem_limit_bytes=V7X_VMEM_LIMIT_BYTES)


def _rms(x, g):
    return x * lax.rsqrt(jnp.mean(x * x, axis=-1, keepdims=True) + EPS) * g


def _dot(a, b):
    return jnp.dot(a, b, preferred_element_type=F32)


def _dot_nt(a, b):
    return lax.dot_general(a, b, (((1,), (1,)), ((), ())), preferred_element_type=F32)


def _ffn_kernel(x_ref, g_ref, wg_ref, wu_ref, wd_ref, gf_ref, o_ref, h_scr, *, final_norm):
    j = pl.program_id(1)

    def partial_out(h):
        a = _dot(h, wg_ref[...])
        u = _dot(h, wu_ref[...])
        return _dot((jax.nn.silu(a) * u).astype(BF16), wd_ref[...])

    last = pl.num_programs(1) - 1

    @pl.when(j == 0)
    def _():
        h = _rms(x_ref[...], g_ref[...]).astype(BF16)
        h_scr[...] = h
        o_ref[...] = partial_out(h)

    @pl.when((j > 0) & (j < last))
    def _():
        o_ref[...] += partial_out(h_scr[...])

    @pl.when(j == last)
    def _():
        o_ref[...] = x_ref[...] + 0.5 * (o_ref[...] + partial_out(h_scr[...]))

    if final_norm:
        @pl.when(j == last)
        def _():
            rows_per_pass = 256

            def finish(r, carry):
                rows = pl.ds(pl.multiple_of(r * rows_per_pass, rows_per_pass), rows_per_pass)
                o_ref[rows, :] = _rms(o_ref[rows, :], gf_ref[...])
                return carry

            lax.fori_loop(0, o_ref.shape[0] // rows_per_pass, finish, 0)


def _ffn(x, g, wg, wu, wd, gf, *, final_norm, tm=1024, tf=512):
    t, d = x.shape
    f = wg.shape[1]
    return pl.pallas_call(
        functools.partial(_ffn_kernel, final_norm=final_norm),
        out_shape=jax.ShapeDtypeStruct((t, d), F32),
        grid=(t // tm, f // tf),
        in_specs=[
            pl.BlockSpec((tm, d), lambda i, j: (i, 0)),
            pl.BlockSpec((1, d), lambda i, j: (0, 0)),
            pl.BlockSpec((d, tf), lambda i, j: (0, j)),
            pl.BlockSpec((d, tf), lambda i, j: (0, j)),
            pl.BlockSpec((tf, d), lambda i, j: (j, 0)),
            pl.BlockSpec((1, d), lambda i, j: (0, 0)),
        ],
        out_specs=pl.BlockSpec((tm, d), lambda i, j: (i, 0)),
        scratch_shapes=[pltpu.VMEM((tm, d), BF16)],
        compiler_params=_params("parallel", "arbitrary"),
        name="ffn",
    )(x, g, wg, wu, wd, gf)


def _inproj_kernel(x_ref, g_ref, w1_ref, w2_ref, o1_ref, o4_ref, o16_ref, o2_ref, y_scr, c4_scr, *, tm):
    h = _rms(x_ref[...], g_ref[...]).astype(BF16)
    y = _dot(h, w1_ref[...])
    o1_ref[0] = y.astype(BF16)
    n4 = tm // 4
    for cb in range(y.shape[1] // LANES):
        cols = slice(cb * LANES, (cb + 1) * LANES)
        y_scr[cb] = y[:, cols]
        for r4 in range(4):
            c4 = y_scr[cb, pl.ds(r4, n4, stride=4), :]
            o4_ref[r4, :, cols] = c4.astype(BF16)
            c4_scr[cb, r4] = c4
            for j in range(4):
                o16_ref[r4 + 4 * j, :, cols] = c4_scr[cb, r4, pl.ds(j, n4 // 4, stride=4), :].astype(BF16)
    o2_ref[...] = _dot(h, w2_ref[...])


def _inproj(x, g, w_qkv, w_rest, *, b, s, tm=256):
    t, d = x.shape
    n1, n2 = w_qkv.shape[1], w_rest.shape[1]
    per_seq = s // tm
    dils = [dil for _, dil in DILATED_CONFIGS]
    cls_shape = lambda dil: jax.ShapeDtypeStruct((b, dil, s // dil, n1), BF16)
    cls_spec = lambda dil: pl.BlockSpec((None, dil, tm // dil, n1), lambda i: (i // per_seq, 0, i % per_seq, 0))
    resident = dict(pipeline_mode=pl.Buffered(1))
    return pl.pallas_call(
        functools.partial(_inproj_kernel, tm=tm),
        out_shape=(*[cls_shape(dil) for dil in dils], jax.ShapeDtypeStruct((t, n2), F32)),
        grid=(t // tm,),
        in_specs=[
            pl.BlockSpec((tm, d), lambda i: (i, 0)),
            pl.BlockSpec((1, d), lambda i: (0, 0)),
            pl.BlockSpec((d, n1), lambda i: (0, 0), **resident),
            pl.BlockSpec((d, n2), lambda i: (0, 0), **resident),
        ],
        out_specs=(*[cls_spec(dil) for dil in dils], pl.BlockSpec((tm, n2), lambda i: (i, 0))),
        scratch_shapes=[pltpu.VMEM((n1 // LANES, tm, LANES), F32),
                        pltpu.VMEM((n1 // LANES, 4, tm // 4, LANES), F32)],
        compiler_params=_params("parallel"),
        name="inproj",
    )(x, g, w_qkv, w_rest)


def _dilated_kernel(q_ref, kp_ref, kc_ref, kn_ref, vp_ref, vc_ref, vn_ref, bias_ref, o_ref, lse_ref,
                    k_scr, v_scr, *, tq, n_cls, dil, rg):
    i = pl.program_id(1)
    scale = HEAD ** -0.5
    sub = 2 * BAND_HALF
    win = sub + 2 * BAND_HALF
    lane = lax.broadcasted_iota(jnp.int32, (sub, LANES), 1)
    scale2 = scale * LOG2E
    ones = jnp.ones((win, HEAD), BF16)
    for rr in range(rg):
        r = pl.program_id(2) * rg + rr
        k_scr[0:BAND_HALF] = kp_ref[rr]
        k_scr[BAND_HALF:BAND_HALF + tq] = kc_ref[rr]
        k_scr[BAND_HALF + tq:] = kn_ref[rr]
        v_scr[0:BAND_HALF] = vp_ref[rr]
        v_scr[BAND_HALF:BAND_HALF + tq] = vc_ref[rr]
        v_scr[BAND_HALF + tq:] = vn_ref[rr]
        for a in range(tq // sub):
            rows = pl.ds(a * sub, sub) if dil == 1 else pl.ds(a * sub * dil + r, sub, stride=dil)
            edge = a == 0 or a == tq // sub - 1
            kidx = i * tq + (a * sub - BAND_HALF) + lax.broadcasted_iota(jnp.int32, (1, win), 1)
            valid = (kidx >= 0) & (kidx < n_cls)
            lse_all = jnp.zeros((sub, LANES), F32)
            for h in range(A_HEADS):
                cols = slice(h * HEAD, (h + 1) * HEAD)
                q = q_ref[rr, a * sub:(a + 1) * sub, cols]
                k = k_scr[a * sub:a * sub + win, cols]
                v = v_scr[a * sub:a * sub + win, cols]
                s2 = _dot_nt(q, k) * scale2 + bias_ref[h]
                if edge:
                    s2 = jnp.where(valid, s2, NEG_INF)
                m2 = jnp.max(s2, axis=1, keepdims=True)
                p = jnp.exp2(s2 - m2).astype(BF16)
                pv = _dot(p, jnp.concatenate([v, ones], axis=1))
                l = pv[:, HEAD:]
                o_ref[h, rows, :] = pv[:, :HEAD] / l
                lse_all = jnp.where(lane == h, m2 * LN2 + jnp.log(l), lse_all)
            lse_ref[rows, :] = lse_all


def _band_bias(dil):
    slopes = 2.0 ** (-8.0 * jnp.arange(1, A_HEADS + 1, dtype=F32) / A_HEADS)
    sub, win = 2 * BAND_HALF, 4 * BAND_HALF
    rel = jnp.abs(BAND_HALF + jnp.arange(sub)[:, None] - jnp.arange(win)[None, :])
    bias = -slopes[:, None, None] * (dil * rel).astype(F32)[None]
    return jnp.where((rel <= BAND_HALF)[None], bias * LOG2E, NEG_INF)


def _dilated_branch(qkv, dil):
    b, _, n_cls, _ = qkv.shape
    tq = min(512, n_cls, 2048 // dil)
    nq = n_cls // tq
    nh = tq // BAND_HALF
    last_halo = n_cls // BAND_HALF - 1
    rg = min(dil, 512 // tq)

    def cur(which):
        return pl.BlockSpec((None, rg, tq, A_WIDTH), lambda bi, i, r: (bi, r, i, which))

    def prev(which):
        return pl.BlockSpec((None, rg, BAND_HALF, A_WIDTH),
                            lambda bi, i, r: (bi, r, jnp.maximum(i * nh - 1, 0), which))

    def nxt(which):
        return pl.BlockSpec((None, rg, BAND_HALF, A_WIDTH),
                            lambda bi, i, r: (bi, r, jnp.minimum((i + 1) * nh, last_halo), which))

    t = b * n_cls * dil
    return pl.pallas_call(
        functools.partial(_dilated_kernel, tq=tq, n_cls=n_cls, dil=dil, rg=rg),
        out_shape=(jax.ShapeDtypeStruct((A_HEADS, t, HEAD), F32), jax.ShapeDtypeStruct((t, LANES), F32)),
        grid=(b, nq, dil // rg),
        in_specs=[cur(0), prev(1), cur(1), nxt(1), prev(2), cur(2), nxt(2),
                  pl.BlockSpec((A_HEADS, 2 * BAND_HALF, 4 * BAND_HALF), lambda bi, i, r: (0, 0, 0))],
        out_specs=(pl.BlockSpec((A_HEADS, tq * dil, HEAD), lambda bi, i, r: (0, bi * nq + i, 0)),
                   pl.BlockSpec((tq * dil, LANES), lambda bi, i, r: (bi * nq + i, 0))),
        scratch_shapes=[pltpu.VMEM((tq + 2 * BAND_HALF, A_WIDTH), BF16),
                        pltpu.VMEM((tq + 2 * BAND_HALF, A_WIDTH), BF16)],
        compiler_params=_params("parallel", "parallel", "arbitrary"),
        name=f"dilated{dil}",
    )(qkv, qkv, qkv, qkv, qkv, qkv, qkv, _band_bias(dil))


def _rotate(y, tab):
    z = y * tab
    r = z + pltpu.roll(z, QK_ROPE, axis=1)
    lane = lax.broadcasted_iota(jnp.int32, r.shape, 1)
    return jnp.where(lane < QK_ROPE, r, 0.0)


def _mla_proj_kernel(cq_ref, ckv_ref, kr_ref, tab_ref, gq_ref, gkv_ref, wq_ref, wkv_ref,
                     q_ref, kt_ref, v_ref):
    cq = _rms(cq_ref[...], gq_ref[...]).astype(BF16)
    ckv = _rms(ckv_ref[...], gkv_ref[...]).astype(BF16)
    tab = tab_ref[...]
    kr_t = _rotate(kr_ref[...], tab).T.astype(BF16)
    ones = jnp.ones((v_ref.shape[0], HEAD), BF16)
    for h in range(B_HEADS):
        lo, mid, hi = 2 * h * HEAD, (2 * h + 1) * HEAD, (2 * h + 2) * HEAD
        q = _dot(cq, wq_ref[:, lo:hi])
        q_ref[:, lo:mid] = q[:, :HEAD].astype(BF16)
        q_ref[:, mid:hi] = _rotate(q[:, HEAD:], tab).astype(BF16)
        kv = _dot(ckv, wkv_ref[:, lo:hi])
        kt_ref[h, :HEAD, :] = kv[:, :HEAD].T.astype(BF16)
        kt_ref[h, HEAD:, :] = kr_t
        v_ref[:, lo:mid] = kv[:, HEAD:].astype(BF16)
        v_ref[:, mid:hi] = ones


def _mla_proj(rest, tab, gq, gkv, wq, wkv, *, s, tm=512):
    t = rest.shape[0]
    lat = 512
    width = B_HEADS * 2 * HEAD
    pos_blocks = s // tm
    return pl.pallas_call(
        _mla_proj_kernel,
        out_shape=(jax.ShapeDtypeStruct((t, width), BF16),
                   jax.ShapeDtypeStruct((B_HEADS, t // tm, 2 * HEAD, tm), BF16),
                   jax.ShapeDtypeStruct((t, width), BF16)),
        grid=(t // tm,),
        in_specs=[
            pl.BlockSpec((tm, lat), lambda i: (i, 0)),
            pl.BlockSpec((tm, lat), lambda i: (i, 1)),
            pl.BlockSpec((tm, LANES), lambda i: (i, 4 * lat // LANES)),
            pl.BlockSpec((tm, LANES), lambda i: (i % pos_blocks, 0)),
            pl.BlockSpec((1, lat), lambda i: (0, 0)),
            pl.BlockSpec((1, lat), lambda i: (0, 0)),
            pl.BlockSpec((lat, width), lambda i: (0, 0)),
            pl.BlockSpec((lat, width), lambda i: (0, 0)),
        ],
        out_specs=(pl.BlockSpec((tm, width), lambda i: (i, 0)),
                   pl.BlockSpec((B_HEADS, None, 2 * HEAD, tm), lambda i: (0, i, 0, 0)),
                   pl.BlockSpec((tm, width), lambda i: (i, 0))),
        compiler_params=_params("parallel"),
        name="mla_proj",
    )(rest, rest, rest, tab, gq, gkv, wq, wkv)


def _mla_attn_kernel(q_ref, kt_ref, v_ref, o_ref, m_scr, a_scr, acc_scr, s_scr, p_scr, mx_scr, *, tq, tk, nk):
    c = (HEAD + QK_ROPE) ** -0.5 * math.log2(math.e)
    q = q_ref[...]
    nl = tk // LANES

    def chunk(kk):
        return pl.ds(pl.multiple_of(kk * tk, tk), tk)

    def scores(kk, slot):
        s = _dot(q, kt_ref[kk])
        s_scr[slot] = s
        mx = s[:, :LANES]
        for j in range(1, nl):
            mx = jnp.maximum(mx, s[:, j * LANES:(j + 1) * LANES])
        mx_scr[slot] = mx

    def softmax(slot, first):
        s = s_scr[slot]
        m_new = jnp.broadcast_to(jnp.max(mx_scr[slot], axis=1, keepdims=True), (tq, LANES))
        if not first:
            m_old = m_scr[...]
            m_new = jnp.maximum(m_old, m_new)
            a_scr[slot] = jnp.exp2((m_old - m_new) * c)
        for j in range(nl):
            cols = slice(j * LANES, (j + 1) * LANES)
            p_scr[slot, :, cols] = jnp.exp2((s[:, cols] - m_new) * c).astype(BF16)
        m_scr[...] = m_new

    def values(kk, slot, first):
        pv = _dot(p_scr[slot], v_ref[chunk(kk), :])
        if first:
            acc_scr[...] = pv
        else:
            alpha = a_scr[slot]
            acc_scr[...] = jnp.concatenate([alpha, alpha], axis=1) * acc_scr[...] + pv

    scores(0, 0)
    softmax(0, True)
    scores(1, 1)
    softmax(1, False)
    values(0, 0, True)
    scores(2, 0)

    pairs = next(n for n in (3, 2, 1) if (nk - 4) % (2 * n) == 0)

    def body(g, carry):
        for u in range(pairs):
            kk = 2 * (pairs * g + u + 1)
            softmax(0, False)
            values(kk - 1, 1, False)
            scores(kk + 1, 1)
            softmax(1, False)
            values(kk, 0, False)
            scores(kk + 2, 0)
        return carry

    lax.fori_loop(0, (nk - 4) // (2 * pairs), body, 0)
    softmax(0, False)
    values(nk - 3, 1, False)
    scores(nk - 1, 1)
    softmax(1, False)
    values(nk - 2, 0, False)
    values(nk - 1, 1, False)
    o_ref[...] = acc_scr[:, :HEAD] / acc_scr[:, HEAD:]


def _mla_attn(q, kt, v, *, b, s, tq=1024):
    t = b * s
    nq = s // tq
    tk = kt.shape[-1]
    nk = s // tk
    assert nk % 2 == 0 and nk >= 4
    return pl.pallas_call(
        functools.partial(_mla_attn_kernel, tq=tq, tk=tk, nk=nk),
        out_shape=jax.ShapeDtypeStruct((t, B_HEADS * HEAD), F32),
        grid=(b, B_HEADS, nq),
        in_specs=[
            pl.BlockSpec((tq, 2 * HEAD), lambda bi, h, i: (bi * nq + i, h)),
            pl.BlockSpec((None, nk, 2 * HEAD, tk), lambda bi, h, i: (h, bi, 0, 0)),
            pl.BlockSpec((s, 2 * HEAD), lambda bi, h, i: (bi, h)),
        ],
        out_specs=pl.BlockSpec((tq, HEAD), lambda bi, h, i: (bi * nq + i, h)),
        scratch_shapes=[pltpu.VMEM((tq, LANES), F32), pltpu.VMEM((2, tq, LANES), F32),
                        pltpu.VMEM((tq, 2 * HEAD), F32), pltpu.VMEM((2, tq, tk), F32),
                        pltpu.VMEM((2, tq, tk), BF16), pltpu.VMEM((2, tq, LANES), F32)],
        compiler_params=_params("parallel", "parallel", "arbitrary"),
        name="mla_attn",
    )(q, kt, v)


def _softplus(x):
    return jnp.maximum(x, 0.0) + jnp.log1p(jnp.exp(-jnp.abs(x)))


def _scan_tile(a, b, row, reverse):
    for sh in (1, 2, 4):
        if reverse:
            keep = row < SUBLANES - sh
            a_s = pltpu.roll(a, SUBLANES - sh, axis=0)
            b_s = pltpu.roll(b, SUBLANES - sh, axis=0)
        else:
            keep = row >= sh
            a_s = pltpu.roll(a, sh, axis=0)
            b_s = pltpu.roll(b, sh, axis=0)
        b = b + a * jnp.where(keep, b_s, 0.0)
        a = a * jnp.where(keep, a_s, 1.0)
    return a, b


def _rglru_fwd_kernel(uc_ref, up_ref, un_ref, cw_ref, cb_ref, wg_ref, bg_ref, lam_ref,
                      hf_ref, ab_ref, bb_ref, ubuf, af_scr, bf_scr, carry, *, ts):
    i = pl.program_id(1)
    ns = pl.num_programs(1)
    ubuf[0:SUBLANES] = jnp.where(i > 0, up_ref[...], 0.0)
    ubuf[SUBLANES:SUBLANES + ts] = uc_ref[...]
    ubuf[SUBLANES + ts:] = jnp.where(i < ns - 1, un_ref[...], 0.0)
    u = cb_ref[...] + cw_ref[0:1, :] * ubuf[SUBLANES - 2:SUBLANES - 2 + ts]
    u = u + cw_ref[1:2, :] * ubuf[SUBLANES - 1:SUBLANES - 1 + ts]
    u = u + cw_ref[2:3, :] * ubuf[SUBLANES:SUBLANES + ts]
    u = u + cw_ref[3:4, :] * ubuf[SUBLANES + 1:SUBLANES + 1 + ts]
    gates = _dot(u.astype(BF16), wg_ref[...]) + bg_ref[...]
    sp = _softplus(-lam_ref[...])
    for d in range(2):
        r = jax.nn.sigmoid(gates[:, (2 * d) * C_WIDTH:(2 * d + 1) * C_WIDTH])
        ig = jax.nn.sigmoid(gates[:, (2 * d + 1) * C_WIDTH:(2 * d + 2) * C_WIDTH])
        log_a = -RG_C * r * sp[d:d + 1, :]
        a = jnp.exp(log_a)
        bterm = jnp.sqrt(-jnp.tanh(log_a) * (a * a + 1.0)) * (ig * u)
        if d == 0:
            af_scr[...] = a
            bf_scr[...] = bterm
        else:
            ab_ref[...] = a
            bb_ref[...] = bterm

    @pl.when(i == 0)
    def _():
        carry[...] = jnp.zeros_like(carry)

    row = lax.broadcasted_iota(jnp.int32, (SUBLANES, C_WIDTH), 0)

    def body(t, c):
        rows = pl.ds(pl.multiple_of(t * SUBLANES, SUBLANES), SUBLANES)
        a, b = _scan_tile(af_scr[rows, :], bf_scr[rows, :], row, reverse=False)
        h = b + a * carry[...]
        hf_ref[rows, :] = h
        carry[...] = h[SUBLANES - 1:SUBLANES, :]
        return c

    lax.fori_loop(0, ts // SUBLANES, body, 0, unroll=SCAN_UNROLL)


def _rglru_bwd_kernel(a_ref, b_ref, hf_ref, g_ref, y_ref, carry, *, ts):
    @pl.when(pl.program_id(1) == 0)
    def _():
        carry[...] = jnp.zeros_like(carry)

    row = lax.broadcasted_iota(jnp.int32, (SUBLANES, C_WIDTH), 0)
    nt = ts // SUBLANES

    def body(t, c):
        rows = pl.ds(pl.multiple_of((nt - 1 - t) * SUBLANES, SUBLANES), SUBLANES)
        a, b = _scan_tile(a_ref[rows, :], b_ref[rows, :], row, reverse=True)
        h = b + a * carry[...]
        carry[...] = h[0:1, :]
        y_ref[rows, :] = jax.nn.gelu(g_ref[rows, :]) * (hf_ref[rows, :] + h)
        return c

    lax.fori_loop(0, nt, body, 0, unroll=SCAN_UNROLL)


def _rglru(rest, conv_w, conv_b, w_gates, b_gates, lam, *, b, s, ts=512):
    t = b * s
    ns = s // ts
    hb = ts // SUBLANES
    last = s // SUBLANES - 1
    u_col, g_col = 2, 3
    tile = lambda bi, i: (bi * ns + i, 0)
    hf, ab, bb = pl.pallas_call(
        functools.partial(_rglru_fwd_kernel, ts=ts),
        out_shape=tuple(jax.ShapeDtypeStruct((t, C_WIDTH), F32) for _ in range(3)),
        grid=(b, ns),
        in_specs=[
            pl.BlockSpec((ts, C_WIDTH), lambda bi, i: (bi * ns + i, u_col)),
            pl.BlockSpec((SUBLANES, C_WIDTH),
                         lambda bi, i: (bi * ns * hb + jnp.maximum(i * hb - 1, 0), u_col)),
            pl.BlockSpec((SUBLANES, C_WIDTH),
                         lambda bi, i: (bi * ns * hb + jnp.minimum((i + 1) * hb, last), u_col)),
            pl.BlockSpec((4, C_WIDTH), lambda bi, i: (0, 0)),
            pl.BlockSpec((1, C_WIDTH), lambda bi, i: (0, 0)),
            pl.BlockSpec((C_WIDTH, 4 * C_WIDTH), lambda bi, i: (0, 0)),
            pl.BlockSpec((1, 4 * C_WIDTH), lambda bi, i: (0, 0)),
            pl.BlockSpec((2, C_WIDTH), lambda bi, i: (0, 0)),
        ],
        out_specs=tuple(pl.BlockSpec((ts, C_WIDTH), tile) for _ in range(3)),
        scratch_shapes=[pltpu.VMEM((ts + 2 * SUBLANES, C_WIDTH), F32),
                        pltpu.VMEM((ts, C_WIDTH), F32), pltpu.VMEM((ts, C_WIDTH), F32),
                        pltpu.VMEM((1, C_WIDTH), F32)],
        compiler_params=_params("parallel", "arbitrary"),
        name="rglru_fwd",
    )(rest, rest, rest, conv_w, conv_b, w_gates, b_gates, lam)
    rev = lambda bi, i: (bi * ns + ns - 1 - i, 0)
    return pl.pallas_call(
        functools.partial(_rglru_bwd_kernel, ts=ts),
        out_shape=jax.ShapeDtypeStruct((t, C_WIDTH), F32),
        grid=(b, ns),
        in_specs=[
            pl.BlockSpec((ts, C_WIDTH), rev),
            pl.BlockSpec((ts, C_WIDTH), rev),
            pl.BlockSpec((ts, C_WIDTH), rev),
            pl.BlockSpec((ts, C_WIDTH), lambda bi, i: (bi * ns + ns - 1 - i, g_col)),
        ],
        out_specs=pl.BlockSpec((ts, C_WIDTH), rev),
        scratch_shapes=[pltpu.VMEM((1, C_WIDTH), F32)],
        compiler_params=_params("parallel", "arbitrary"),
        name="rglru_bwd",
    )(ab, bb, hf, rest)


def _outproj_kernel(x_ref, o1_ref, o2_ref, o3_ref, l1_ref, l2_ref, l3_ref, yb_ref, yc_ref,
                    ga_ref, gb_ref, gc_ref, w_ref, out_ref):
    l1, l2, l3 = l1_ref[...], l2_ref[...], l3_ref[...]
    m = jnp.maximum(jnp.maximum(l1, l2), l3)
    e1, e2, e3 = jnp.exp(l1 - m), jnp.exp(l2 - m), jnp.exp(l3 - m)
    z = e1 + e2 + e3
    w1, w2, w3 = e1 / z, e2 / z, e3 / z
    parts = []
    for h in range(A_HEADS):
        parts.append(w1[:, h:h + 1] * o1_ref[h] + w2[:, h:h + 1] * o2_ref[h] + w3[:, h:h + 1] * o3_ref[h])
    ya = jnp.concatenate(parts, axis=1)
    b0, c0 = A_WIDTH, 2 * A_WIDTH
    y = _dot(_rms(yb_ref[...], gb_ref[...]).astype(BF16), w_ref[b0:c0, :])
    y = y + _dot(_rms(yc_ref[...], gc_ref[...]).astype(BF16), w_ref[c0:, :])
    y = y + _dot(_rms(ya, ga_ref[...]).astype(BF16), w_ref[0:b0, :])
    out_ref[...] = x_ref[...] + y


def _outproj(x, o_branches, lse_branches, yb, yc, ga, gb, gc, w, *, tm=512):
    t, d = x.shape
    row = lambda n: pl.BlockSpec((tm, n), lambda i: (i, 0))
    full = lambda r, n: pl.BlockSpec((r, n), lambda i: (0, 0))
    heads = pl.BlockSpec((A_HEADS, tm, HEAD), lambda i: (0, i, 0))
    return pl.pallas_call(
        _outproj_kernel,
        out_shape=jax.ShapeDtypeStruct((t, d), F32),
        grid=(t // tm,),
        in_specs=[row(d), heads, heads, heads, row(LANES), row(LANES), row(LANES),
                  row(A_WIDTH), row(C_WIDTH), full(1, A_WIDTH), full(1, A_WIDTH), full(1, C_WIDTH),
                  full(d, d)],
        out_specs=row(d),
        compiler_params=_params("parallel"),
        name="outproj",
    )(x, *o_branches, *lse_branches, yb, yc, ga, gb, gc, w)


def _mem_kv_kernel(m_ref, g_ref, w_ref, o_ref):
    o_ref[...] = _dot(_rms(m_ref[...], g_ref[...]).astype(BF16), w_ref[...]).astype(BF16)


def _mem_kv(mem, g, w_kv):
    t, d = mem.shape
    n = w_kv.shape[1]
    tm = 256
    return pl.pallas_call(
        _mem_kv_kernel,
        out_shape=jax.ShapeDtypeStruct((t, n), BF16),
        grid=(t // tm,),
        in_specs=[pl.BlockSpec((tm, d), lambda i: (i, 0)), pl.BlockSpec((1, d), lambda i: (0, 0)),
                  pl.BlockSpec((d, n), lambda i: (0, 0))],
        out_specs=pl.BlockSpec((tm, n), lambda i: (i, 0)),
        compiler_params=_params("parallel"),
        name="mem_kv",
    )(mem, g, w_kv)


def _xattn_kernel(x_ref, g_ref, wq_ref, kv_ref, wo_ref, o_ref):
    x = x_ref[...]
    q = _dot(_rms(x, g_ref[...]).astype(BF16), wq_ref[...]).astype(BF16)
    c = HEAD ** -0.5 * LOG2E
    ones = jnp.ones((kv_ref.shape[0], HEAD), BF16)
    outs = []
    for h in range(X_HEADS):
        k = kv_ref[:, h * HEAD:(h + 1) * HEAD]
        v = kv_ref[:, X_WIDTH + h * HEAD:X_WIDTH + (h + 1) * HEAD]
        s = _dot_nt(q[:, h * HEAD:(h + 1) * HEAD], k)
        p = jnp.exp2((s - jnp.max(s, axis=1, keepdims=True)) * c).astype(BF16)
        pv = _dot(p, jnp.concatenate([v, ones], axis=1))
        outs.append(pv[:, :HEAD] / pv[:, HEAD:])
    o = jnp.concatenate(outs, axis=1).astype(BF16)
    o_ref[...] = x + _dot(o, wo_ref[...])


def _xattn(x, g, wq, kv, wo, *, s, n_mem, tm=512):
    t, d = x.shape
    per_seq = s // tm
    return pl.pallas_call(
        _xattn_kernel,
        out_shape=jax.ShapeDtypeStruct((t, d), F32),
        grid=(t // tm,),
        in_specs=[
            pl.BlockSpec((tm, d), lambda i: (i, 0)),
            pl.BlockSpec((1, d), lambda i: (0, 0)),
            pl.BlockSpec((d, X_WIDTH), lambda i: (0, 0)),
            pl.BlockSpec((n_mem, 2 * X_WIDTH), lambda i: (i // per_seq, 0)),
            pl.BlockSpec((X_WIDTH, d), lambda i: (0, 0)),
        ],
        out_specs=pl.BlockSpec((tm, d), lambda i: (i, 0)),
        compiler_params=_params("parallel"),
        name="xattn",
    )(x, g, wq, kv, wo)


def _cast_kernel(w_ref, o_ref):
    o_ref[...] = w_ref[...].astype(BF16)


def _layer_to_bf16(w, l):
    _, rows, cols = w.shape
    block_rows = next(c for c in (512, 256, 128, 64, 32, 16) if rows % c == 0 and c * cols * 4 <= (4 << 20))
    return pl.pallas_call(
        _cast_kernel,
        out_shape=jax.ShapeDtypeStruct((rows, cols), BF16),
        grid=(rows // block_rows,),
        in_specs=[pl.BlockSpec((None, block_rows, cols), lambda i: (l, i, 0))],
        out_specs=pl.BlockSpec((block_rows, cols), lambda i: (i, 0)),
        compiler_params=_params("parallel"),
        name="cast_bf16",
    )(w)


def _rope_table(s):
    inv = ROPE_THETA ** (-jnp.arange(0, QK_ROPE, 2, dtype=F32) / QK_ROPE)
    ang = jnp.arange(s, dtype=F32)[:, None] * inv[None, :]
    cos, sin = jnp.cos(ang), jnp.sin(ang)
    return jnp.concatenate([cos, cos, -sin, sin], axis=1)


def _swap_halves(w):
    half = w.shape[-1] // 2
    return jnp.concatenate([w[..., half:], w[..., :half]], axis=-1)


def _prep_layer(p, l):
    row = lambda v: v.reshape(1, -1)
    w_in = p['w_in'][l]
    qkv_end, cq_end, ckv_end, kr_end, u_end = 2304, 2816, 3328, 3392, 3904
    w_rope = w_in[:, ckv_end:kr_end]
    w_rest = jnp.concatenate([w_in[:, qkv_end:ckv_end], w_in[:, kr_end:], w_rope, _swap_halves(w_rope)], axis=1)
    wq = p['w_q_up'][l].reshape(-1, B_HEADS, HEAD + QK_ROPE)
    wq = jnp.concatenate([wq, _swap_halves(wq[..., HEAD:])], axis=-1).reshape(-1, B_HEADS * 2 * HEAD)
    eye = jnp.eye(C_BLOCKS, dtype=F32)
    dense = lambda w: jnp.einsum('ncd,nm->ncmd', w, eye).reshape(C_WIDTH, C_WIDTH)
    w_r, w_i = p['w_rg_r'][l], p['w_rg_i'][l]
    w_gates = jnp.concatenate([dense(w_r[0]), dense(w_i[0]), dense(w_r[1]), dense(w_i[1])], axis=1)
    b_r, b_i = p['b_rg_r'][l], p['b_rg_i'][l]
    b_gates = jnp.concatenate([b_r[0], b_i[0], b_r[1], b_i[1]]).reshape(1, -1)
    bf = lambda w: w.astype(BF16)
    big = lambda name: _layer_to_bf16(p[name], l)
    return dict(
        g_ffn1=row(p['g_ffn1'][l]), w1_gate=big('w1_gate'), w1_up=big('w1_up'), w1_down=big('w1_down'),
        g_mix=row(p['g_mix'][l]), w_qkv=bf(w_in[:, :qkv_end]), w_rest=bf(w_rest),
        g_q_lat=row(p['g_q_lat'][l]), g_kv_lat=row(p['g_kv_lat'][l]), w_q=bf(wq), w_kv=bf(p['w_kv_up'][l]),
        conv_w=p['conv_w'][l], conv_b=row(p['conv_b'][l]), w_gates=bf(w_gates), b_gates=b_gates,
        lam=p['rg_lambda'][l],
        g_out_a=row(p['g_out_a'][l]), g_out_b=row(p['g_out_b'][l]), g_out_c=row(p['g_out_c'][l]),
        w_out=big('w_out'),
        g_xattn=row(p['g_xattn'][l]), g_mem=row(p['g_mem'][l]), w_xq=bf(p['w_xq'][l]),
        w_xkv=bf(jnp.concatenate([p['w_xk'][l], p['w_xv'][l]], axis=1)), w_xo=bf(p['w_xo'][l]),
        g_ffn2=row(p['g_ffn2'][l]), w2_gate=big('w2_gate'), w2_up=big('w2_up'), w2_down=big('w2_down'),
    )


def _trunk(x, mem, layers, g_final):
    b, s, d = x.shape
    n_mem = mem.shape[1]
    x = x.reshape(b * s, d)
    mem = mem.reshape(b * n_mem, d)
    tab = _rope_table(s)
    for l, w in enumerate(layers):
        x = _ffn(x, w['g_ffn1'], w['w1_gate'], w['w1_up'], w['w1_down'], g_final, final_norm=False)
        *qkv_by_dil, rest = _inproj(x, w['g_mix'], w['w_qkv'], w['w_rest'], b=b, s=s)
        branches = [_dilated_branch(qkv, dil) for qkv, (_, dil) in zip(qkv_by_dil, DILATED_CONFIGS)]
        q, kt, v = _mla_proj(rest, tab, w['g_q_lat'], w['g_kv_lat'], w['w_q'], w['w_kv'], s=s)
        yb = _mla_attn(q, kt, v, b=b, s=s)
        yc = _rglru(rest, w['conv_w'], w['conv_b'], w['w_gates'], w['b_gates'], w['lam'], b=b, s=s)
        x = _outproj(x, [o for o, _ in branches], [lse for _, lse in branches], yb, yc,
                     w['g_out_a'], w['g_out_b'], w['g_out_c'], w['w_out'])
        kv = _mem_kv(mem, w['g_mem'], w['w_xkv'])
        x = _xattn(x, w['g_xattn'], w['w_xq'], kv, w['w_xo'], s=s, n_mem=n_mem)
        x = _ffn(x, w['g_ffn2'], w['w2_gate'], w['w2_up'], w['w2_down'], g_final,
                 final_norm=(l == len(layers) - 1))
    return x.reshape(b, s, d)


def kernel(x_prompt, x_sample, mem_prompt, mem_sample, g_ffn1, w1_gate, w1_up, w1_down, g_mix, w_in, g_q_lat, w_q_up, g_kv_lat, w_kv_up, conv_w, conv_b, w_rg_r, b_rg_r, w_rg_i, b_rg_i, rg_lambda, g_out_a, g_out_b, g_out_c, w_out, g_xattn, g_mem, w_xq, w_xk, w_xv, w_xo, g_ffn2, w2_gate, w2_up, w2_down, g_final):
    p = dict(g_ffn1=g_ffn1, w1_gate=w1_gate, w1_up=w1_up, w1_down=w1_down,
             g_mix=g_mix, w_in=w_in, g_q_lat=g_q_lat, w_q_up=w_q_up, g_kv_lat=g_kv_lat, w_kv_up=w_kv_up,
             conv_w=conv_w, conv_b=conv_b, w_rg_r=w_rg_r, b_rg_r=b_rg_r, w_rg_i=w_rg_i, b_rg_i=b_rg_i,
             rg_lambda=rg_lambda, g_out_a=g_out_a, g_out_b=g_out_b, g_out_c=g_out_c, w_out=w_out,
             g_xattn=g_xattn, g_mem=g_mem, w_xq=w_xq, w_xk=w_xk, w_xv=w_xv, w_xo=w_xo,
             g_ffn2=g_ffn2, w2_gate=w2_gate, w2_up=w2_up, w2_down=w2_down)
    layers = [_prep_layer(p, l) for l in range(g_ffn1.shape[0])]
    gf = g_final.reshape(1, -1)
    return (_trunk(x_prompt, mem_prompt, layers, gf), _trunk(x_sample, mem_sample, layers, gf))
```

```python
import functools
import math

import jax
import jax.numpy as jnp
from jax import lax
from jax.experimental import pallas as pl
from jax.experimental.pallas import tpu as pltpu

BF16 = jnp.bfloat16
F32 = jnp.float32

D_MODEL = 2048
A_HEADS = 6
HEAD = 128
A_WIDTH = A_HEADS * HEAD
DILATED_CONFIGS = ((128, 1), (512, 4), (2048, 16))
B_HEADS = 6
QK_ROPE = 64
ROPE_THETA = 10000.0
C_WIDTH = 512
C_BLOCKS = 8
C_BLOCK_W = 64
RG_C = 8.0
X_HEADS = 4
X_WIDTH = 512
D_FF = 5632
EPS = 1e-6
NEG_INF = -1e30
LOG2E = math.log2(math.e)
LN2 = math.log(2.0)

V7X_VMEM_LIMIT_BYTES = 56 * 1024 * 1024
LANES = 128
SUBLANES = 8

REST_WIDTH = 4 * 512 + LANES
BAND_HALF = 64
SCAN_UNROLL = 4


def _params(*sem):
    return pltpu.CompilerParams(dimension_semantics=sem, vmem_limit_bytes=V7X_VMEM_LIMIT_BYTES)


def _rms(x, g):
    return x * lax.rsqrt(jnp.mean(x * x, axis=-1, keepdims=True) + EPS) * g


def _dot(a, b):
    return jnp.dot(a, b, preferred_element_type=F32)


def _dot_nt(a, b):
    return lax.dot_general(a, b, (((1,), (1,)), ((), ())), preferred_element_type=F32)


def _ffn_kernel(x_ref, g_ref, wg_ref, wu_ref, wd_ref, gf_ref, o_ref, h_scr, *, final_norm):
    j = pl.program_id(1)

    def partial_out(h):
        a = _dot(h, wg_ref[...])
        u = _dot(h, wu_ref[...])
        return _dot((jax.nn.silu(a) * u).astype(BF16), wd_ref[...])

    last = pl.num_programs(1) - 1

    @pl.when(j == 0)
    def _():
        h = _rms(x_ref[...], g_ref[...]).astype(BF16)
        h_scr[...] = h
        o_ref[...] = partial_out(h)

    @pl.when((j > 0) & (j < last))
    def _():
        o_ref[...] += partial_out(h_scr[...])

    @pl.when(j == last)
    def _():
        o_ref[...] = x_ref[...] + 0.5 * (o_ref[...] + partial_out(h_scr[...]))

    if final_norm:
        @pl.when(j == last)
        def _():
            rows_per_pass = 256

            def finish(r, carry):
                rows = pl.ds(pl.multiple_of(r * rows_per_pass, rows_per_pass), rows_per_pass)
                o_ref[rows, :] = _rms(o_ref[rows, :], gf_ref[...])
                return carry

            lax.fori_loop(0, o_ref.shape[0] // rows_per_pass, finish, 0)


def _ffn(x, g, wg, wu, wd, gf, *, final_norm, tm=1024, tf=512):
    t, d = x.shape
    f = wg.shape[1]
    return pl.pallas_call(
        functools.partial(_ffn_kernel, final_norm=final_norm),
        out_shape=jax.ShapeDtypeStruct((t, d), F32),
        grid=(t // tm, f // tf),
        in_specs=[
            pl.BlockSpec((tm, d), lambda i, j: (i, 0)),
            pl.BlockSpec((1, d), lambda i, j: (0, 0)),
            pl.BlockSpec((d, tf), lambda i, j: (0, j)),
            pl.BlockSpec((d, tf), lambda i, j: (0, j)),
            pl.BlockSpec((tf, d), lambda i, j: (j, 0)),
            pl.BlockSpec((1, d), lambda i, j: (0, 0)),
        ],
        out_specs=pl.BlockSpec((tm, d), lambda i, j: (i, 0)),
        scratch_shapes=[pltpu.VMEM((tm, d), BF16)],
        compiler_params=_params("parallel", "arbitrary"),
        name="ffn",
    )(x, g, wg, wu, wd, gf)


def _inproj_kernel(x_ref, g_ref, w1_ref, w2_ref, o1_ref, o4_ref, o16_ref, o2_ref, y_scr, c4_scr, *, tm):
    h = _rms(x_ref[...], g_ref[...]).astype(BF16)
    y = _dot(h, w1_ref[...])
    o1_ref[0] = y.astype(BF16)
    n4 = tm // 4
    for cb in range(y.shape[1] // LANES):
        cols = slice(cb * LANES, (cb + 1) * LANES)
        y_scr[cb] = y[:, cols]
        for r4 in range(4):
            c4 = y_scr[cb, pl.ds(r4, n4, stride=4), :]
            o4_ref[r4, :, cols] = c4.astype(BF16)
            c4_scr[cb, r4] = c4
            for j in range(4):
                o16_ref[r4 + 4 * j, :, cols] = c4_scr[cb, r4, pl.ds(j, n4 // 4, stride=4), :].astype(BF16)
    o2_ref[...] = _dot(h, w2_ref[...])


def _inproj(x, g, w_qkv, w_rest, *, b, s, tm=256):
    t, d = x.shape
    n1, n2 = w_qkv.shape[1], w_rest.shape[1]
    per_seq = s // tm
    dils = [dil for _, dil in DILATED_CONFIGS]
    cls_shape = lambda dil: jax.ShapeDtypeStruct((b, dil, s // dil, n1), BF16)
    cls_spec = lambda dil: pl.BlockSpec((None, dil, tm // dil, n1), lambda i: (i // per_seq, 0, i % per_seq, 0))
    resident = dict(pipeline_mode=pl.Buffered(1))
    return pl.pallas_call(
        functools.partial(_inproj_kernel, tm=tm),
        out_shape=(*[cls_shape(dil) for dil in dils], jax.ShapeDtypeStruct((t, n2), F32)),
        grid=(t // tm,),
        in_specs=[
            pl.BlockSpec((tm, d), lambda i: (i, 0)),
            pl.BlockSpec((1, d), lambda i: (0, 0)),
            pl.BlockSpec((d, n1), lambda i: (0, 0), **resident),
            pl.BlockSpec((d, n2), lambda i: (0, 0), **resident),
        ],
        out_specs=(*[cls_spec(dil) for dil in dils], pl.BlockSpec((tm, n2), lambda i: (i, 0))),
        scratch_shapes=[pltpu.VMEM((n1 // LANES, tm, LANES), F32),
                        pltpu.VMEM((n1 // LANES, 4, tm // 4, LANES), F32)],
        compiler_params=_params("parallel"),
        name="inproj",
    )(x, g, w_qkv, w_rest)


def _dilated_kernel(q_ref, kp_ref, kc_ref, kn_ref, vp_ref, vc_ref, vn_ref, bias_ref, o_ref, lse_ref,
                    k_scr, v_scr, *, tq, n_cls, dil, rg):
    i = pl.program_id(1)
    scale = HEAD ** -0.5
    sub = 2 * BAND_HALF
    win = sub + 2 * BAND_HALF
    lane = lax.broadcasted_iota(jnp.int32, (sub, LANES), 1)
    scale2 = scale * LOG2E
    ones = jnp.ones((win, HEAD), BF16)
    for rr in range(rg):
        r = pl.program_id(2) * rg + rr
        k_scr[0:BAND_HALF] = kp_ref[rr]
        k_scr[BAND_HALF:BAND_HALF + tq] = kc_ref[rr]
        k_scr[BAND_HALF + tq:] = kn_ref[rr]
        v_scr[0:BAND_HALF] = vp_ref[rr]
        v_scr[BAND_HALF:BAND_HALF + tq] = vc_ref[rr]
        v_scr[BAND_HALF + tq:] = vn_ref[rr]
        for a in range(tq // sub):
            rows = pl.ds(a * sub, sub) if dil == 1 else pl.ds(a * sub * dil + r, sub, stride=dil)
            edge = a == 0 or a == tq // sub - 1
            kidx = i * tq + (a * sub - BAND_HALF) + lax.broadcasted_iota(jnp.int32, (1, win), 1)
            valid = (kidx >= 0) & (kidx < n_cls)
            lse_all = jnp.zeros((sub, LANES), F32)
            for h in range(A_HEADS):
                cols = slice(h * HEAD, (h + 1) * HEAD)
                q = q_ref[rr, a * sub:(a + 1) * sub, cols]
                k = k_scr[a * sub:a * sub + win, cols]
                v = v_scr[a * sub:a * sub + win, cols]
                s2 = _dot_nt(q, k) * scale2 + bias_ref[h]
                if edge:
                    s2 = jnp.where(valid, s2, NEG_INF)
                m2 = jnp.max(s2, axis=1, keepdims=True)
                p = jnp.exp2(s2 - m2).astype(BF16)
                pv = _dot(p, jnp.concatenate([v, ones], axis=1))
                l = pv[:, HEAD:]
                o_ref[h, rows, :] = pv[:, :HEAD] / l
                lse_all = jnp.where(lane == h, m2 * LN2 + jnp.log(l), lse_all)
            lse_ref[rows, :] = lse_all


def _band_bias(dil):
    slopes = 2.0 ** (-8.0 * jnp.arange(1, A_HEADS + 1, dtype=F32) / A_HEADS)
    sub, win = 2 * BAND_HALF, 4 * BAND_HALF
    rel = jnp.abs(BAND_HALF + jnp.arange(sub)[:, None] - jnp.arange(win)[None, :])
    bias = -slopes[:, None, None] * (dil * rel).astype(F32)[None]
    return jnp.where((rel <= BAND_HALF)[None], bias * LOG2E, NEG_INF)


def _dilated_branch(qkv, dil):
    b, _, n_cls, _ = qkv.shape
    tq = min(512, n_cls, 2048 // dil)
    nq = n_cls // tq
    nh = tq // BAND_HALF
    last_halo = n_cls // BAND_HALF - 1
    rg = min(dil, 512 // tq)

    def cur(which):
        return pl.BlockSpec((None, rg, tq, A_WIDTH), lambda bi, i, r: (bi, r, i, which))

    def prev(which):
        return pl.BlockSpec((None, rg, BAND_HALF, A_WIDTH),
                            lambda bi, i, r: (bi, r, jnp.maximum(i * nh - 1, 0), which))

    def nxt(which):
        return pl.BlockSpec((None, rg, BAND_HALF, A_WIDTH),
                            lambda bi, i, r: (bi, r, jnp.minimum((i + 1) * nh, last_halo), which))

    t = b * n_cls * dil
    return pl.pallas_call(
        functools.partial(_dilated_kernel, tq=tq, n_cls=n_cls, dil=dil, rg=rg),
        out_shape=(jax.ShapeDtypeStruct((A_HEADS, t, HEAD), F32), jax.ShapeDtypeStruct((t, LANES), F32)),
        grid=(b, nq, dil // rg),
        in_specs=[cur(0), prev(1), cur(1), nxt(1), prev(2), cur(2), nxt(2),
                  pl.BlockSpec((A_HEADS, 2 * BAND_HALF, 4 * BAND_HALF), lambda bi, i, r: (0, 0, 0))],
        out_specs=(pl.BlockSpec((A_HEADS, tq * dil, HEAD), lambda bi, i, r: (0, bi * nq + i, 0)),
                   pl.BlockSpec((tq * dil, LANES), lambda bi, i, r: (bi * nq + i, 0))),
        scratch_shapes=[pltpu.VMEM((tq + 2 * BAND_HALF, A_WIDTH), BF16),
                        pltpu.VMEM((tq + 2 * BAND_HALF, A_WIDTH), BF16)],
        compiler_params=_params("parallel", "parallel", "arbitrary"),
        name=f"dilated{dil}",
    )(qkv, qkv, qkv, qkv, qkv, qkv, qkv, _band_bias(dil))


def _rotate(y, tab):
    z = y * tab
    r = z + pltpu.roll(z, QK_ROPE, axis=1)
    lane = lax.broadcasted_iota(jnp.int32, r.shape, 1)
    return jnp.where(lane < QK_ROPE, r, 0.0)


def _mla_proj_kernel(cq_ref, ckv_ref, kr_ref, tab_ref, gq_ref, gkv_ref, wq_ref, wkv_ref,
                     q_ref, kt_ref, v_ref):
    cq = _rms(cq_ref[...], gq_ref[...]).astype(BF16)
    ckv = _rms(ckv_ref[...], gkv_ref[...]).astype(BF16)
    tab = tab_ref[...]
    kr_t = _rotate(kr_ref[...], tab).T.astype(BF16)
    for h in range(B_HEADS):
        lo, mid, hi = 2 * h * HEAD, (2 * h + 1) * HEAD, (2 * h + 2) * HEAD
        q = _dot(cq, wq_ref[:, lo:hi])
        q_ref[:, lo:mid] = q[:, :HEAD].astype(BF16)
        q_ref[:, mid:hi] = _rotate(q[:, HEAD:], tab).astype(BF16)
        kv = _dot(ckv, wkv_ref[:, lo:hi])
        kt_ref[h, :HEAD, :] = kv[:, :HEAD].T.astype(BF16)
        kt_ref[h, HEAD:, :] = kr_t
        v_ref[:, h * HEAD:(h + 1) * HEAD] = kv[:, HEAD:].astype(BF16)


def _mla_proj(rest, tab, gq, gkv, wq, wkv, *, s, tm=512):
    t = rest.shape[0]
    lat = 512
    width = B_HEADS * 2 * HEAD
    pos_blocks = s // tm
    return pl.pallas_call(
        _mla_proj_kernel,
        out_shape=(jax.ShapeDtypeStruct((t, width), BF16),
                   jax.ShapeDtypeStruct((B_HEADS, t // tm, 2 * HEAD, tm), BF16),
                   jax.ShapeDtypeStruct((t, B_HEADS * HEAD), BF16)),
        grid=(t // tm,),
        in_specs=[
            pl.BlockSpec((tm, lat), lambda i: (i, 0)),
            pl.BlockSpec((tm, lat), lambda i: (i, 1)),
            pl.BlockSpec((tm, LANES), lambda i: (i, 4 * lat // LANES)),
            pl.BlockSpec((tm, LANES), lambda i: (i % pos_blocks, 0)),
            pl.BlockSpec((1, lat), lambda i: (0, 0)),
            pl.BlockSpec((1, lat), lambda i: (0, 0)),
            pl.BlockSpec((lat, width), lambda i: (0, 0)),
            pl.BlockSpec((lat, width), lambda i: (0, 0)),
        ],
        out_specs=(pl.BlockSpec((tm, width), lambda i: (i, 0)),
                   pl.BlockSpec((B_HEADS, None, 2 * HEAD, tm), lambda i: (0, i, 0, 0)),
                   pl.BlockSpec((tm, B_HEADS * HEAD), lambda i: (i, 0))),
        compiler_params=_params("parallel"),
        name="mla_proj",
    )(rest, rest, rest, tab, gq, gkv, wq, wkv)


def _mla_attn_kernel(q_ref, kt_ref, v_ref, o_ref, m_scr, a_scr, acc_scr, s_scr, p_scr, mx_scr, *, tq, tk, nk):
    c = (HEAD + QK_ROPE) ** -0.5 * math.log2(math.e)
    q = q_ref[...]
    nl = tk // LANES
    ones = jnp.ones((tk, HEAD), BF16)

    def chunk(kk):
        return pl.ds(pl.multiple_of(kk * tk, tk), tk)

    def scores(kk, slot):
        s = _dot(q, kt_ref[kk])
        s_scr[slot] = s
        mx = s[:, :LANES]
        for j in range(1, nl):
            mx = jnp.maximum(mx, s[:, j * LANES:(j + 1) * LANES])
        mx_scr[slot] = mx

    def softmax(slot, first):
        s = s_scr[slot]
        m_new = jnp.broadcast_to(jnp.max(mx_scr[slot], axis=1, keepdims=True), (tq, LANES))
        if not first:
            m_old = m_scr[...]
            m_new = jnp.maximum(m_old, m_new)
            a_scr[slot] = jnp.exp2((m_old - m_new) * c)
        for j in range(nl):
            cols = slice(j * LANES, (j + 1) * LANES)
            p_scr[slot, :, cols] = jnp.exp2((s[:, cols] - m_new) * c).astype(BF16)
        m_scr[...] = m_new

    def values(kk, slot, first):
        pv = _dot(p_scr[slot], jnp.concatenate([v_ref[chunk(kk), :], ones], axis=1))
        if first:
            acc_scr[...] = pv
        else:
            alpha = a_scr[slot]
            acc_scr[...] = jnp.concatenate([alpha, alpha], axis=1) * acc_scr[...] + pv

    scores(0, 0)
    softmax(0, True)
    scores(1, 1)
    softmax(1, False)
    values(0, 0, True)
    scores(2, 0)

    pairs = next(n for n in (3, 2, 1) if (nk - 4) % (2 * n) == 0)

    def body(g, carry):
        for u in range(pairs):
            kk = 2 * (pairs * g + u + 1)
            softmax(0, False)
            values(kk - 1, 1, False)
            scores(kk + 1, 1)
            softmax(1, False)
            values(kk, 0, False)
            scores(kk + 2, 0)
        return carry

    lax.fori_loop(0, (nk - 4) // (2 * pairs), body, 0)
    softmax(0, False)
    values(nk - 3, 1, False)
    scores(nk - 1, 1)
    softmax(1, False)
    values(nk - 2, 0, False)
    values(nk - 1, 1, False)
    o_ref[...] = acc_scr[:, :HEAD] / acc_scr[:, HEAD:]


def _mla_attn(q, kt, v, *, b, s, tq=1024):
    t = b * s
    nq = s // tq
    tk = kt.shape[-1]
    nk = s // tk
    assert nk % 2 == 0 and nk >= 4
    return pl.pallas_call(
        functools.partial(_mla_attn_kernel, tq=tq, tk=tk, nk=nk),
        out_shape=jax.ShapeDtypeStruct((t, B_HEADS * HEAD), F32),
        grid=(b, B_HEADS, nq),
        in_specs=[
            pl.BlockSpec((tq, 2 * HEAD), lambda bi, h, i: (bi * nq + i, h)),
            pl.BlockSpec((None, nk, 2 * HEAD, tk), lambda bi, h, i: (h, bi, 0, 0)),
            pl.BlockSpec((s, HEAD), lambda bi, h, i: (bi, h)),
        ],
        out_specs=pl.BlockSpec((tq, HEAD), lambda bi, h, i: (bi * nq + i, h)),
        scratch_shapes=[pltpu.VMEM((tq, LANES), F32), pltpu.VMEM((2, tq, LANES), F32),
                        pltpu.VMEM((tq, 2 * HEAD), F32), pltpu.VMEM((2, tq, tk), F32),
                        pltpu.VMEM((2, tq, tk), BF16), pltpu.VMEM((2, tq, LANES), F32)],
        compiler_params=_params("parallel", "parallel", "arbitrary"),
        name="mla_attn",
    )(q, kt, v)


def _softplus(x):
    return jnp.maximum(x, 0.0) + jnp.log1p(jnp.exp(-jnp.abs(x)))


def _scan_tile(a, b, row, reverse):
    for sh in (1, 2, 4):
        if reverse:
            keep = row < SUBLANES - sh
            a_s = pltpu.roll(a, SUBLANES - sh, axis=0)
            b_s = pltpu.roll(b, SUBLANES - sh, axis=0)
        else:
            keep = row >= sh
            a_s = pltpu.roll(a, sh, axis=0)
            b_s = pltpu.roll(b, sh, axis=0)
        b = b + a * jnp.where(keep, b_s, 0.0)
        a = a * jnp.where(keep, a_s, 1.0)
    return a, b


def _rglru_fwd_kernel(uc_ref, up_ref, un_ref, cw_ref, cb_ref, wg_ref, bg_ref, lam_ref,
                      hf_ref, ab_ref, bb_ref, ubuf, af_scr, bf_scr, carry, *, ts):
    i = pl.program_id(1)
    ns = pl.num_programs(1)
    ubuf[0:SUBLANES] = jnp.where(i > 0, up_ref[...], 0.0)
    ubuf[SUBLANES:SUBLANES + ts] = uc_ref[...]
    ubuf[SUBLANES + ts:] = jnp.where(i < ns - 1, un_ref[...], 0.0)
    u = cb_ref[...] + cw_ref[0:1, :] * ubuf[SUBLANES - 2:SUBLANES - 2 + ts]
    u = u + cw_ref[1:2, :] * ubuf[SUBLANES - 1:SUBLANES - 1 + ts]
    u = u + cw_ref[2:3, :] * ubuf[SUBLANES:SUBLANES + ts]
    u = u + cw_ref[3:4, :] * ubuf[SUBLANES + 1:SUBLANES + 1 + ts]
    gates = _dot(u.astype(BF16), wg_ref[...]) + bg_ref[...]
    sp = _softplus(-lam_ref[...])
    for d in range(2):
        r = jax.nn.sigmoid(gates[:, (2 * d) * C_WIDTH:(2 * d + 1) * C_WIDTH])
        ig = jax.nn.sigmoid(gates[:, (2 * d + 1) * C_WIDTH:(2 * d + 2) * C_WIDTH])
        log_a = -RG_C * r * sp[d:d + 1, :]
        a = jnp.exp(log_a)
        bterm = jnp.sqrt(-jnp.tanh(log_a) * (a * a + 1.0)) * (ig * u)
        if d == 0:
            af_scr[...] = a
            bf_scr[...] = bterm
        else:
            ab_ref[...] = a
            bb_ref[...] = bterm

    @pl.when(i == 0)
    def _():
        carry[...] = jnp.zeros_like(carry)

    row = lax.broadcasted_iota(jnp.int32, (SUBLANES, C_WIDTH), 0)

    def body(t, c):
        rows = pl.ds(pl.multiple_of(t * SUBLANES, SUBLANES), SUBLANES)
        a, b = _scan_tile(af_scr[rows, :], bf_scr[rows, :], row, reverse=False)
        h = b + a * carry[...]
        hf_ref[rows, :] = h
        carry[...] = h[SUBLANES - 1:SUBLANES, :]
        return c

    lax.fori_loop(0, ts // SUBLANES, body, 0, unroll=SCAN_UNROLL)


def _rglru_bwd_kernel(a_ref, b_ref, hf_ref, g_ref, y_ref, carry, *, ts):
    @pl.when(pl.program_id(1) == 0)
    def _():
        carry[...] = jnp.zeros_like(carry)

    row = lax.broadcasted_iota(jnp.int32, (SUBLANES, C_WIDTH), 0)
    nt = ts // SUBLANES

    def body(t, c):
        rows = pl.ds(pl.multiple_of((nt - 1 - t) * SUBLANES, SUBLANES), SUBLANES)
        a, b = _scan_tile(a_ref[rows, :], b_ref[rows, :], row, reverse=True)
        h = b + a * carry[...]
        carry[...] = h[0:1, :]
        y_ref[rows, :] = jax.nn.gelu(g_ref[rows, :]) * (hf_ref[rows, :] + h)
        return c

    lax.fori_loop(0, nt, body, 0, unroll=SCAN_UNROLL)


def _rglru(rest, conv_w, conv_b, w_gates, b_gates, lam, *, b, s, ts=512):
    t = b * s
    ns = s // ts
    hb = ts // SUBLANES
    last = s // SUBLANES - 1
    u_col, g_col = 2, 3
    tile = lambda bi, i: (bi * ns + i, 0)
    hf, ab, bb = pl.pallas_call(
        functools.partial(_rglru_fwd_kernel, ts=ts),
        out_shape=tuple(jax.ShapeDtypeStruct((t, C_WIDTH), F32) for _ in range(3)),
        grid=(b, ns),
        in_specs=[
            pl.BlockSpec((ts, C_WIDTH), lambda bi, i: (bi * ns + i, u_col)),
            pl.BlockSpec((SUBLANES, C_WIDTH),
                         lambda bi, i: (bi * ns * hb + jnp.maximum(i * hb - 1, 0), u_col)),
            pl.BlockSpec((SUBLANES, C_WIDTH),
                         lambda bi, i: (bi * ns * hb + jnp.minimum((i + 1) * hb, last), u_col)),
            pl.BlockSpec((4, C_WIDTH), lambda bi, i: (0, 0)),
            pl.BlockSpec((1, C_WIDTH), lambda bi, i: (0, 0)),
            pl.BlockSpec((C_WIDTH, 4 * C_WIDTH), lambda bi, i: (0, 0)),
            pl.BlockSpec((1, 4 * C_WIDTH), lambda bi, i: (0, 0)),
            pl.BlockSpec((2, C_WIDTH), lambda bi, i: (0, 0)),
        ],
        out_specs=tuple(pl.BlockSpec((ts, C_WIDTH), tile) for _ in range(3)),
        scratch_shapes=[pltpu.VMEM((ts + 2 * SUBLANES, C_WIDTH), F32),
                        pltpu.VMEM((ts, C_WIDTH), F32), pltpu.VMEM((ts, C_WIDTH), F32),
                        pltpu.VMEM((1, C_WIDTH), F32)],
        compiler_params=_params("parallel", "arbitrary"),
        name="rglru_fwd",
    )(rest, rest, rest, conv_w, conv_b, w_gates, b_gates, lam)
    rev = lambda bi, i: (bi * ns + ns - 1 - i, 0)
    return pl.pallas_call(
        functools.partial(_rglru_bwd_kernel, ts=ts),
        out_shape=jax.ShapeDtypeStruct((t, C_WIDTH), F32),
        grid=(b, ns),
        in_specs=[
            pl.BlockSpec((ts, C_WIDTH), rev),
            pl.BlockSpec((ts, C_WIDTH), rev),
            pl.BlockSpec((ts, C_WIDTH), rev),
            pl.BlockSpec((ts, C_WIDTH), lambda bi, i: (bi * ns + ns - 1 - i, g_col)),
        ],
        out_specs=pl.BlockSpec((ts, C_WIDTH), rev),
        scratch_shapes=[pltpu.VMEM((1, C_WIDTH), F32)],
        compiler_params=_params("parallel", "arbitrary"),
        name="rglru_bwd",
    )(ab, bb, hf, rest)


def _outproj_xattn_kernel(x_ref, o1_ref, o2_ref, o3_ref, l1_ref, l2_ref, l3_ref, yb_ref, yc_ref,
                          ga_ref, gb_ref, gc_ref, w_ref, gx_ref, wxq_ref, kv_ref, wxo_ref, out_ref):
    l1, l2, l3 = l1_ref[...], l2_ref[...], l3_ref[...]
    m = jnp.maximum(jnp.maximum(l1, l2), l3)
    e1, e2, e3 = jnp.exp(l1 - m), jnp.exp(l2 - m), jnp.exp(l3 - m)
    z = e1 + e2 + e3
    w1, w2, w3 = e1 / z, e2 / z, e3 / z
    parts = []
    for h in range(A_HEADS):
        parts.append(w1[:, h:h + 1] * o1_ref[h] + w2[:, h:h + 1] * o2_ref[h] + w3[:, h:h + 1] * o3_ref[h])
    ya = jnp.concatenate(parts, axis=1)
    b0, c0 = A_WIDTH, 2 * A_WIDTH
    y = _dot(_rms(yb_ref[...], gb_ref[...]).astype(BF16), w_ref[b0:c0, :])
    y = y + _dot(_rms(yc_ref[...], gc_ref[...]).astype(BF16), w_ref[c0:, :])
    y = y + _dot(_rms(ya, ga_ref[...]).astype(BF16), w_ref[0:b0, :])
    out_ref[...] = _xattn_block(x_ref[...] + y, gx_ref, wxq_ref, kv_ref, wxo_ref)


def _outproj_xattn(x, o_branches, lse_branches, yb, yc, ga, gb, gc, w, gx, wxq, kv, wxo, *, s, n_mem, tm=512):
    t, d = x.shape
    per_seq = s // tm
    row = lambda n: pl.BlockSpec((tm, n), lambda i: (i, 0))
    full = lambda r, n: pl.BlockSpec((r, n), lambda i: (0, 0))
    heads = pl.BlockSpec((A_HEADS, tm, HEAD), lambda i: (0, i, 0))
    return pl.pallas_call(
        _outproj_xattn_kernel,
        out_shape=jax.ShapeDtypeStruct((t, d), F32),
        grid=(t // tm,),
        in_specs=[row(d), heads, heads, heads, row(LANES), row(LANES), row(LANES),
                  row(A_WIDTH), row(C_WIDTH), full(1, A_WIDTH), full(1, A_WIDTH), full(1, C_WIDTH),
                  full(d, d), full(1, d), full(d, X_WIDTH),
                  pl.BlockSpec((n_mem, 2 * X_WIDTH), lambda i: (i // per_seq, 0)), full(X_WIDTH, d)],
        out_specs=row(d),
        compiler_params=_params("parallel"),
        name="outproj_xattn",
    )(x, *o_branches, *lse_branches, yb, yc, ga, gb, gc, w, gx, wxq, kv, wxo)


def _mem_kv_kernel(m_ref, g_ref, w_ref, o_ref):
    o_ref[...] = _dot(_rms(m_ref[...], g_ref[...]).astype(BF16), w_ref[...]).astype(BF16)


def _mem_kv(mem, g, w_kv):
    t, d = mem.shape
    n = w_kv.shape[1]
    tm = 256
    return pl.pallas_call(
        _mem_kv_kernel,
        out_shape=jax.ShapeDtypeStruct((t, n), BF16),
        grid=(t // tm,),
        in_specs=[pl.BlockSpec((tm, d), lambda i: (i, 0)), pl.BlockSpec((1, d), lambda i: (0, 0)),
                  pl.BlockSpec((d, n), lambda i: (0, 0))],
        out_specs=pl.BlockSpec((tm, n), lambda i: (i, 0)),
        compiler_params=_params("parallel"),
        name="mem_kv",
    )(mem, g, w_kv)


def _xattn_block(x, g_ref, wq_ref, kv_ref, wo_ref):
    q = _dot(_rms(x, g_ref[...]).astype(BF16), wq_ref[...]).astype(BF16)
    c = HEAD ** -0.5 * LOG2E
    ones = jnp.ones((kv_ref.shape[0], HEAD), BF16)
    outs = []
    for h in range(X_HEADS):
        k = kv_ref[:, h * HEAD:(h + 1) * HEAD]
        v = kv_ref[:, X_WIDTH + h * HEAD:X_WIDTH + (h + 1) * HEAD]
        s = _dot_nt(q[:, h * HEAD:(h + 1) * HEAD], k)
        p = jnp.exp2((s - jnp.max(s, axis=1, keepdims=True)) * c).astype(BF16)
        pv = _dot(p, jnp.concatenate([v, ones], axis=1))
        outs.append(pv[:, :HEAD] / pv[:, HEAD:])
    o = jnp.concatenate(outs, axis=1).astype(BF16)
    return x + _dot(o, wo_ref[...])


def _cast_kernel(w_ref, o_ref):
    o_ref[...] = w_ref[...].astype(BF16)


def _layer_to_bf16(w, l):
    _, rows, cols = w.shape
    block_rows = next(c for c in (512, 256, 128, 64, 32, 16) if rows % c == 0 and c * cols * 4 <= (4 << 20))
    return pl.pallas_call(
        _cast_kernel,
        out_shape=jax.ShapeDtypeStruct((rows, cols), BF16),
        grid=(rows // block_rows,),
        in_specs=[pl.BlockSpec((None, block_rows, cols), lambda i: (l, i, 0))],
        out_specs=pl.BlockSpec((block_rows, cols), lambda i: (i, 0)),
        compiler_params=_params("parallel"),
        name="cast_bf16",
    )(w)


def _rope_table(s):
    inv = ROPE_THETA ** (-jnp.arange(0, QK_ROPE, 2, dtype=F32) / QK_ROPE)
    ang = jnp.arange(s, dtype=F32)[:, None] * inv[None, :]
    cos, sin = jnp.cos(ang), jnp.sin(ang)
    return jnp.concatenate([cos, cos, -sin, sin], axis=1)


def _swap_halves(w):
    half = w.shape[-1] // 2
    return jnp.concatenate([w[..., half:], w[..., :half]], axis=-1)


def _prep_layer(p, l):
    row = lambda v: v.reshape(1, -1)
    w_in = p['w_in'][l]
    qkv_end, cq_end, ckv_end, kr_end, u_end = 2304, 2816, 3328, 3392, 3904
    w_rope = w_in[:, ckv_end:kr_end]
    w_rest = jnp.concatenate([w_in[:, qkv_end:ckv_end], w_in[:, kr_end:], w_rope, _swap_halves(w_rope)], axis=1)
    wq = p['w_q_up'][l].reshape(-1, B_HEADS, HEAD + QK_ROPE)
    wq = jnp.concatenate([wq, _swap_halves(wq[..., HEAD:])], axis=-1).reshape(-1, B_HEADS * 2 * HEAD)
    eye = jnp.eye(C_BLOCKS, dtype=F32)
    dense = lambda w: jnp.einsum('ncd,nm->ncmd', w, eye).reshape(C_WIDTH, C_WIDTH)
    w_r, w_i = p['w_rg_r'][l], p['w_rg_i'][l]
    w_gates = jnp.concatenate([dense(w_r[0]), dense(w_i[0]), dense(w_r[1]), dense(w_i[1])], axis=1)
    b_r, b_i = p['b_rg_r'][l], p['b_rg_i'][l]
    b_gates = jnp.concatenate([b_r[0], b_i[0], b_r[1], b_i[1]]).reshape(1, -1)
    bf = lambda w: w.astype(BF16)
    big = lambda name: _layer_to_bf16(p[name], l)
    return dict(
        g_ffn1=row(p['g_ffn1'][l]), w1_gate=big('w1_gate'), w1_up=big('w1_up'), w1_down=big('w1_down'),
        g_mix=row(p['g_mix'][l]), w_qkv=bf(w_in[:, :qkv_end]), w_rest=bf(w_rest),
        g_q_lat=row(p['g_q_lat'][l]), g_kv_lat=row(p['g_kv_lat'][l]), w_q=bf(wq), w_kv=bf(p['w_kv_up'][l]),
        conv_w=p['conv_w'][l], conv_b=row(p['conv_b'][l]), w_gates=bf(w_gates), b_gates=b_gates,
        lam=p['rg_lambda'][l],
        g_out_a=row(p['g_out_a'][l]), g_out_b=row(p['g_out_b'][l]), g_out_c=row(p['g_out_c'][l]),
        w_out=big('w_out'),
        g_xattn=row(p['g_xattn'][l]), g_mem=row(p['g_mem'][l]), w_xq=bf(p['w_xq'][l]),
        w_xkv=bf(jnp.concatenate([p['w_xk'][l], p['w_xv'][l]], axis=1)), w_xo=bf(p['w_xo'][l]),
        g_ffn2=row(p['g_ffn2'][l]), w2_gate=big('w2_gate'), w2_up=big('w2_up'), w2_down=big('w2_down'),
    )


def _trunk(x, mem, layers, g_final):
    b, s, d = x.shape
    n_mem = mem.shape[1]
    x = x.reshape(b * s, d)
    mem = mem.reshape(b * n_mem, d)
    tab = _rope_table(s)
    for l, w in enumerate(layers):
        x = _ffn(x, w['g_ffn1'], w['w1_gate'], w['w1_up'], w['w1_down'], g_final, final_norm=False)
        *qkv_by_dil, rest = _inproj(x, w['g_mix'], w['w_qkv'], w['w_rest'], b=b, s=s)
        branches = [_dilated_branch(qkv, dil) for qkv, (_, dil) in zip(qkv_by_dil, DILATED_CONFIGS)]
        q, kt, v = _mla_proj(rest, tab, w['g_q_lat'], w['g_kv_lat'], w['w_q'], w['w_kv'], s=s)
        yb = _mla_attn(q, kt, v, b=b, s=s)
        yc = _rglru(rest, w['conv_w'], w['conv_b'], w['w_gates'], w['b_gates'], w['lam'], b=b, s=s)
        kv = _mem_kv(mem, w['g_mem'], w['w_xkv'])
        x = _outproj_xattn(x, [o for o, _ in branches], [lse for _, lse in branches], yb, yc,
                           w['g_out_a'], w['g_out_b'], w['g_out_c'], w['w_out'],
                           w['g_xattn'], w['w_xq'], kv, w['w_xo'], s=s, n_mem=n_mem)
        x = _ffn(x, w['g_ffn2'], w['w2_gate'], w['w2_up'], w['w2_down'], g_final,
                 final_norm=(l == len(layers) - 1))
    return x.reshape(b, s, d)


def kernel(x_prompt, x_sample, mem_prompt, mem_sample, g_ffn1, w1_gate, w1_up, w1_down, g_mix, w_in, g_q_lat, w_q_up, g_kv_lat, w_kv_up, conv_w, conv_b, w_rg_r, b_rg_r, w_rg_i, b_rg_i, rg_lambda, g_out_a, g_out_b, g_out_c, w_out, g_xattn, g_mem, w_xq, w_xk, w_xv, w_xo, g_ffn2, w2_gate, w2_up, w2_down, g_final):
    p = dict(g_ffn1=g_ffn1, w1_gate=w1_gate, w1_up=w1_up, w1_down=w1_down,
             g_mix=g_mix, w_in=w_in, g_q_lat=g_q_lat, w_q_up=w_q_up, g_kv_lat=g_kv_lat, w_kv_up=w_kv_up,
             conv_w=conv_w, conv_b=conv_b, w_rg_r=w_rg_r, b_rg_r=b_rg_r, w_rg_i=w_rg_i, b_rg_i=b_rg_i,
             rg_lambda=rg_lambda, g_out_a=g_out_a, g_out_b=g_out_b, g_out_c=g_out_c, w_out=w_out,
             g_xattn=g_xattn, g_mem=g_mem, w_xq=w_xq, w_xk=w_xk, w_xv=w_xv, w_xo=w_xo,
             g_ffn2=g_ffn2, w2_gate=w2_gate, w2_up=w2_up, w2_down=w2_down)
    layers = [_prep_layer(p, l) for l in range(g_ffn1.shape[0])]
    gf = g_final.reshape(1, -1)
    return (_trunk(x_prompt, mem_prompt, layers, gf), _trunk(x_sample, mem_sample, layers, gf))
```

```python
import functools
import math

import jax
import jax.numpy as jnp
from jax import lax
from jax.experimental import pallas as pl
from jax.experimental.pallas import tpu as pltpu

BF16 = jnp.bfloat16
F32 = jnp.float32

D_MODEL = 2048
A_HEADS = 6
HEAD = 128
A_WIDTH = A_HEADS * HEAD
DILATED_CONFIGS = ((128, 1), (512, 4), (2048, 16))
B_HEADS = 6
QK_ROPE = 64
ROPE_THETA = 10000.0
C_WIDTH = 512
C_BLOCKS = 8
C_BLOCK_W = 64
RG_C = 8.0
X_HEADS = 4
X_WIDTH = 512
D_FF = 5632
EPS = 1e-6
NEG_INF = -1e30
LOG2E = math.log2(math.e)
LN2 = math.log(2.0)

V7X_VMEM_LIMIT_BYTES = 56 * 1024 * 1024
LANES = 128
SUBLANES = 8

REST_WIDTH = 4 * 512 + LANES
BAND_HALF = 64
SCAN_UNROLL = 4


def _params(*sem):
    return pltpu.CompilerParams(dimension_semantics=sem, vmem_limit_bytes=V7X_VMEM_LIMIT_BYTES)


def _rms(x, g):
    return x * lax.rsqrt(jnp.mean(x * x, axis=-1, keepdims=True) + EPS) * g


def _dot(a, b):
    return jnp.dot(a, b, preferred_element_type=F32)


def _dot_nt(a, b):
    return lax.dot_general(a, b, (((1,), (1,)), ((), ())), preferred_element_type=F32)


def _ffn_kernel(x_ref, g_ref, wg_ref, wu_ref, wd_ref, gf_ref, o_ref, h_scr, *, final_norm):
    j = pl.program_id(1)

    def partial_out(h):
        a = _dot(h, wg_ref[...])
        u = _dot(h, wu_ref[...])
        return _dot((jax.nn.silu(a) * u).astype(BF16), wd_ref[...])

    last = pl.num_programs(1) - 1

    @pl.when(j == 0)
    def _():
        h = _rms(x_ref[...], g_ref[...]).astype(BF16)
        h_scr[...] = h
        o_ref[...] = partial_out(h)

    @pl.when((j > 0) & (j < last))
    def _():
        o_ref[...] += partial_out(h_scr[...])

    @pl.when(j == last)
    def _():
        o_ref[...] = x_ref[...] + 0.5 * (o_ref[...] + partial_out(h_scr[...]))

    if final_norm:
        @pl.when(j == last)
        def _():
            rows_per_pass = 256

            def finish(r, carry):
                rows = pl.ds(pl.multiple_of(r * rows_per_pass, rows_per_pass), rows_per_pass)
                o_ref[rows, :] = _rms(o_ref[rows, :], gf_ref[...])
                return carry

            lax.fori_loop(0, o_ref.shape[0] // rows_per_pass, finish, 0)


def _ffn(x, g, wg, wu, wd, gf, *, final_norm, tm=1024, tf=512):
    t, d = x.shape
    f = wg.shape[1]
    return pl.pallas_call(
        functools.partial(_ffn_kernel, final_norm=final_norm),
        out_shape=jax.ShapeDtypeStruct((t, d), F32),
        grid=(t // tm, f // tf),
        in_specs=[
            pl.BlockSpec((tm, d), lambda i, j: (i, 0)),
            pl.BlockSpec((1, d), lambda i, j: (0, 0)),
            pl.BlockSpec((d, tf), lambda i, j: (0, j)),
            pl.BlockSpec((d, tf), lambda i, j: (0, j)),
            pl.BlockSpec((tf, d), lambda i, j: (j, 0)),
            pl.BlockSpec((1, d), lambda i, j: (0, 0)),
        ],
        out_specs=pl.BlockSpec((tm, d), lambda i, j: (i, 0)),
        scratch_shapes=[pltpu.VMEM((tm, d), BF16)],
        compiler_params=_params("parallel", "arbitrary"),
        name="ffn",
    )(x, g, wg, wu, wd, gf)


def _inproj_kernel(x_ref, g_ref, w1_ref, w2_ref, o1_ref, o4_ref, o16_ref, o2_ref, y_scr, c4_scr, *, tm):
    h = _rms(x_ref[...], g_ref[...]).astype(BF16)
    y = _dot(h, w1_ref[...])
    o1_ref[0] = y.astype(BF16)
    n4 = tm // 4
    for cb in range(y.shape[1] // LANES):
        cols = slice(cb * LANES, (cb + 1) * LANES)
        y_scr[cb] = y[:, cols]
        for r4 in range(4):
            c4 = y_scr[cb, pl.ds(r4, n4, stride=4), :]
            o4_ref[r4, :, cols] = c4.astype(BF16)
            c4_scr[cb, r4] = c4
            for j in range(4):
                o16_ref[r4 + 4 * j, :, cols] = c4_scr[cb, r4, pl.ds(j, n4 // 4, stride=4), :].astype(BF16)
    o2_ref[...] = _dot(h, w2_ref[...])


def _inproj(x, g, w_qkv, w_rest, *, b, s, tm=256):
    t, d = x.shape
    n1, n2 = w_qkv.shape[1], w_rest.shape[1]
    per_seq = s // tm
    dils = [dil for _, dil in DILATED_CONFIGS]
    cls_shape = lambda dil: jax.ShapeDtypeStruct((b, dil, s // dil, n1), BF16)
    cls_spec = lambda dil: pl.BlockSpec((None, dil, tm // dil, n1), lambda i: (i // per_seq, 0, i % per_seq, 0))
    resident = dict(pipeline_mode=pl.Buffered(1))
    return pl.pallas_call(
        functools.partial(_inproj_kernel, tm=tm),
        out_shape=(*[cls_shape(dil) for dil in dils], jax.ShapeDtypeStruct((t, n2), F32)),
        grid=(t // tm,),
        in_specs=[
            pl.BlockSpec((tm, d), lambda i: (i, 0)),
            pl.BlockSpec((1, d), lambda i: (0, 0)),
            pl.BlockSpec((d, n1), lambda i: (0, 0), **resident),
            pl.BlockSpec((d, n2), lambda i: (0, 0), **resident),
        ],
        out_specs=(*[cls_spec(dil) for dil in dils], pl.BlockSpec((tm, n2), lambda i: (i, 0))),
        scratch_shapes=[pltpu.VMEM((n1 // LANES, tm, LANES), F32),
                        pltpu.VMEM((n1 // LANES, 4, tm // 4, LANES), F32)],
        compiler_params=_params("parallel"),
        name="inproj",
    )(x, g, w_qkv, w_rest)


def _dilated_kernel(q_ref, kp_ref, kc_ref, kn_ref, vp_ref, vc_ref, vn_ref, bias_ref, o_ref, lse_ref,
                    k_scr, v_scr, *, tq, n_cls, dil, rg):
    i = pl.program_id(1)
    scale = HEAD ** -0.5
    sub = 2 * BAND_HALF
    win = sub + 2 * BAND_HALF
    lane = lax.broadcasted_iota(jnp.int32, (sub, LANES), 1)
    scale2 = scale * LOG2E
    ones = jnp.ones((win, HEAD), BF16)
    for rr in range(rg):
        r = pl.program_id(2) * rg + rr
        k_scr[0:BAND_HALF] = kp_ref[rr]
        k_scr[BAND_HALF:BAND_HALF + tq] = kc_ref[rr]
        k_scr[BAND_HALF + tq:] = kn_ref[rr]
        v_scr[0:BAND_HALF] = vp_ref[rr]
        v_scr[BAND_HALF:BAND_HALF + tq] = vc_ref[rr]
        v_scr[BAND_HALF + tq:] = vn_ref[rr]
        for a in range(tq // sub):
            rows = pl.ds(a * sub, sub) if dil == 1 else pl.ds(a * sub * dil + r, sub, stride=dil)
            edge = a == 0 or a == tq // sub - 1
            kidx = i * tq + (a * sub - BAND_HALF) + lax.broadcasted_iota(jnp.int32, (1, win), 1)
            valid = (kidx >= 0) & (kidx < n_cls)
            lse_all = jnp.zeros((sub, LANES), F32)
            for h in range(A_HEADS):
                cols = slice(h * HEAD, (h + 1) * HEAD)
                q = q_ref[rr, a * sub:(a + 1) * sub, cols]
                k = k_scr[a * sub:a * sub + win, cols]
                v = v_scr[a * sub:a * sub + win, cols]
                s2 = _dot_nt(q, k) * scale2 + bias_ref[h]
                if edge:
                    s2 = jnp.where(valid, s2, NEG_INF)
                m2 = jnp.max(s2, axis=1, keepdims=True)
                p = jnp.exp2(s2 - m2).astype(BF16)
                pv = _dot(p, jnp.concatenate([v, ones], axis=1))
                l = pv[:, HEAD:]
                o_ref[h, rows, :] = pv[:, :HEAD] / l
                lse_all = jnp.where(lane == h, m2 * LN2 + jnp.log(l), lse_all)
            lse_ref[rows, :] = lse_all


def _band_bias(dil):
    slopes = 2.0 ** (-8.0 * jnp.arange(1, A_HEADS + 1, dtype=F32) / A_HEADS)
    sub, win = 2 * BAND_HALF, 4 * BAND_HALF
    rel = jnp.abs(BAND_HALF + jnp.arange(sub)[:, None] - jnp.arange(win)[None, :])
    bias = -slopes[:, None, None] * (dil * rel).astype(F32)[None]
    return jnp.where((rel <= BAND_HALF)[None], bias * LOG2E, NEG_INF)


def _dilated_branch(qkv, dil):
    b, _, n_cls, _ = qkv.shape
    queries_per_step = 1024
    tq = min(queries_per_step, n_cls, 4096 // dil)
    nq = n_cls // tq
    nh = tq // BAND_HALF
    last_halo = n_cls // BAND_HALF - 1
    rg = min(dil, queries_per_step // tq)

    def cur(which):
        return pl.BlockSpec((None, rg, tq, A_WIDTH), lambda bi, i, r: (bi, r, i, which))

    def prev(which):
        return pl.BlockSpec((None, rg, BAND_HALF, A_WIDTH),
                            lambda bi, i, r: (bi, r, jnp.maximum(i * nh - 1, 0), which))

    def nxt(which):
        return pl.BlockSpec((None, rg, BAND_HALF, A_WIDTH),
                            lambda bi, i, r: (bi, r, jnp.minimum((i + 1) * nh, last_halo), which))

    t = b * n_cls * dil
    return pl.pallas_call(
        functools.partial(_dilated_kernel, tq=tq, n_cls=n_cls, dil=dil, rg=rg),
        out_shape=(jax.ShapeDtypeStruct((A_HEADS, t, HEAD), F32), jax.ShapeDtypeStruct((t, LANES), F32)),
        grid=(b, nq, dil // rg),
        in_specs=[cur(0), prev(1), cur(1), nxt(1), prev(2), cur(2), nxt(2),
                  pl.BlockSpec((A_HEADS, 2 * BAND_HALF, 4 * BAND_HALF), lambda bi, i, r: (0, 0, 0))],
        out_specs=(pl.BlockSpec((A_HEADS, tq * dil, HEAD), lambda bi, i, r: (0, bi * nq + i, 0)),
                   pl.BlockSpec((tq * dil, LANES), lambda bi, i, r: (bi * nq + i, 0))),
        scratch_shapes=[pltpu.VMEM((tq + 2 * BAND_HALF, A_WIDTH), BF16),
                        pltpu.VMEM((tq + 2 * BAND_HALF, A_WIDTH), BF16)],
        compiler_params=_params("parallel", "parallel", "arbitrary"),
        name=f"dilated{dil}",
    )(qkv, qkv, qkv, qkv, qkv, qkv, qkv, _band_bias(dil))


def _rotate(y, tab):
    z = y * tab
    r = z + pltpu.roll(z, QK_ROPE, axis=1)
    lane = lax.broadcasted_iota(jnp.int32, r.shape, 1)
    return jnp.where(lane < QK_ROPE, r, 0.0)


def _mla_proj_kernel(cq_ref, ckv_ref, kr_ref, tab_ref, gq_ref, gkv_ref, wq_ref, wkv_ref,
                     q_ref, kt_ref, v_ref):
    cq = _rms(cq_ref[...], gq_ref[...]).astype(BF16)
    ckv = _rms(ckv_ref[...], gkv_ref[...]).astype(BF16)
    tab = tab_ref[...]
    kr_t = _rotate(kr_ref[...], tab).T.astype(BF16)
    for h in range(B_HEADS):
        lo, mid, hi = 2 * h * HEAD, (2 * h + 1) * HEAD, (2 * h + 2) * HEAD
        q = _dot(cq, wq_ref[:, lo:hi])
        q_ref[:, lo:mid] = q[:, :HEAD].astype(BF16)
        q_ref[:, mid:hi] = _rotate(q[:, HEAD:], tab).astype(BF16)
        kv = _dot(ckv, wkv_ref[:, lo:hi])
        kt_ref[h, :HEAD, :] = kv[:, :HEAD].T.astype(BF16)
        kt_ref[h, HEAD:, :] = kr_t
        v_ref[:, h * HEAD:(h + 1) * HEAD] = kv[:, HEAD:].astype(BF16)


def _mla_proj(rest, tab, gq, gkv, wq, wkv, *, s, tm=512):
    t = rest.shape[0]
    lat = 512
    width = B_HEADS * 2 * HEAD
    pos_blocks = s // tm
    return pl.pallas_call(
        _mla_proj_kernel,
        out_shape=(jax.ShapeDtypeStruct((t, width), BF16),
                   jax.ShapeDtypeStruct((B_HEADS, t // tm, 2 * HEAD, tm), BF16),
                   jax.ShapeDtypeStruct((t, B_HEADS * HEAD), BF16)),
        grid=(t // tm,),
        in_specs=[
            pl.BlockSpec((tm, lat), lambda i: (i, 0)),
            pl.BlockSpec((tm, lat), lambda i: (i, 1)),
            pl.BlockSpec((tm, LANES), lambda i: (i, 4 * lat // LANES)),
            pl.BlockSpec((tm, LANES), lambda i: (i % pos_blocks, 0)),
            pl.BlockSpec((1, lat), lambda i: (0, 0)),
            pl.BlockSpec((1, lat), lambda i: (0, 0)),
            pl.BlockSpec((lat, width), lambda i: (0, 0)),
            pl.BlockSpec((lat, width), lambda i: (0, 0)),
        ],
        out_specs=(pl.BlockSpec((tm, width), lambda i: (i, 0)),
                   pl.BlockSpec((B_HEADS, None, 2 * HEAD, tm), lambda i: (0, i, 0, 0)),
                   pl.BlockSpec((tm, B_HEADS * HEAD), lambda i: (i, 0))),
        compiler_params=_params("parallel"),
        name="mla_proj",
    )(rest, rest, rest, tab, gq, gkv, wq, wkv)


def _mla_attn_kernel(q_ref, kt_ref, v_ref, o_ref, m_scr, a_scr, acc_scr, s_scr, p_scr, mx_scr, *, tq, tk, nk):
    c = (HEAD + QK_ROPE) ** -0.5 * math.log2(math.e)
    q = q_ref[...]
    nl = tk // LANES
    ones = jnp.ones((tk, HEAD), BF16)

    def chunk(kk):
        return pl.ds(pl.multiple_of(kk * tk, tk), tk)

    def scores(kk, slot):
        s = _dot(q, kt_ref[kk])
        s_scr[slot] = s
        mx = s[:, :LANES]
        for j in range(1, nl):
            mx = jnp.maximum(mx, s[:, j * LANES:(j + 1) * LANES])
        mx_scr[slot] = mx

    def softmax(slot, first):
        s = s_scr[slot]
        m_new = jnp.broadcast_to(jnp.max(mx_scr[slot], axis=1, keepdims=True), (tq, LANES))
        if not first:
            m_old = m_scr[...]
            m_new = jnp.maximum(m_old, m_new)
            a_scr[slot] = jnp.exp2((m_old - m_new) * c)
        for j in range(nl):
            cols = slice(j * LANES, (j + 1) * LANES)
            p_scr[slot, :, cols] = jnp.exp2((s[:, cols] - m_new) * c).astype(BF16)
        m_scr[...] = m_new

    def values(kk, slot, first):
        pv = _dot(p_scr[slot], jnp.concatenate([v_ref[chunk(kk), :], ones], axis=1))
        if first:
            acc_scr[...] = pv
        else:
            alpha = a_scr[slot]
            acc_scr[...] = jnp.concatenate([alpha, alpha], axis=1) * acc_scr[...] + pv

    scores(0, 0)
    softmax(0, True)
    scores(1, 1)
    softmax(1, False)
    values(0, 0, True)
    scores(2, 0)

    pairs = next(n for n in (3, 2, 1) if (nk - 4) % (2 * n) == 0)

    def body(g, carry):
        for u in range(pairs):
            kk = 2 * (pairs * g + u + 1)
            softmax(0, False)
            values(kk - 1, 1, False)
            scores(kk + 1, 1)
            softmax(1, False)
            values(kk, 0, False)
            scores(kk + 2, 0)
        return carry

    lax.fori_loop(0, (nk - 4) // (2 * pairs), body, 0)
    softmax(0, False)
    values(nk - 3, 1, False)
    scores(nk - 1, 1)
    softmax(1, False)
    values(nk - 2, 0, False)
    values(nk - 1, 1, False)
    o_ref[...] = acc_scr[:, :HEAD] / acc_scr[:, HEAD:]


def _mla_attn(q, kt, v, *, b, s, tq=1024):
    t = b * s
    nq = s // tq
    tk = kt.shape[-1]
    nk = s // tk
    assert nk % 2 == 0 and nk >= 4
    return pl.pallas_call(
        functools.partial(_mla_attn_kernel, tq=tq, tk=tk, nk=nk),
        out_shape=jax.ShapeDtypeStruct((t, B_HEADS * HEAD), F32),
        grid=(b, B_HEADS, nq),
        in_specs=[
            pl.BlockSpec((tq, 2 * HEAD), lambda bi, h, i: (bi * nq + i, h)),
            pl.BlockSpec((None, nk, 2 * HEAD, tk), lambda bi, h, i: (h, bi, 0, 0)),
            pl.BlockSpec((s, HEAD), lambda bi, h, i: (bi, h)),
        ],
        out_specs=pl.BlockSpec((tq, HEAD), lambda bi, h, i: (bi * nq + i, h)),
        scratch_shapes=[pltpu.VMEM((tq, LANES), F32), pltpu.VMEM((2, tq, LANES), F32),
                        pltpu.VMEM((tq, 2 * HEAD), F32), pltpu.VMEM((2, tq, tk), F32),
                        pltpu.VMEM((2, tq, tk), BF16), pltpu.VMEM((2, tq, LANES), F32)],
        compiler_params=_params("parallel", "parallel", "arbitrary"),
        name="mla_attn",
    )(q, kt, v)


def _softplus(x):
    return jnp.maximum(x, 0.0) + jnp.log1p(jnp.exp(-jnp.abs(x)))


def _scan_tile(a, b, row, reverse):
    for sh in (1, 2, 4):
        if reverse:
            keep = row < SUBLANES - sh
            a_s = pltpu.roll(a, SUBLANES - sh, axis=0)
            b_s = pltpu.roll(b, SUBLANES - sh, axis=0)
        else:
            keep = row >= sh
            a_s = pltpu.roll(a, sh, axis=0)
            b_s = pltpu.roll(b, sh, axis=0)
        b = b + a * jnp.where(keep, b_s, 0.0)
        a = a * jnp.where(keep, a_s, 1.0)
    return a, b


def _rglru_fwd_kernel(uc_ref, up_ref, un_ref, cw_ref, cb_ref, wg_ref, bg_ref, lam_ref,
                      hf_ref, ab_ref, bb_ref, ubuf, af_scr, bf_scr, carry, *, ts):
    i = pl.program_id(1)
    ns = pl.num_programs(1)
    ubuf[0:SUBLANES] = jnp.where(i > 0, up_ref[...], 0.0)
    ubuf[SUBLANES:SUBLANES + ts] = uc_ref[...]
    ubuf[SUBLANES + ts:] = jnp.where(i < ns - 1, un_ref[...], 0.0)
    u = cb_ref[...] + cw_ref[0:1, :] * ubuf[SUBLANES - 2:SUBLANES - 2 + ts]
    u = u + cw_ref[1:2, :] * ubuf[SUBLANES - 1:SUBLANES - 1 + ts]
    u = u + cw_ref[2:3, :] * ubuf[SUBLANES:SUBLANES + ts]
    u = u + cw_ref[3:4, :] * ubuf[SUBLANES + 1:SUBLANES + 1 + ts]
    gates = _dot(u.astype(BF16), wg_ref[...]) + bg_ref[...]
    sp = _softplus(-lam_ref[...])
    for d in range(2):
        r = jax.nn.sigmoid(gates[:, (2 * d) * C_WIDTH:(2 * d + 1) * C_WIDTH])
        ig = jax.nn.sigmoid(gates[:, (2 * d + 1) * C_WIDTH:(2 * d + 2) * C_WIDTH])
        log_a = -RG_C * r * sp[d:d + 1, :]
        a = jnp.exp(log_a)
        bterm = jnp.sqrt(-jnp.tanh(log_a) * (a * a + 1.0)) * (ig * u)
        if d == 0:
            af_scr[...] = a
            bf_scr[...] = bterm
        else:
            ab_ref[...] = a
            bb_ref[...] = bterm

    @pl.when(i == 0)
    def _():
        carry[...] = jnp.zeros_like(carry)

    row = lax.broadcasted_iota(jnp.int32, (SUBLANES, C_WIDTH), 0)

    def body(t, c):
        rows = pl.ds(pl.multiple_of(t * SUBLANES, SUBLANES), SUBLANES)
        a, b = _scan_tile(af_scr[rows, :], bf_scr[rows, :], row, reverse=False)
        h = b + a * carry[...]
        hf_ref[rows, :] = h
        carry[...] = h[SUBLANES - 1:SUBLANES, :]
        return c

    lax.fori_loop(0, ts // SUBLANES, body, 0, unroll=SCAN_UNROLL)


def _rglru_bwd_kernel(a_ref, b_ref, hf_ref, g_ref, y_ref, carry, *, ts):
    @pl.when(pl.program_id(1) == 0)
    def _():
        carry[...] = jnp.zeros_like(carry)

    row = lax.broadcasted_iota(jnp.int32, (SUBLANES, C_WIDTH), 0)
    nt = ts // SUBLANES

    def body(t, c):
        rows = pl.ds(pl.multiple_of((nt - 1 - t) * SUBLANES, SUBLANES), SUBLANES)
        a, b = _scan_tile(a_ref[rows, :], b_ref[rows, :], row, reverse=True)
        h = b + a * carry[...]
        carry[...] = h[0:1, :]
        y_ref[rows, :] = jax.nn.gelu(g_ref[rows, :]) * (hf_ref[rows, :] + h)
        return c

    lax.fori_loop(0, nt, body, 0, unroll=SCAN_UNROLL)


def _rglru(rest, conv_w, conv_b, w_gates, b_gates, lam, *, b, s, ts=512):
    t = b * s
    ns = s // ts
    hb = ts // SUBLANES
    last = s // SUBLANES - 1
    u_col, g_col = 2, 3
    tile = lambda bi, i: (bi * ns + i, 0)
    hf, ab, bb = pl.pallas_call(
        functools.partial(_rglru_fwd_kernel, ts=ts),
        out_shape=tuple(jax.ShapeDtypeStruct((t, C_WIDTH), F32) for _ in range(3)),
        grid=(b, ns),
        in_specs=[
            pl.BlockSpec((ts, C_WIDTH), lambda bi, i: (bi * ns + i, u_col)),
            pl.BlockSpec((SUBLANES, C_WIDTH),
                         lambda bi, i: (bi * ns * hb + jnp.maximum(i * hb - 1, 0), u_col)),
            pl.BlockSpec((SUBLANES, C_WIDTH),
                         lambda bi, i: (bi * ns * hb + jnp.minimum((i + 1) * hb, last), u_col)),
            pl.BlockSpec((4, C_WIDTH), lambda bi, i: (0, 0)),
            pl.BlockSpec((1, C_WIDTH), lambda bi, i: (0, 0)),
            pl.BlockSpec((C_WIDTH, 4 * C_WIDTH), lambda bi, i: (0, 0)),
            pl.BlockSpec((1, 4 * C_WIDTH), lambda bi, i: (0, 0)),
            pl.BlockSpec((2, C_WIDTH), lambda bi, i: (0, 0)),
        ],
        out_specs=tuple(pl.BlockSpec((ts, C_WIDTH), tile) for _ in range(3)),
        scratch_shapes=[pltpu.VMEM((ts + 2 * SUBLANES, C_WIDTH), F32),
                        pltpu.VMEM((ts, C_WIDTH), F32), pltpu.VMEM((ts, C_WIDTH), F32),
                        pltpu.VMEM((1, C_WIDTH), F32)],
        compiler_params=_params("parallel", "arbitrary"),
        name="rglru_fwd",
    )(rest, rest, rest, conv_w, conv_b, w_gates, b_gates, lam)
    rev = lambda bi, i: (bi * ns + ns - 1 - i, 0)
    return pl.pallas_call(
        functools.partial(_rglru_bwd_kernel, ts=ts),
        out_shape=jax.ShapeDtypeStruct((t, C_WIDTH), F32),
        grid=(b, ns),
        in_specs=[
            pl.BlockSpec((ts, C_WIDTH), rev),
            pl.BlockSpec((ts, C_WIDTH), rev),
            pl.BlockSpec((ts, C_WIDTH), rev),
            pl.BlockSpec((ts, C_WIDTH), lambda bi, i: (bi * ns + ns - 1 - i, g_col)),
        ],
        out_specs=pl.BlockSpec((ts, C_WIDTH), rev),
        scratch_shapes=[pltpu.VMEM((1, C_WIDTH), F32)],
        compiler_params=_params("parallel", "arbitrary"),
        name="rglru_bwd",
    )(ab, bb, hf, rest)


def _outproj_xattn_kernel(x_ref, o1_ref, o2_ref, o3_ref, l1_ref, l2_ref, l3_ref, yb_ref, yc_ref,
                          ga_ref, gb_ref, gc_ref, w_ref, gx_ref, wxq_ref, kv_ref, wxo_ref, out_ref):
    l1, l2, l3 = l1_ref[...], l2_ref[...], l3_ref[...]
    m = jnp.maximum(jnp.maximum(l1, l2), l3)
    e1, e2, e3 = jnp.exp(l1 - m), jnp.exp(l2 - m), jnp.exp(l3 - m)
    z = e1 + e2 + e3
    w1, w2, w3 = e1 / z, e2 / z, e3 / z
    parts = []
    for h in range(A_HEADS):
        parts.append(w1[:, h:h + 1] * o1_ref[h] + w2[:, h:h + 1] * o2_ref[h] + w3[:, h:h + 1] * o3_ref[h])
    ya = jnp.concatenate(parts, axis=1)
    b0, c0 = A_WIDTH, 2 * A_WIDTH
    y = _dot(_rms(yb_ref[...], gb_ref[...]).astype(BF16), w_ref[b0:c0, :])
    y = y + _dot(_rms(yc_ref[...], gc_ref[...]).astype(BF16), w_ref[c0:, :])
    y = y + _dot(_rms(ya, ga_ref[...]).astype(BF16), w_ref[0:b0, :])
    out_ref[...] = _xattn_block(x_ref[...] + y, gx_ref, wxq_ref, kv_ref, wxo_ref)


def _outproj_xattn(x, o_branches, lse_branches, yb, yc, ga, gb, gc, w, gx, wxq, kv, wxo, *, s, n_mem, tm=512):
    t, d = x.shape
    per_seq = s // tm
    row = lambda n: pl.BlockSpec((tm, n), lambda i: (i, 0))
    full = lambda r, n: pl.BlockSpec((r, n), lambda i: (0, 0))
    heads = pl.BlockSpec((A_HEADS, tm, HEAD), lambda i: (0, i, 0))
    return pl.pallas_call(
        _outproj_xattn_kernel,
        out_shape=jax.ShapeDtypeStruct((t, d), F32),
        grid=(t // tm,),
        in_specs=[row(d), heads, heads, heads, row(LANES), row(LANES), row(LANES),
                  row(A_WIDTH), row(C_WIDTH), full(1, A_WIDTH), full(1, A_WIDTH), full(1, C_WIDTH),
                  full(d, d), full(1, d), full(d, X_WIDTH),
                  pl.BlockSpec((n_mem, 2 * X_WIDTH), lambda i: (i // per_seq, 0)), full(X_WIDTH, d)],
        out_specs=row(d),
        compiler_params=_params("parallel"),
        name="outproj_xattn",
    )(x, *o_branches, *lse_branches, yb, yc, ga, gb, gc, w, gx, wxq, kv, wxo)


def _mem_kv_kernel(m_ref, g_ref, w_ref, o_ref):
    o_ref[...] = _dot(_rms(m_ref[...], g_ref[...]).astype(BF16), w_ref[...]).astype(BF16)


def _mem_kv(mem, g, w_kv):
    t, d = mem.shape
    n = w_kv.shape[1]
    tm = 256
    return pl.pallas_call(
        _mem_kv_kernel,
        out_shape=jax.ShapeDtypeStruct((t, n), BF16),
        grid=(t // tm,),
        in_specs=[pl.BlockSpec((tm, d), lambda i: (i, 0)), pl.BlockSpec((1, d), lambda i: (0, 0)),
                  pl.BlockSpec((d, n), lambda i: (0, 0))],
        out_specs=pl.BlockSpec((tm, n), lambda i: (i, 0)),
        compiler_params=_params("parallel"),
        name="mem_kv",
    )(mem, g, w_kv)


def _xattn_block(x, g_ref, wq_ref, kv_ref, wo_ref):
    q = _dot(_rms(x, g_ref[...]).astype(BF16), wq_ref[...]).astype(BF16)
    c = HEAD ** -0.5 * LOG2E
    ones = jnp.ones((kv_ref.shape[0], HEAD), BF16)
    outs = []
    for h in range(X_HEADS):
        k = kv_ref[:, h * HEAD:(h + 1) * HEAD]
        v = kv_ref[:, X_WIDTH + h * HEAD:X_WIDTH + (h + 1) * HEAD]
        s = _dot_nt(q[:, h * HEAD:(h + 1) * HEAD], k)
        p = jnp.exp2((s - jnp.max(s, axis=1, keepdims=True)) * c).astype(BF16)
        pv = _dot(p, jnp.concatenate([v, ones], axis=1))
        outs.append(pv[:, :HEAD] / pv[:, HEAD:])
    o = jnp.concatenate(outs, axis=1).astype(BF16)
    return x + _dot(o, wo_ref[...])


def _cast_kernel(w_ref, o_ref):
    o_ref[...] = w_ref[...].astype(BF16)


def _layer_to_bf16(w, l):
    _, rows, cols = w.shape
    block_rows = next(c for c in (512, 256, 128, 64, 32, 16) if rows % c == 0 and c * cols * 4 <= (4 << 20))
    return pl.pallas_call(
        _cast_kernel,
        out_shape=jax.ShapeDtypeStruct((rows, cols), BF16),
        grid=(rows // block_rows,),
        in_specs=[pl.BlockSpec((None, block_rows, cols), lambda i: (l, i, 0))],
        out_specs=pl.BlockSpec((block_rows, cols), lambda i: (i, 0)),
        compiler_params=_params("parallel"),
        name="cast_bf16",
    )(w)


def _rope_table(s):
    inv = ROPE_THETA ** (-jnp.arange(0, QK_ROPE, 2, dtype=F32) / QK_ROPE)
    ang = jnp.arange(s, dtype=F32)[:, None] * inv[None, :]
    cos, sin = jnp.cos(ang), jnp.sin(ang)
    return jnp.concatenate([cos, cos, -sin, sin], axis=1)


def _swap_halves(w):
    half = w.shape[-1] // 2
    return jnp.concatenate([w[..., half:], w[..., :half]], axis=-1)


def _prep_layer(p, l):
    row = lambda v: v.reshape(1, -1)
    w_in = p['w_in'][l]
    qkv_end, cq_end, ckv_end, kr_end, u_end = 2304, 2816, 3328, 3392, 3904
    w_rope = w_in[:, ckv_end:kr_end]
    w_rest = jnp.concatenate([w_in[:, qkv_end:ckv_end], w_in[:, kr_end:], w_rope, _swap_halves(w_rope)], axis=1)
    wq = p['w_q_up'][l].reshape(-1, B_HEADS, HEAD + QK_ROPE)
    wq = jnp.concatenate([wq, _swap_halves(wq[..., HEAD:])], axis=-1).reshape(-1, B_HEADS * 2 * HEAD)
    eye = jnp.eye(C_BLOCKS, dtype=F32)
    dense = lambda w: jnp.einsum('ncd,nm->ncmd', w, eye).reshape(C_WIDTH, C_WIDTH)
    w_r, w_i = p['w_rg_r'][l], p['w_rg_i'][l]
    w_gates = jnp.concatenate([dense(w_r[0]), dense(w_i[0]), dense(w_r[1]), dense(w_i[1])], axis=1)
    b_r, b_i = p['b_rg_r'][l], p['b_rg_i'][l]
    b_gates = jnp.concatenate([b_r[0], b_i[0], b_r[1], b_i[1]]).reshape(1, -1)
    bf = lambda w: w.astype(BF16)
    big = lambda name: _layer_to_bf16(p[name], l)
    return dict(
        g_ffn1=row(p['g_ffn1'][l]), w1_gate=big('w1_gate'), w1_up=big('w1_up'), w1_down=big('w1_down'),
        g_mix=row(p['g_mix'][l]), w_qkv=bf(w_in[:, :qkv_end]), w_rest=bf(w_rest),
        g_q_lat=row(p['g_q_lat'][l]), g_kv_lat=row(p['g_kv_lat'][l]), w_q=bf(wq), w_kv=bf(p['w_kv_up'][l]),
        conv_w=p['conv_w'][l], conv_b=row(p['conv_b'][l]), w_gates=bf(w_gates), b_gates=b_gates,
        lam=p['rg_lambda'][l],
        g_out_a=row(p['g_out_a'][l]), g_out_b=row(p['g_out_b'][l]), g_out_c=row(p['g_out_c'][l]),
        w_out=big('w_out'),
        g_xattn=row(p['g_xattn'][l]), g_mem=row(p['g_mem'][l]), w_xq=bf(p['w_xq'][l]),
        w_xkv=bf(jnp.concatenate([p['w_xk'][l], p['w_xv'][l]], axis=1)), w_xo=bf(p['w_xo'][l]),
        g_ffn2=row(p['g_ffn2'][l]), w2_gate=big('w2_gate'), w2_up=big('w2_up'), w2_down=big('w2_down'),
    )


def _trunk(x, mem, layers, g_final):
    b, s, d = x.shape
    n_mem = mem.shape[1]
    x = x.reshape(b * s, d)
    mem = mem.reshape(b * n_mem, d)
    tab = _rope_table(s)
    for l, w in enumerate(layers):
        x = _ffn(x, w['g_ffn1'], w['w1_gate'], w['w1_up'], w['w1_down'], g_final, final_norm=False)
        *qkv_by_dil, rest = _inproj(x, w['g_mix'], w['w_qkv'], w['w_rest'], b=b, s=s)
        branches = [_dilated_branch(qkv, dil) for qkv, (_, dil) in zip(qkv_by_dil, DILATED_CONFIGS)]
        q, kt, v = _mla_proj(rest, tab, w['g_q_lat'], w['g_kv_lat'], w['w_q'], w['w_kv'], s=s)
        yb = _mla_attn(q, kt, v, b=b, s=s)
        yc = _rglru(rest, w['conv_w'], w['conv_b'], w['w_gates'], w['b_gates'], w['lam'], b=b, s=s)
        kv = _mem_kv(mem, w['g_mem'], w['w_xkv'])
        x = _outproj_xattn(x, [o for o, _ in branches], [lse for _, lse in branches], yb, yc,
                           w['g_out_a'], w['g_out_b'], w['g_out_c'], w['w_out'],
                           w['g_xattn'], w['w_xq'], kv, w['w_xo'], s=s, n_mem=n_mem)
        x = _ffn(x, w['g_ffn2'], w['w2_gate'], w['w2_up'], w['w2_down'], g_final,
                 final_norm=(l == len(layers) - 1))
    return x.reshape(b, s, d)


def kernel(x_prompt, x_sample, mem_prompt, mem_sample, g_ffn1, w1_gate, w1_up, w1_down, g_mix, w_in, g_q_lat, w_q_up, g_kv_lat, w_kv_up, conv_w, conv_b, w_rg_r, b_rg_r, w_rg_i, b_rg_i, rg_lambda, g_out_a, g_out_b, g_out_c, w_out, g_xattn, g_mem, w_xq, w_xk, w_xv, w_xo, g_ffn2, w2_gate, w2_up, w2_down, g_final):
    p = dict(g_ffn1=g_ffn1, w1_gate=w1_gate, w1_up=w1_up, w1_down=w1_down,
             g_mix=g_mix, w_in=w_in, g_q_lat=g_q_lat, w_q_up=w_q_up, g_kv_lat=g_kv_lat, w_kv_up=w_kv_up,
             conv_w=conv_w, conv_b=conv_b, w_rg_r=w_rg_r, b_rg_r=b_rg_r, w_rg_i=w_rg_i, b_rg_i=b_rg_i,
             rg_lambda=rg_lambda, g_out_a=g_out_a, g_out_b=g_out_b, g_out_c=g_out_c, w_out=w_out,
             g_xattn=g_xattn, g_mem=g_mem, w_xq=w_xq, w_xk=w_xk, w_xv=w_xv, w_xo=w_xo,
             g_ffn2=g_ffn2, w2_gate=w2_gate, w2_up=w2_up, w2_down=w2_down)
    layers = [_prep_layer(p, l) for l in range(g_ffn1.shape[0])]
    gf = g_final.reshape(1, -1)
    return (_trunk(x_prompt, mem_prompt, layers, gf), _trunk(x_sample, mem_sample, layers, gf))
```

```python
import functools
import math

import jax
import jax.numpy as jnp
from jax import lax
from jax.experimental import pallas as pl
from jax.experimental.pallas import tpu as pltpu

BF16 = jnp.bfloat16
F32 = jnp.float32

D_MODEL = 2048
A_HEADS = 6
HEAD = 128
A_WIDTH = A_HEADS * HEAD
DILATED_CONFIGS = ((128, 1), (512, 4), (2048, 16))
B_HEADS = 6
QK_ROPE = 64
ROPE_THETA = 10000.0
C_WIDTH = 512
C_BLOCKS = 8
C_BLOCK_W = 64
RG_C = 8.0
X_HEADS = 4
X_WIDTH = 512
D_FF = 5632
EPS = 1e-6
NEG_INF = -1e30
LOG2E = math.log2(math.e)
LN2 = math.log(2.0)

V7X_VMEM_LIMIT_BYTES = 56 * 1024 * 1024
LANES = 128
SUBLANES = 8

LATENT = 512
BAND_HALF = 64
SCAN_UNROLL = 4


def _params(*sem):
    return pltpu.CompilerParams(dimension_semantics=sem, vmem_limit_bytes=V7X_VMEM_LIMIT_BYTES)


def _rms(x, g):
    return x * lax.rsqrt(jnp.mean(x * x, axis=-1, keepdims=True) + EPS) * g


def _dot(a, b):
    return jnp.dot(a, b, preferred_element_type=F32)


def _dot_nt(a, b):
    return lax.dot_general(a, b, (((1,), (1,)), ((), ())), preferred_element_type=F32)


def _ffn_kernel(x_ref, g_ref, wg_ref, wu_ref, wd_ref, gf_ref, o_ref, h_scr, *, final_norm):
    j = pl.program_id(1)

    def partial_out(h):
        a = _dot(h, wg_ref[...])
        u = _dot(h, wu_ref[...])
        return _dot((jax.nn.silu(a) * u).astype(BF16), wd_ref[...])

    last = pl.num_programs(1) - 1

    @pl.when(j == 0)
    def _():
        h = _rms(x_ref[...], g_ref[...]).astype(BF16)
        h_scr[...] = h
        o_ref[...] = partial_out(h)

    @pl.when((j > 0) & (j < last))
    def _():
        o_ref[...] += partial_out(h_scr[...])

    @pl.when(j == last)
    def _():
        o_ref[...] = x_ref[...] + 0.5 * (o_ref[...] + partial_out(h_scr[...]))

    if final_norm:
        @pl.when(j == last)
        def _():
            rows_per_pass = 256

            def finish(r, carry):
                rows = pl.ds(pl.multiple_of(r * rows_per_pass, rows_per_pass), rows_per_pass)
                o_ref[rows, :] = _rms(o_ref[rows, :], gf_ref[...])
                return carry

            lax.fori_loop(0, o_ref.shape[0] // rows_per_pass, finish, 0)


def _ffn(x, g, wg, wu, wd, gf, *, final_norm, tm=1024, tf=512):
    t, d = x.shape
    f = wg.shape[1]
    return pl.pallas_call(
        functools.partial(_ffn_kernel, final_norm=final_norm),
        out_shape=jax.ShapeDtypeStruct((t, d), F32),
        grid=(t // tm, f // tf),
        in_specs=[
            pl.BlockSpec((tm, d), lambda i, j: (i, 0)),
            pl.BlockSpec((1, d), lambda i, j: (0, 0)),
            pl.BlockSpec((d, tf), lambda i, j: (0, j)),
            pl.BlockSpec((d, tf), lambda i, j: (0, j)),
            pl.BlockSpec((tf, d), lambda i, j: (j, 0)),
            pl.BlockSpec((1, d), lambda i, j: (0, 0)),
        ],
        out_specs=pl.BlockSpec((tm, d), lambda i, j: (i, 0)),
        scratch_shapes=[pltpu.VMEM((tm, d), BF16)],
        compiler_params=_params("parallel", "arbitrary"),
        name="ffn",
    )(x, g, wg, wu, wd, gf)


def _inproj_kernel(x_ref, g_ref, w1_ref, w2_ref, tab_ref, gq_ref, gkv_ref, wq_ref, wkv_ref,
                   o1_ref, o4_ref, o16_ref, o2_ref, q_ref, kt_ref, v_ref, y_scr, c4_scr, *, tm):
    h = _rms(x_ref[...], g_ref[...]).astype(BF16)
    y = _dot(h, w1_ref[...])
    o1_ref[0] = y.astype(BF16)
    n4 = tm // 4
    for cb in range(y.shape[1] // LANES):
        cols = slice(cb * LANES, (cb + 1) * LANES)
        y_scr[cb] = y[:, cols]
        for r4 in range(4):
            c4 = y_scr[cb, pl.ds(r4, n4, stride=4), :]
            o4_ref[r4, :, cols] = c4.astype(BF16)
            c4_scr[cb, r4] = c4
            for j in range(4):
                o16_ref[r4 + 4 * j, :, cols] = c4_scr[cb, r4, pl.ds(j, n4 // 4, stride=4), :].astype(BF16)
    rest = _dot(h, w2_ref[...])
    o2_ref[...] = rest[:, 2 * LATENT:4 * LATENT]
    _mla_project(rest[:, :LATENT], rest[:, LATENT:2 * LATENT], rest[:, 4 * LATENT:], tab_ref[...],
                 gq_ref, gkv_ref, wq_ref, wkv_ref, q_ref, kt_ref, v_ref)


def _inproj(x, g, w_qkv, w_rest, tab, gq, gkv, wq, wkv, *, b, s, tm=256):
    t, d = x.shape
    n1, n2 = w_qkv.shape[1], w_rest.shape[1]
    qw = B_HEADS * 2 * HEAD
    per_seq = s // tm
    dils = [dil for _, dil in DILATED_CONFIGS]
    cls_shape = lambda dil: jax.ShapeDtypeStruct((b, dil, s // dil, n1), BF16)
    cls_spec = lambda dil: pl.BlockSpec((None, dil, tm // dil, n1), lambda i: (i // per_seq, 0, i % per_seq, 0))
    resident = dict(pipeline_mode=pl.Buffered(1))
    return pl.pallas_call(
        functools.partial(_inproj_kernel, tm=tm),
        out_shape=(*[cls_shape(dil) for dil in dils], jax.ShapeDtypeStruct((t, 2 * LATENT), F32),
                   jax.ShapeDtypeStruct((t, qw), BF16),
                   jax.ShapeDtypeStruct((B_HEADS, t // tm, 2 * HEAD, tm), BF16),
                   jax.ShapeDtypeStruct((t, B_HEADS * HEAD), BF16)),
        grid=(t // tm,),
        in_specs=[
            pl.BlockSpec((tm, d), lambda i: (i, 0)),
            pl.BlockSpec((1, d), lambda i: (0, 0)),
            pl.BlockSpec((d, n1), lambda i: (0, 0), **resident),
            pl.BlockSpec((d, n2), lambda i: (0, 0), **resident),
            pl.BlockSpec((tm, LANES), lambda i: (i % per_seq, 0)),
            pl.BlockSpec((1, LATENT), lambda i: (0, 0)),
            pl.BlockSpec((1, LATENT), lambda i: (0, 0)),
            pl.BlockSpec((LATENT, qw), lambda i: (0, 0), **resident),
            pl.BlockSpec((LATENT, qw), lambda i: (0, 0), **resident),
        ],
        out_specs=(*[cls_spec(dil) for dil in dils], pl.BlockSpec((tm, 2 * LATENT), lambda i: (i, 0)),
                   pl.BlockSpec((tm, qw), lambda i: (i, 0)),
                   pl.BlockSpec((B_HEADS, None, 2 * HEAD, tm), lambda i: (0, i, 0, 0)),
                   pl.BlockSpec((tm, B_HEADS * HEAD), lambda i: (i, 0))),
        scratch_shapes=[pltpu.VMEM((n1 // LANES, tm, LANES), F32),
                        pltpu.VMEM((n1 // LANES, 4, tm // 4, LANES), F32)],
        compiler_params=_params("parallel"),
        name="inproj",
    )(x, g, w_qkv, w_rest, tab, gq, gkv, wq, wkv)


def _dilated_kernel(q_ref, kp_ref, kc_ref, kn_ref, vp_ref, vc_ref, vn_ref, bias_ref, o_ref, lse_ref,
                    k_scr, v_scr, *, tq, n_cls, dil, rg):
    i = pl.program_id(1)
    scale = HEAD ** -0.5
    sub = 2 * BAND_HALF
    win = sub + 2 * BAND_HALF
    lane = lax.broadcasted_iota(jnp.int32, (sub, LANES), 1)
    scale2 = scale * LOG2E
    ones = jnp.ones((win, HEAD), BF16)
    for rr in range(rg):
        r = pl.program_id(2) * rg + rr
        k_scr[0:BAND_HALF] = kp_ref[rr]
        k_scr[BAND_HALF:BAND_HALF + tq] = kc_ref[rr]
        k_scr[BAND_HALF + tq:] = kn_ref[rr]
        v_scr[0:BAND_HALF] = vp_ref[rr]
        v_scr[BAND_HALF:BAND_HALF + tq] = vc_ref[rr]
        v_scr[BAND_HALF + tq:] = vn_ref[rr]
        for a in range(tq // sub):
            rows = pl.ds(a * sub, sub) if dil == 1 else pl.ds(a * sub * dil + r, sub, stride=dil)
            edge = a == 0 or a == tq // sub - 1
            kidx = i * tq + (a * sub - BAND_HALF) + lax.broadcasted_iota(jnp.int32, (1, win), 1)
            valid = (kidx >= 0) & (kidx < n_cls)
            lse_all = jnp.zeros((sub, LANES), F32)
            for h in range(A_HEADS):
                cols = slice(h * HEAD, (h + 1) * HEAD)
                q = q_ref[rr, a * sub:(a + 1) * sub, cols]
                k = k_scr[a * sub:a * sub + win, cols]
                v = v_scr[a * sub:a * sub + win, cols]
                s2 = _dot_nt(q, k) * scale2 + bias_ref[h]
                if edge:
                    s2 = jnp.where(valid, s2, NEG_INF)
                m2 = jnp.max(s2, axis=1, keepdims=True)
                p = jnp.exp2(s2 - m2).astype(BF16)
                pv = _dot(p, jnp.concatenate([v, ones], axis=1))
                l = pv[:, HEAD:]
                o_ref[h, rows, :] = pv[:, :HEAD] / l
                lse_all = jnp.where(lane == h, m2 * LN2 + jnp.log(l), lse_all)
            lse_ref[rows, :] = lse_all


def _band_bias(dil):
    slopes = 2.0 ** (-8.0 * jnp.arange(1, A_HEADS + 1, dtype=F32) / A_HEADS)
    sub, win = 2 * BAND_HALF, 4 * BAND_HALF
    rel = jnp.abs(BAND_HALF + jnp.arange(sub)[:, None] - jnp.arange(win)[None, :])
    bias = -slopes[:, None, None] * (dil * rel).astype(F32)[None]
    return jnp.where((rel <= BAND_HALF)[None], bias * LOG2E, NEG_INF)


def _dilated_branch(qkv, dil):
    b, _, n_cls, _ = qkv.shape
    queries_per_step = 1024
    tq = min(queries_per_step, n_cls, 4096 // dil)
    nq = n_cls // tq
    nh = tq // BAND_HALF
    last_halo = n_cls // BAND_HALF - 1
    rg = min(dil, queries_per_step // tq)

    def cur(which):
        return pl.BlockSpec((None, rg, tq, A_WIDTH), lambda bi, i, r: (bi, r, i, which))

    def prev(which):
        return pl.BlockSpec((None, rg, BAND_HALF, A_WIDTH),
                            lambda bi, i, r: (bi, r, jnp.maximum(i * nh - 1, 0), which))

    def nxt(which):
        return pl.BlockSpec((None, rg, BAND_HALF, A_WIDTH),
                            lambda bi, i, r: (bi, r, jnp.minimum((i + 1) * nh, last_halo), which))

    t = b * n_cls * dil
    return pl.pallas_call(
        functools.partial(_dilated_kernel, tq=tq, n_cls=n_cls, dil=dil, rg=rg),
        out_shape=(jax.ShapeDtypeStruct((A_HEADS, t, HEAD), F32), jax.ShapeDtypeStruct((t, LANES), F32)),
        grid=(b, nq, dil // rg),
        in_specs=[cur(0), prev(1), cur(1), nxt(1), prev(2), cur(2), nxt(2),
                  pl.BlockSpec((A_HEADS, 2 * BAND_HALF, 4 * BAND_HALF), lambda bi, i, r: (0, 0, 0))],
        out_specs=(pl.BlockSpec((A_HEADS, tq * dil, HEAD), lambda bi, i, r: (0, bi * nq + i, 0)),
                   pl.BlockSpec((tq * dil, LANES), lambda bi, i, r: (bi * nq + i, 0))),
        scratch_shapes=[pltpu.VMEM((tq + 2 * BAND_HALF, A_WIDTH), BF16),
                        pltpu.VMEM((tq + 2 * BAND_HALF, A_WIDTH), BF16)],
        compiler_params=_params("parallel", "parallel", "arbitrary"),
        name=f"dilated{dil}",
    )(qkv, qkv, qkv, qkv, qkv, qkv, qkv, _band_bias(dil))


def _rotate(y, tab):
    z = y * tab
    r = z + pltpu.roll(z, QK_ROPE, axis=1)
    lane = lax.broadcasted_iota(jnp.int32, r.shape, 1)
    return jnp.where(lane < QK_ROPE, r, 0.0)


def _mla_project(c_q, c_kv, k_rope, tab, gq_ref, gkv_ref, wq_ref, wkv_ref, q_ref, kt_ref, v_ref):
    cq = _rms(c_q, gq_ref[...]).astype(BF16)
    ckv = _rms(c_kv, gkv_ref[...]).astype(BF16)
    kr_t = _rotate(k_rope, tab).T.astype(BF16)
    for h in range(B_HEADS):
        lo, mid, hi = 2 * h * HEAD, (2 * h + 1) * HEAD, (2 * h + 2) * HEAD
        q = _dot(cq, wq_ref[:, lo:hi])
        q_ref[:, lo:mid] = q[:, :HEAD].astype(BF16)
        q_ref[:, mid:hi] = _rotate(q[:, HEAD:], tab).astype(BF16)
        kv = _dot(ckv, wkv_ref[:, lo:hi])
        kt_ref[h, :HEAD, :] = kv[:, :HEAD].T.astype(BF16)
        kt_ref[h, HEAD:, :] = kr_t
        v_ref[:, h * HEAD:(h + 1) * HEAD] = kv[:, HEAD:].astype(BF16)


def _mla_attn_kernel(q_ref, kt_ref, v_ref, o_ref, m_scr, a_scr, acc_scr, s_scr, p_scr, mx_scr, *, tq, tk, nk):
    c = (HEAD + QK_ROPE) ** -0.5 * math.log2(math.e)
    q = q_ref[...]
    nl = tk // LANES
    ones = jnp.ones((tk, HEAD), BF16)

    def chunk(kk):
        return pl.ds(pl.multiple_of(kk * tk, tk), tk)

    def scores(kk, slot):
        sub = tk // kt_ref.shape[-1]
        s = jnp.concatenate([_dot(q, kt_ref[sub * kk + u]) for u in range(sub)], axis=1)
        s_scr[slot] = s
        mx = s[:, :LANES]
        for j in range(1, nl):
            mx = jnp.maximum(mx, s[:, j * LANES:(j + 1) * LANES])
        mx_scr[slot] = mx

    def softmax(slot, first):
        s = s_scr[slot]
        m_new = jnp.broadcast_to(jnp.max(mx_scr[slot], axis=1, keepdims=True), (tq, LANES))
        if not first:
            m_old = m_scr[...]
            m_new = jnp.maximum(m_old, m_new)
            a_scr[slot] = jnp.exp2((m_old - m_new) * c)
        for j in range(nl):
            cols = slice(j * LANES, (j + 1) * LANES)
            p_scr[slot, :, cols] = jnp.exp2((s[:, cols] - m_new) * c).astype(BF16)
        m_scr[...] = m_new

    def values(kk, slot, first):
        pv = _dot(p_scr[slot], jnp.concatenate([v_ref[chunk(kk), :], ones], axis=1))
        if first:
            acc_scr[...] = pv
        else:
            alpha = a_scr[slot]
            acc_scr[...] = jnp.concatenate([alpha, alpha], axis=1) * acc_scr[...] + pv

    scores(0, 0)
    softmax(0, True)
    scores(1, 1)
    softmax(1, False)
    values(0, 0, True)
    scores(2, 0)

    pairs = next(n for n in (3, 2, 1) if (nk - 4) % (2 * n) == 0)

    def body(g, carry):
        for u in range(pairs):
            kk = 2 * (pairs * g + u + 1)
            softmax(0, False)
            values(kk - 1, 1, False)
            scores(kk + 1, 1)
            softmax(1, False)
            values(kk, 0, False)
            scores(kk + 2, 0)
        return carry

    lax.fori_loop(0, (nk - 4) // (2 * pairs), body, 0)
    softmax(0, False)
    values(nk - 3, 1, False)
    scores(nk - 1, 1)
    softmax(1, False)
    values(nk - 2, 0, False)
    values(nk - 1, 1, False)
    o_ref[...] = acc_scr[:, :HEAD] / acc_scr[:, HEAD:]


def _mla_attn(q, kt, v, *, b, s, tq=1024):
    t = b * s
    nq = s // tq
    tk = 512
    kw = kt.shape[-1]
    nk = s // tk
    assert nk % 2 == 0 and nk >= 4
    return pl.pallas_call(
        functools.partial(_mla_attn_kernel, tq=tq, tk=tk, nk=nk),
        out_shape=jax.ShapeDtypeStruct((t, B_HEADS * HEAD), F32),
        grid=(b, B_HEADS, nq),
        in_specs=[
            pl.BlockSpec((tq, 2 * HEAD), lambda bi, h, i: (bi * nq + i, h)),
            pl.BlockSpec((None, s // kw, 2 * HEAD, kw), lambda bi, h, i: (h, bi, 0, 0)),
            pl.BlockSpec((s, HEAD), lambda bi, h, i: (bi, h)),
        ],
        out_specs=pl.BlockSpec((tq, HEAD), lambda bi, h, i: (bi * nq + i, h)),
        scratch_shapes=[pltpu.VMEM((tq, LANES), F32), pltpu.VMEM((2, tq, LANES), F32),
                        pltpu.VMEM((tq, 2 * HEAD), F32), pltpu.VMEM((2, tq, tk), F32),
                        pltpu.VMEM((2, tq, tk), BF16), pltpu.VMEM((2, tq, LANES), F32)],
        compiler_params=_params("parallel", "parallel", "arbitrary"),
        name="mla_attn",
    )(q, kt, v)


def _softplus(x):
    return jnp.maximum(x, 0.0) + jnp.log1p(jnp.exp(-jnp.abs(x)))


def _scan_tile(a, b, row, reverse):
    for sh in (1, 2, 4):
        if reverse:
            keep = row < SUBLANES - sh
            a_s = pltpu.roll(a, SUBLANES - sh, axis=0)
            b_s = pltpu.roll(b, SUBLANES - sh, axis=0)
        else:
            keep = row >= sh
            a_s = pltpu.roll(a, sh, axis=0)
            b_s = pltpu.roll(b, sh, axis=0)
        b = b + a * jnp.where(keep, b_s, 0.0)
        a = a * jnp.where(keep, a_s, 1.0)
    return a, b


def _rglru_fwd_kernel(uc_ref, up_ref, un_ref, cw_ref, cb_ref, wg_ref, bg_ref, lam_ref,
                      hf_ref, ab_ref, bb_ref, ubuf, af_scr, bf_scr, carry, *, ts):
    i = pl.program_id(1)
    ns = pl.num_programs(1)
    ubuf[0:SUBLANES] = jnp.where(i > 0, up_ref[...], 0.0)
    ubuf[SUBLANES:SUBLANES + ts] = uc_ref[...]
    ubuf[SUBLANES + ts:] = jnp.where(i < ns - 1, un_ref[...], 0.0)
    u = cb_ref[...] + cw_ref[0:1, :] * ubuf[SUBLANES - 2:SUBLANES - 2 + ts]
    u = u + cw_ref[1:2, :] * ubuf[SUBLANES - 1:SUBLANES - 1 + ts]
    u = u + cw_ref[2:3, :] * ubuf[SUBLANES:SUBLANES + ts]
    u = u + cw_ref[3:4, :] * ubuf[SUBLANES + 1:SUBLANES + 1 + ts]
    gates = _dot(u.astype(BF16), wg_ref[...]) + bg_ref[...]
    sp = _softplus(-lam_ref[...])
    for d in range(2):
        r = jax.nn.sigmoid(gates[:, (2 * d) * C_WIDTH:(2 * d + 1) * C_WIDTH])
        ig = jax.nn.sigmoid(gates[:, (2 * d + 1) * C_WIDTH:(2 * d + 2) * C_WIDTH])
        log_a = -RG_C * r * sp[d:d + 1, :]
        a = jnp.exp(log_a)
        bterm = jnp.sqrt(-jnp.tanh(log_a) * (a * a + 1.0)) * (ig * u)
        if d == 0:
            af_scr[...] = a
            bf_scr[...] = bterm
        else:
            ab_ref[...] = a
            bb_ref[...] = bterm

    @pl.when(i == 0)
    def _():
        carry[...] = jnp.zeros_like(carry)

    row = lax.broadcasted_iota(jnp.int32, (SUBLANES, C_WIDTH), 0)

    def body(t, c):
        rows = pl.ds(pl.multiple_of(t * SUBLANES, SUBLANES), SUBLANES)
        a, b = _scan_tile(af_scr[rows, :], bf_scr[rows, :], row, reverse=False)
        h = b + a * carry[...]
        hf_ref[rows, :] = h
        carry[...] = h[SUBLANES - 1:SUBLANES, :]
        return c

    lax.fori_loop(0, ts // SUBLANES, body, 0, unroll=SCAN_UNROLL)


def _rglru_bwd_kernel(a_ref, b_ref, hf_ref, g_ref, y_ref, carry, *, ts):
    @pl.when(pl.program_id(1) == 0)
    def _():
        carry[...] = jnp.zeros_like(carry)

    row = lax.broadcasted_iota(jnp.int32, (SUBLANES, C_WIDTH), 0)
    nt = ts // SUBLANES

    def body(t, c):
        rows = pl.ds(pl.multiple_of((nt - 1 - t) * SUBLANES, SUBLANES), SUBLANES)
        a, b = _scan_tile(a_ref[rows, :], b_ref[rows, :], row, reverse=True)
        h = b + a * carry[...]
        carry[...] = h[0:1, :]
        y_ref[rows, :] = jax.nn.gelu(g_ref[rows, :]) * (hf_ref[rows, :] + h)
        return c

    lax.fori_loop(0, nt, body, 0, unroll=SCAN_UNROLL)


def _rglru(rest, conv_w, conv_b, w_gates, b_gates, lam, *, b, s, ts=1024):
    t = b * s
    ns = s // ts
    hb = ts // SUBLANES
    last = s // SUBLANES - 1
    u_col, g_col = 0, 1
    tile = lambda bi, i: (bi * ns + i, 0)
    hf, ab, bb = pl.pallas_call(
        functools.partial(_rglru_fwd_kernel, ts=ts),
        out_shape=tuple(jax.ShapeDtypeStruct((t, C_WIDTH), F32) for _ in range(3)),
        grid=(b, ns),
        in_specs=[
            pl.BlockSpec((ts, C_WIDTH), lambda bi, i: (bi * ns + i, u_col)),
            pl.BlockSpec((SUBLANES, C_WIDTH),
                         lambda bi, i: (bi * ns * hb + jnp.maximum(i * hb - 1, 0), u_col)),
            pl.BlockSpec((SUBLANES, C_WIDTH),
                         lambda bi, i: (bi * ns * hb + jnp.minimum((i + 1) * hb, last), u_col)),
            pl.BlockSpec((4, C_WIDTH), lambda bi, i: (0, 0)),
            pl.BlockSpec((1, C_WIDTH), lambda bi, i: (0, 0)),
            pl.BlockSpec((C_WIDTH, 4 * C_WIDTH), lambda bi, i: (0, 0)),
            pl.BlockSpec((1, 4 * C_WIDTH), lambda bi, i: (0, 0)),
            pl.BlockSpec((2, C_WIDTH), lambda bi, i: (0, 0)),
        ],
        out_specs=tuple(pl.BlockSpec((ts, C_WIDTH), tile) for _ in range(3)),
        scratch_shapes=[pltpu.VMEM((ts + 2 * SUBLANES, C_WIDTH), F32),
                        pltpu.VMEM((ts, C_WIDTH), F32), pltpu.VMEM((ts, C_WIDTH), F32),
                        pltpu.VMEM((1, C_WIDTH), F32)],
        compiler_params=_params("parallel", "arbitrary"),
        name="rglru_fwd",
    )(rest, rest, rest, conv_w, conv_b, w_gates, b_gates, lam)
    rev = lambda bi, i: (bi * ns + ns - 1 - i, 0)
    return pl.pallas_call(
        functools.partial(_rglru_bwd_kernel, ts=ts),
        out_shape=jax.ShapeDtypeStruct((t, C_WIDTH), F32),
        grid=(b, ns),
        in_specs=[
            pl.BlockSpec((ts, C_WIDTH), rev),
            pl.BlockSpec((ts, C_WIDTH), rev),
            pl.BlockSpec((ts, C_WIDTH), rev),
            pl.BlockSpec((ts, C_WIDTH), lambda bi, i: (bi * ns + ns - 1 - i, g_col)),
        ],
        out_specs=pl.BlockSpec((ts, C_WIDTH), rev),
        scratch_shapes=[pltpu.VMEM((1, C_WIDTH), F32)],
        compiler_params=_params("parallel", "arbitrary"),
        name="rglru_bwd",
    )(ab, bb, hf, rest)


def _outproj_xattn_kernel(x_ref, o1_ref, o2_ref, o3_ref, l1_ref, l2_ref, l3_ref, yb_ref, yc_ref,
                          ga_ref, gb_ref, gc_ref, w_ref, gx_ref, wxq_ref, kv_ref, wxo_ref, out_ref):
    l1, l2, l3 = l1_ref[...], l2_ref[...], l3_ref[...]
    m = jnp.maximum(jnp.maximum(l1, l2), l3)
    e1, e2, e3 = jnp.exp(l1 - m), jnp.exp(l2 - m), jnp.exp(l3 - m)
    z = e1 + e2 + e3
    w1, w2, w3 = e1 / z, e2 / z, e3 / z
    parts = []
    for h in range(A_HEADS):
        parts.append(w1[:, h:h + 1] * o1_ref[h] + w2[:, h:h + 1] * o2_ref[h] + w3[:, h:h + 1] * o3_ref[h])
    ya = jnp.concatenate(parts, axis=1)
    b0, c0 = A_WIDTH, 2 * A_WIDTH
    y = _dot(_rms(yb_ref[...], gb_ref[...]).astype(BF16), w_ref[b0:c0, :])
    y = y + _dot(_rms(yc_ref[...], gc_ref[...]).astype(BF16), w_ref[c0:, :])
    y = y + _dot(_rms(ya, ga_ref[...]).astype(BF16), w_ref[0:b0, :])
    out_ref[...] = _xattn_block(x_ref[...] + y, gx_ref, wxq_ref, kv_ref, wxo_ref)


def _outproj_xattn(x, o_branches, lse_branches, yb, yc, ga, gb, gc, w, gx, wxq, kv, wxo, *, s, n_mem, tm=512):
    t, d = x.shape
    per_seq = s // tm
    row = lambda n: pl.BlockSpec((tm, n), lambda i: (i, 0))
    full = lambda r, n: pl.BlockSpec((r, n), lambda i: (0, 0))
    heads = pl.BlockSpec((A_HEADS, tm, HEAD), lambda i: (0, i, 0))
    return pl.pallas_call(
        _outproj_xattn_kernel,
        out_shape=jax.ShapeDtypeStruct((t, d), F32),
        grid=(t // tm,),
        in_specs=[row(d), heads, heads, heads, row(LANES), row(LANES), row(LANES),
                  row(A_WIDTH), row(C_WIDTH), full(1, A_WIDTH), full(1, A_WIDTH), full(1, C_WIDTH),
                  full(d, d), full(1, d), full(d, X_WIDTH),
                  pl.BlockSpec((n_mem, 2 * X_WIDTH), lambda i: (i // per_seq, 0)), full(X_WIDTH, d)],
        out_specs=row(d),
        compiler_params=_params("parallel"),
        name="outproj_xattn",
    )(x, *o_branches, *lse_branches, yb, yc, ga, gb, gc, w, gx, wxq, kv, wxo)


def _mem_kv_kernel(m_ref, g_ref, w_ref, o_ref):
    o_ref[...] = _dot(_rms(m_ref[...], g_ref[...]).astype(BF16), w_ref[...]).astype(BF16)


def _mem_kv(mem, g, w_kv):
    t, d = mem.shape
    n = w_kv.shape[1]
    tm = 256
    return pl.pallas_call(
        _mem_kv_kernel,
        out_shape=jax.ShapeDtypeStruct((t, n), BF16),
        grid=(t // tm,),
        in_specs=[pl.BlockSpec((tm, d), lambda i: (i, 0)), pl.BlockSpec((1, d), lambda i: (0, 0)),
                  pl.BlockSpec((d, n), lambda i: (0, 0))],
        out_specs=pl.BlockSpec((tm, n), lambda i: (i, 0)),
        compiler_params=_params("parallel"),
        name="mem_kv",
    )(mem, g, w_kv)


def _xattn_block(x, g_ref, wq_ref, kv_ref, wo_ref):
    q = _dot(_rms(x, g_ref[...]).astype(BF16), wq_ref[...]).astype(BF16)
    c = HEAD ** -0.5 * LOG2E
    ones = jnp.ones((kv_ref.shape[0], HEAD), BF16)
    outs = []
    for h in range(X_HEADS):
        k = kv_ref[:, h * HEAD:(h + 1) * HEAD]
        v = kv_ref[:, X_WIDTH + h * HEAD:X_WIDTH + (h + 1) * HEAD]
        s = _dot_nt(q[:, h * HEAD:(h + 1) * HEAD], k)
        p = jnp.exp2((s - jnp.max(s, axis=1, keepdims=True)) * c).astype(BF16)
        pv = _dot(p, jnp.concatenate([v, ones], axis=1))
        outs.append(pv[:, :HEAD] / pv[:, HEAD:])
    o = jnp.concatenate(outs, axis=1).astype(BF16)
    return x + _dot(o, wo_ref[...])


def _cast_kernel(w_ref, o_ref):
    o_ref[...] = w_ref[...].astype(BF16)


def _layer_to_bf16(w, l):
    _, rows, cols = w.shape
    block_rows = next(c for c in (512, 256, 128, 64, 32, 16) if rows % c == 0 and c * cols * 4 <= (4 << 20))
    return pl.pallas_call(
        _cast_kernel,
        out_shape=jax.ShapeDtypeStruct((rows, cols), BF16),
        grid=(rows // block_rows,),
        in_specs=[pl.BlockSpec((None, block_rows, cols), lambda i: (l, i, 0))],
        out_specs=pl.BlockSpec((block_rows, cols), lambda i: (i, 0)),
        compiler_params=_params("parallel"),
        name="cast_bf16",
    )(w)


def _rope_table(s):
    inv = ROPE_THETA ** (-jnp.arange(0, QK_ROPE, 2, dtype=F32) / QK_ROPE)
    ang = jnp.arange(s, dtype=F32)[:, None] * inv[None, :]
    cos, sin = jnp.cos(ang), jnp.sin(ang)
    return jnp.concatenate([cos, cos, -sin, sin], axis=1)


def _swap_halves(w):
    half = w.shape[-1] // 2
    return jnp.concatenate([w[..., half:], w[..., :half]], axis=-1)


def _prep_layer(p, l):
    row = lambda v: v.reshape(1, -1)
    w_in = p['w_in'][l]
    qkv_end, cq_end, ckv_end, kr_end, u_end = 2304, 2816, 3328, 3392, 3904
    w_rope = w_in[:, ckv_end:kr_end]
    w_rest = jnp.concatenate([w_in[:, qkv_end:ckv_end], w_in[:, kr_end:], w_rope, _swap_halves(w_rope)], axis=1)
    wq = p['w_q_up'][l].reshape(-1, B_HEADS, HEAD + QK_ROPE)
    wq = jnp.concatenate([wq, _swap_halves(wq[..., HEAD:])], axis=-1).reshape(-1, B_HEADS * 2 * HEAD)
    eye = jnp.eye(C_BLOCKS, dtype=F32)
    dense = lambda w: jnp.einsum('ncd,nm->ncmd', w, eye).reshape(C_WIDTH, C_WIDTH)
    w_r, w_i = p['w_rg_r'][l], p['w_rg_i'][l]
    w_gates = jnp.concatenate([dense(w_r[0]), dense(w_i[0]), dense(w_r[1]), dense(w_i[1])], axis=1)
    b_r, b_i = p['b_rg_r'][l], p['b_rg_i'][l]
    b_gates = jnp.concatenate([b_r[0], b_i[0], b_r[1], b_i[1]]).reshape(1, -1)
    bf = lambda w: w.astype(BF16)
    big = lambda name: _layer_to_bf16(p[name], l)
    return dict(
        g_ffn1=row(p['g_ffn1'][l]), w1_gate=big('w1_gate'), w1_up=big('w1_up'), w1_down=big('w1_down'),
        g_mix=row(p['g_mix'][l]), w_qkv=bf(w_in[:, :qkv_end]), w_rest=bf(w_rest),
        g_q_lat=row(p['g_q_lat'][l]), g_kv_lat=row(p['g_kv_lat'][l]), w_q=bf(wq), w_kv=bf(p['w_kv_up'][l]),
        conv_w=p['conv_w'][l], conv_b=row(p['conv_b'][l]), w_gates=bf(w_gates), b_gates=b_gates,
        lam=p['rg_lambda'][l],
        g_out_a=row(p['g_out_a'][l]), g_out_b=row(p['g_out_b'][l]), g_out_c=row(p['g_out_c'][l]),
        w_out=big('w_out'),
        g_xattn=row(p['g_xattn'][l]), g_mem=row(p['g_mem'][l]), w_xq=bf(p['w_xq'][l]),
        w_xkv=bf(jnp.concatenate([p['w_xk'][l], p['w_xv'][l]], axis=1)), w_xo=bf(p['w_xo'][l]),
        g_ffn2=row(p['g_ffn2'][l]), w2_gate=big('w2_gate'), w2_up=big('w2_up'), w2_down=big('w2_down'),
    )


def _trunk(x, mem, layers, g_final):
    b, s, d = x.shape
    n_mem = mem.shape[1]
    x = x.reshape(b * s, d)
    mem = mem.reshape(b * n_mem, d)
    tab = _rope_table(s)
    for l, w in enumerate(layers):
        x = _ffn(x, w['g_ffn1'], w['w1_gate'], w['w1_up'], w['w1_down'], g_final, final_norm=False)
        *qkv_by_dil, rest, q, kt, v = _inproj(x, w['g_mix'], w['w_qkv'], w['w_rest'], tab, w['g_q_lat'],
                                              w['g_kv_lat'], w['w_q'], w['w_kv'], b=b, s=s)
        branches = [_dilated_branch(qkv, dil) for qkv, (_, dil) in zip(qkv_by_dil, DILATED_CONFIGS)]
        yb = _mla_attn(q, kt, v, b=b, s=s)
        yc = _rglru(rest, w['conv_w'], w['conv_b'], w['w_gates'], w['b_gates'], w['lam'], b=b, s=s)
        kv = _mem_kv(mem, w['g_mem'], w['w_xkv'])
        x = _outproj_xattn(x, [o for o, _ in branches], [lse for _, lse in branches], yb, yc,
                           w['g_out_a'], w['g_out_b'], w['g_out_c'], w['w_out'],
                           w['g_xattn'], w['w_xq'], kv, w['w_xo'], s=s, n_mem=n_mem)
        x = _ffn(x, w['g_ffn2'], w['w2_gate'], w['w2_up'], w['w2_down'], g_final,
                 final_norm=(l == len(layers) - 1))
    return x.reshape(b, s, d)


def kernel(x_prompt, x_sample, mem_prompt, mem_sample, g_ffn1, w1_gate, w1_up, w1_down, g_mix, w_in, g_q_lat, w_q_up, g_kv_lat, w_kv_up, conv_w, conv_b, w_rg_r, b_rg_r, w_rg_i, b_rg_i, rg_lambda, g_out_a, g_out_b, g_out_c, w_out, g_xattn, g_mem, w_xq, w_xk, w_xv, w_xo, g_ffn2, w2_gate, w2_up, w2_down, g_final):
    p = dict(g_ffn1=g_ffn1, w1_gate=w1_gate, w1_up=w1_up, w1_down=w1_down,
             g_mix=g_mix, w_in=w_in, g_q_lat=g_q_lat, w_q_up=w_q_up, g_kv_lat=g_kv_lat, w_kv_up=w_kv_up,
             conv_w=conv_w, conv_b=conv_b, w_rg_r=w_rg_r, b_rg_r=b_rg_r, w_rg_i=w_rg_i, b_rg_i=b_rg_i,
             rg_lambda=rg_lambda, g_out_a=g_out_a, g_out_b=g_out_b, g_out_c=g_out_c, w_out=w_out,
             g_xattn=g_xattn, g_mem=g_mem, w_xq=w_xq, w_xk=w_xk, w_xv=w_xv, w_xo=w_xo,
             g_ffn2=g_ffn2, w2_gate=w2_gate, w2_up=w2_up, w2_down=w2_down)
    layers = [_prep_layer(p, l) for l in range(g_ffn1.shape[0])]
    gf = g_final.reshape(1, -1)
    return (_trunk(x_prompt, mem_prompt, layers, gf), _trunk(x_sample, mem_sample, layers, gf))
```

```python
import functools
import math

import jax
import jax.numpy as jnp
from jax import lax
from jax.experimental import pallas as pl
from jax.experimental.pallas import tpu as pltpu

BF16 = jnp.bfloat16
F32 = jnp.float32

A_HEADS = 6
HEAD = 128
A_WIDTH = A_HEADS * HEAD
DILATED_CONFIGS = ((128, 1), (512, 4), (2048, 16))
B_HEADS = 6
QK_ROPE = 64
ROPE_THETA = 10000.0
C_WIDTH = 512
C_BLOCKS = 8
RG_C = 8.0
X_HEADS = 4
X_WIDTH = 512
EPS = 1e-6
NEG_INF = -1e30
LOG2E = math.log2(math.e)
LN2 = math.log(2.0)

V7X_VMEM_LIMIT_BYTES = 56 * 1024 * 1024
LANES = 128
SUBLANES = 8

LATENT = 512
BAND_HALF = 64
SCAN_UNROLL = 4


def _params(*sem):
    return pltpu.CompilerParams(dimension_semantics=sem, vmem_limit_bytes=V7X_VMEM_LIMIT_BYTES)


def _rms(x, g):
    return x * lax.rsqrt(jnp.mean(x * x, axis=-1, keepdims=True) + EPS) * g


def _dot(a, b):
    return jnp.dot(a, b, preferred_element_type=F32)


def _dot_nt(a, b):
    return lax.dot_general(a, b, (((1,), (1,)), ((), ())), preferred_element_type=F32)


def _ffn_kernel(x_ref, g_ref, wg_ref, wu_ref, wd_ref, gf_ref, o_ref, h_scr, *, final_norm):
    j = pl.program_id(1)

    def partial_out(h):
        a = _dot(h, wg_ref[...])
        u = _dot(h, wu_ref[...])
        return _dot((jax.nn.silu(a) * u).astype(BF16), wd_ref[...])

    last = pl.num_programs(1) - 1

    @pl.when(j == 0)
    def _():
        h = _rms(x_ref[...], g_ref[...]).astype(BF16)
        h_scr[...] = h
        o_ref[...] = partial_out(h)

    @pl.when((j > 0) & (j < last))
    def _():
        o_ref[...] += partial_out(h_scr[...])

    @pl.when(j == last)
    def _():
        o_ref[...] = x_ref[...] + 0.5 * (o_ref[...] + partial_out(h_scr[...]))

    if final_norm:
        @pl.when(j == last)
        def _():
            rows_per_pass = 256

            def finish(r, carry):
                rows = pl.ds(pl.multiple_of(r * rows_per_pass, rows_per_pass), rows_per_pass)
                o_ref[rows, :] = _rms(o_ref[rows, :], gf_ref[...])
                return carry

            lax.fori_loop(0, o_ref.shape[0] // rows_per_pass, finish, 0)


def _ffn(x, g, wg, wu, wd, gf, *, final_norm, tm=1024, tf=512):
    t, d = x.shape
    f = wg.shape[1]
    return pl.pallas_call(
        functools.partial(_ffn_kernel, final_norm=final_norm),
        out_shape=jax.ShapeDtypeStruct((t, d), F32),
        grid=(t // tm, f // tf),
        in_specs=[
            pl.BlockSpec((tm, d), lambda i, j: (i, 0)),
            pl.BlockSpec((1, d), lambda i, j: (0, 0)),
            pl.BlockSpec((d, tf), lambda i, j: (0, j)),
            pl.BlockSpec((d, tf), lambda i, j: (0, j)),
            pl.BlockSpec((tf, d), lambda i, j: (j, 0)),
            pl.BlockSpec((1, d), lambda i, j: (0, 0)),
        ],
        out_specs=pl.BlockSpec((tm, d), lambda i, j: (i, 0)),
        scratch_shapes=[pltpu.VMEM((tm, d), BF16)],
        compiler_params=_params("parallel", "arbitrary"),
        name="ffn",
    )(x, g, wg, wu, wd, gf)


def _inproj_kernel(x_ref, g_ref, w1_ref, w2_ref, tab_ref, gq_ref, gkv_ref, wq_ref, wkv_ref,
                   o1_ref, o4_ref, o16_ref, o2_ref, q_ref, kt_ref, v_ref, y_scr, c4_scr, *, tm):
    h = _rms(x_ref[...], g_ref[...]).astype(BF16)
    y = _dot(h, w1_ref[...])
    o1_ref[0] = y.astype(BF16)
    n4 = tm // 4
    for cb in range(y.shape[1] // LANES):
        cols = slice(cb * LANES, (cb + 1) * LANES)
        y_scr[cb] = y[:, cols]
        for r4 in range(4):
            c4 = y_scr[cb, pl.ds(r4, n4, stride=4), :]
            o4_ref[r4, :, cols] = c4.astype(BF16)
            c4_scr[cb, r4] = c4
            for j in range(4):
                o16_ref[r4 + 4 * j, :, cols] = c4_scr[cb, r4, pl.ds(j, n4 // 4, stride=4), :].astype(BF16)
    rest = _dot(h, w2_ref[...])
    o2_ref[...] = rest[:, 2 * LATENT:4 * LATENT]
    _mla_project(rest[:, :LATENT], rest[:, LATENT:2 * LATENT], rest[:, 4 * LATENT:], tab_ref[...],
                 gq_ref, gkv_ref, wq_ref, wkv_ref, q_ref, kt_ref, v_ref)


def _inproj(x, g, w_qkv, w_rest, tab, gq, gkv, wq, wkv, *, b, s, tm=256):
    t, d = x.shape
    n1, n2 = w_qkv.shape[1], w_rest.shape[1]
    qw = B_HEADS * 2 * HEAD
    per_seq = s // tm
    dils = [dil for _, dil in DILATED_CONFIGS]
    cls_shape = lambda dil: jax.ShapeDtypeStruct((b, dil, s // dil, n1), BF16)
    cls_spec = lambda dil: pl.BlockSpec((None, dil, tm // dil, n1), lambda i: (i // per_seq, 0, i % per_seq, 0))
    resident = dict(pipeline_mode=pl.Buffered(1))
    return pl.pallas_call(
        functools.partial(_inproj_kernel, tm=tm),
        out_shape=(*[cls_shape(dil) for dil in dils], jax.ShapeDtypeStruct((t, 2 * LATENT), F32),
                   jax.ShapeDtypeStruct((t, qw), BF16),
                   jax.ShapeDtypeStruct((B_HEADS, t // tm, 2 * HEAD, tm), BF16),
                   jax.ShapeDtypeStruct((t, B_HEADS * HEAD), BF16)),
        grid=(t // tm,),
        in_specs=[
            pl.BlockSpec((tm, d), lambda i: (i, 0)),
            pl.BlockSpec((1, d), lambda i: (0, 0)),
            pl.BlockSpec((d, n1), lambda i: (0, 0), **resident),
            pl.BlockSpec((d, n2), lambda i: (0, 0), **resident),
            pl.BlockSpec((tm, LANES), lambda i: (i % per_seq, 0)),
            pl.BlockSpec((1, LATENT), lambda i: (0, 0)),
            pl.BlockSpec((1, LATENT), lambda i: (0, 0)),
            pl.BlockSpec((LATENT, qw), lambda i: (0, 0), **resident),
            pl.BlockSpec((LATENT, qw), lambda i: (0, 0), **resident),
        ],
        out_specs=(*[cls_spec(dil) for dil in dils], pl.BlockSpec((tm, 2 * LATENT), lambda i: (i, 0)),
                   pl.BlockSpec((tm, qw), lambda i: (i, 0)),
                   pl.BlockSpec((B_HEADS, None, 2 * HEAD, tm), lambda i: (0, i, 0, 0)),
                   pl.BlockSpec((tm, B_HEADS * HEAD), lambda i: (i, 0))),
        scratch_shapes=[pltpu.VMEM((n1 // LANES, tm, LANES), F32),
                        pltpu.VMEM((n1 // LANES, 4, tm // 4, LANES), F32)],
        compiler_params=_params("parallel"),
        name="inproj",
    )(x, g, w_qkv, w_rest, tab, gq, gkv, wq, wkv)


def _dilated_kernel(q_ref, kp_ref, kc_ref, kn_ref, vp_ref, vc_ref, vn_ref, bias_ref, o_ref, lse_ref,
                    k_scr, v_scr, *, tq, n_cls, dil, rg):
    i = pl.program_id(1)
    scale = HEAD ** -0.5
    sub = 2 * BAND_HALF
    win = sub + 2 * BAND_HALF
    lane = lax.broadcasted_iota(jnp.int32, (sub, LANES), 1)
    scale2 = scale * LOG2E
    ones = jnp.ones((win, HEAD), BF16)
    for rr in range(rg):
        r = pl.program_id(2) * rg + rr
        k_scr[0:BAND_HALF] = kp_ref[rr]
        k_scr[BAND_HALF:BAND_HALF + tq] = kc_ref[rr]
        k_scr[BAND_HALF + tq:] = kn_ref[rr]
        v_scr[0:BAND_HALF] = vp_ref[rr]
        v_scr[BAND_HALF:BAND_HALF + tq] = vc_ref[rr]
        v_scr[BAND_HALF + tq:] = vn_ref[rr]
        for a in range(tq // sub):
            rows = pl.ds(a * sub, sub) if dil == 1 else pl.ds(a * sub * dil + r, sub, stride=dil)
            edge = a == 0 or a == tq // sub - 1
            kidx = i * tq + (a * sub - BAND_HALF) + lax.broadcasted_iota(jnp.int32, (1, win), 1)
            valid = (kidx >= 0) & (kidx < n_cls)
            lse_all = jnp.zeros((sub, LANES), F32)
            for h in range(A_HEADS):
                cols = slice(h * HEAD, (h + 1) * HEAD)
                q = q_ref[rr, a * sub:(a + 1) * sub, cols]
                k = k_scr[a * sub:a * sub + win, cols]
                v = v_scr[a * sub:a * sub + win, cols]
                s2 = _dot_nt(q, k) * scale2 + bias_ref[h]
                if edge:
                    s2 = jnp.where(valid, s2, NEG_INF)
                m2 = jnp.max(s2, axis=1, keepdims=True)
                p = jnp.exp2(s2 - m2).astype(BF16)
                pv = _dot(p, jnp.concatenate([v, ones], axis=1))
                l = pv[:, HEAD:]
                o_ref[h, rows, :] = pv[:, :HEAD] / l
                lse_all = jnp.where(lane == h, m2 * LN2 + jnp.log(l), lse_all)
            lse_ref[rows, :] = lse_all


def _band_bias(dil):
    slopes = 2.0 ** (-8.0 * jnp.arange(1, A_HEADS + 1, dtype=F32) / A_HEADS)
    sub, win = 2 * BAND_HALF, 4 * BAND_HALF
    rel = jnp.abs(BAND_HALF + jnp.arange(sub)[:, None] - jnp.arange(win)[None, :])
    bias = -slopes[:, None, None] * (dil * rel).astype(F32)[None]
    return jnp.where((rel <= BAND_HALF)[None], bias * LOG2E, NEG_INF)


def _dilated_branch(qkv, dil):
    b, _, n_cls, _ = qkv.shape
    queries_per_step = 1024
    tq = min(queries_per_step, n_cls, 4096 // dil)
    nq = n_cls // tq
    nh = tq // BAND_HALF
    last_halo = n_cls // BAND_HALF - 1
    rg = min(dil, queries_per_step // tq)

    def cur(which):
        return pl.BlockSpec((None, rg, tq, A_WIDTH), lambda bi, i, r: (bi, r, i, which))

    def prev(which):
        return pl.BlockSpec((None, rg, BAND_HALF, A_WIDTH),
                            lambda bi, i, r: (bi, r, jnp.maximum(i * nh - 1, 0), which))

    def nxt(which):
        return pl.BlockSpec((None, rg, BAND_HALF, A_WIDTH),
                            lambda bi, i, r: (bi, r, jnp.minimum((i + 1) * nh, last_halo), which))

    t = b * n_cls * dil
    return pl.pallas_call(
        functools.partial(_dilated_kernel, tq=tq, n_cls=n_cls, dil=dil, rg=rg),
        out_shape=(jax.ShapeDtypeStruct((A_HEADS, t, HEAD), F32), jax.ShapeDtypeStruct((t, LANES), F32)),
        grid=(b, nq, dil // rg),
        in_specs=[cur(0), prev(1), cur(1), nxt(1), prev(2), cur(2), nxt(2),
                  pl.BlockSpec((A_HEADS, 2 * BAND_HALF, 4 * BAND_HALF), lambda bi, i, r: (0, 0, 0))],
        out_specs=(pl.BlockSpec((A_HEADS, tq * dil, HEAD), lambda bi, i, r: (0, bi * nq + i, 0)),
                   pl.BlockSpec((tq * dil, LANES), lambda bi, i, r: (bi * nq + i, 0))),
        scratch_shapes=[pltpu.VMEM((tq + 2 * BAND_HALF, A_WIDTH), BF16),
                        pltpu.VMEM((tq + 2 * BAND_HALF, A_WIDTH), BF16)],
        compiler_params=_params("parallel", "parallel", "arbitrary"),
        name=f"dilated{dil}",
    )(qkv, qkv, qkv, qkv, qkv, qkv, qkv, _band_bias(dil))


def _rotate(y, tab):
    z = y * tab
    r = z + pltpu.roll(z, QK_ROPE, axis=1)
    lane = lax.broadcasted_iota(jnp.int32, r.shape, 1)
    return jnp.where(lane < QK_ROPE, r, 0.0)


def _mla_project(c_q, c_kv, k_rope, tab, gq_ref, gkv_ref, wq_ref, wkv_ref, q_ref, kt_ref, v_ref):
    cq = _rms(c_q, gq_ref[...]).astype(BF16)
    ckv = _rms(c_kv, gkv_ref[...]).astype(BF16)
    kr_t = _rotate(k_rope, tab).T.astype(BF16)
    for h in range(B_HEADS):
        lo, mid, hi = 2 * h * HEAD, (2 * h + 1) * HEAD, (2 * h + 2) * HEAD
        q = _dot(cq, wq_ref[:, lo:hi])
        q_ref[:, lo:mid] = q[:, :HEAD].astype(BF16)
        q_ref[:, mid:hi] = _rotate(q[:, HEAD:], tab).astype(BF16)
        kv = _dot(ckv, wkv_ref[:, lo:hi])
        kt_ref[h, :HEAD, :] = kv[:, :HEAD].T.astype(BF16)
        kt_ref[h, HEAD:, :] = kr_t
        v_ref[:, h * HEAD:(h + 1) * HEAD] = kv[:, HEAD:].astype(BF16)


def _mla_attn_kernel(q_ref, kt_ref, v_ref, o_ref, m_scr, a_scr, acc_scr, s_scr, p_scr, mx_scr, *, tq, tk, nk):
    c = (HEAD + QK_ROPE) ** -0.5 * math.log2(math.e)
    q = q_ref[...]
    nl = tk // LANES
    ones = jnp.ones((tk, HEAD), BF16)

    def chunk(kk):
        return pl.ds(pl.multiple_of(kk * tk, tk), tk)

    def scores(kk, slot):
        sub = tk // kt_ref.shape[-1]
        s = jnp.concatenate([_dot(q, kt_ref[sub * kk + u]) for u in range(sub)], axis=1)
        s_scr[slot] = s
        mx = s[:, :LANES]
        for j in range(1, nl):
            mx = jnp.maximum(mx, s[:, j * LANES:(j + 1) * LANES])
        mx_scr[slot] = mx

    def softmax(slot, first):
        s = s_scr[slot]
        m_new = jnp.broadcast_to(jnp.max(mx_scr[slot], axis=1, keepdims=True), (tq, LANES))
        if not first:
            m_old = m_scr[...]
            m_new = jnp.maximum(m_old, m_new)
            a_scr[slot] = jnp.exp2((m_old - m_new) * c)
        for j in range(nl):
            cols = slice(j * LANES, (j + 1) * LANES)
            p_scr[slot, :, cols] = jnp.exp2((s[:, cols] - m_new) * c).astype(BF16)
        m_scr[...] = m_new

    def values(kk, slot, first):
        pv = _dot(p_scr[slot], jnp.concatenate([v_ref[chunk(kk), :], ones], axis=1))
        if first:
            acc_scr[...] = pv
        else:
            alpha = a_scr[slot]
            acc_scr[...] = jnp.concatenate([alpha, alpha], axis=1) * acc_scr[...] + pv

    scores(0, 0)
    softmax(0, True)
    scores(1, 1)
    softmax(1, False)
    values(0, 0, True)
    scores(2, 0)

    pairs = next(n for n in (6, 3, 2, 1) if (nk - 4) % (2 * n) == 0)

    def body(g, carry):
        for u in range(pairs):
            kk = 2 * (pairs * g + u + 1)
            softmax(0, False)
            values(kk - 1, 1, False)
            scores(kk + 1, 1)
            softmax(1, False)
            values(kk, 0, False)
            scores(kk + 2, 0)
        return carry

    lax.fori_loop(0, (nk - 4) // (2 * pairs), body, 0)
    softmax(0, False)
    values(nk - 3, 1, False)
    scores(nk - 1, 1)
    softmax(1, False)
    values(nk - 2, 0, False)
    values(nk - 1, 1, False)
    o_ref[...] = acc_scr[:, :HEAD] / acc_scr[:, HEAD:]


def _mla_attn(q, kt, v, *, b, s, tq=1024):
    t = b * s
    nq = s // tq
    tk = 512
    kw = kt.shape[-1]
    nk = s // tk
    assert nk % 2 == 0 and nk >= 4
    return pl.pallas_call(
        functools.partial(_mla_attn_kernel, tq=tq, tk=tk, nk=nk),
        out_shape=jax.ShapeDtypeStruct((t, B_HEADS * HEAD), F32),
        grid=(b, B_HEADS, nq),
        in_specs=[
            pl.BlockSpec((tq, 2 * HEAD), lambda bi, h, i: (bi * nq + i, h)),
            pl.BlockSpec((None, s // kw, 2 * HEAD, kw), lambda bi, h, i: (h, bi, 0, 0)),
            pl.BlockSpec((s, HEAD), lambda bi, h, i: (bi, h)),
        ],
        out_specs=pl.BlockSpec((tq, HEAD), lambda bi, h, i: (bi * nq + i, h)),
        scratch_shapes=[pltpu.VMEM((tq, LANES), F32), pltpu.VMEM((2, tq, LANES), F32),
                        pltpu.VMEM((tq, 2 * HEAD), F32), pltpu.VMEM((2, tq, tk), F32),
                        pltpu.VMEM((2, tq, tk), BF16), pltpu.VMEM((2, tq, LANES), F32)],
        compiler_params=_params("parallel", "parallel", "arbitrary"),
        name="mla_attn",
    )(q, kt, v)


def _softplus(x):
    return jnp.maximum(x, 0.0) + jnp.log1p(jnp.exp(-jnp.abs(x)))


def _scan_tile(a, b, row, reverse):
    for sh in (1, 2, 4):
        if reverse:
            keep = row < SUBLANES - sh
            a_s = pltpu.roll(a, SUBLANES - sh, axis=0)
            b_s = pltpu.roll(b, SUBLANES - sh, axis=0)
        else:
            keep = row >= sh
            a_s = pltpu.roll(a, sh, axis=0)
            b_s = pltpu.roll(b, sh, axis=0)
        b = b + a * jnp.where(keep, b_s, 0.0)
        a = a * jnp.where(keep, a_s, 1.0)
    return a, b


def _rglru_fwd_kernel(uc_ref, up_ref, un_ref, cw_ref, cb_ref, wg_ref, bg_ref, lam_ref,
                      hf_ref, ab_ref, bb_ref, ubuf, af_scr, bf_scr, carry, *, ts):
    i = pl.program_id(1)
    ns = pl.num_programs(1)
    ubuf[0:SUBLANES] = jnp.where(i > 0, up_ref[...], 0.0)
    ubuf[SUBLANES:SUBLANES + ts] = uc_ref[...]
    ubuf[SUBLANES + ts:] = jnp.where(i < ns - 1, un_ref[...], 0.0)
    u = cb_ref[...] + cw_ref[0:1, :] * ubuf[SUBLANES - 2:SUBLANES - 2 + ts]
    u = u + cw_ref[1:2, :] * ubuf[SUBLANES - 1:SUBLANES - 1 + ts]
    u = u + cw_ref[2:3, :] * ubuf[SUBLANES:SUBLANES + ts]
    u = u + cw_ref[3:4, :] * ubuf[SUBLANES + 1:SUBLANES + 1 + ts]
    gates = _dot(u.astype(BF16), wg_ref[...]) + bg_ref[...]
    sp = _softplus(-lam_ref[...])
    for d in range(2):
        r = jax.nn.sigmoid(gates[:, (2 * d) * C_WIDTH:(2 * d + 1) * C_WIDTH])
        ig = jax.nn.sigmoid(gates[:, (2 * d + 1) * C_WIDTH:(2 * d + 2) * C_WIDTH])
        log_a = -RG_C * r * sp[d:d + 1, :]
        a = jnp.exp(log_a)
        bterm = jnp.sqrt(-jnp.tanh(log_a) * (a * a + 1.0)) * (ig * u)
        if d == 0:
            af_scr[...] = a
            bf_scr[...] = bterm
        else:
            ab_ref[...] = a
            bb_ref[...] = bterm

    @pl.when(i == 0)
    def _():
        carry[...] = jnp.zeros_like(carry)

    row = lax.broadcasted_iota(jnp.int32, (SUBLANES, C_WIDTH), 0)

    def body(t, c):
        rows = pl.ds(pl.multiple_of(t * SUBLANES, SUBLANES), SUBLANES)
        a, b = _scan_tile(af_scr[rows, :], bf_scr[rows, :], row, reverse=False)
        h = b + a * carry[...]
        hf_ref[rows, :] = h
        carry[...] = h[SUBLANES - 1:SUBLANES, :]
        return c

    lax.fori_loop(0, ts // SUBLANES, body, 0, unroll=SCAN_UNROLL)


def _rglru_bwd_kernel(a_ref, b_ref, hf_ref, g_ref, y_ref, carry, *, ts):
    @pl.when(pl.program_id(1) == 0)
    def _():
        carry[...] = jnp.zeros_like(carry)

    row = lax.broadcasted_iota(jnp.int32, (SUBLANES, C_WIDTH), 0)
    nt = ts // SUBLANES

    def body(t, c):
        rows = pl.ds(pl.multiple_of((nt - 1 - t) * SUBLANES, SUBLANES), SUBLANES)
        a, b = _scan_tile(a_ref[rows, :], b_ref[rows, :], row, reverse=True)
        h = b + a * carry[...]
        carry[...] = h[0:1, :]
        y_ref[rows, :] = jax.nn.gelu(g_ref[rows, :]) * (hf_ref[rows, :] + h)
        return c

    lax.fori_loop(0, nt, body, 0, unroll=SCAN_UNROLL)


def _rglru(rest, conv_w, conv_b, w_gates, b_gates, lam, *, b, s, ts=1024):
    t = b * s
    ns = s // ts
    hb = ts // SUBLANES
    last = s // SUBLANES - 1
    u_col, g_col = 0, 1
    tile = lambda bi, i: (bi * ns + i, 0)
    hf, ab, bb = pl.pallas_call(
        functools.partial(_rglru_fwd_kernel, ts=ts),
        out_shape=tuple(jax.ShapeDtypeStruct((t, C_WIDTH), F32) for _ in range(3)),
        grid=(b, ns),
        in_specs=[
            pl.BlockSpec((ts, C_WIDTH), lambda bi, i: (bi * ns + i, u_col)),
            pl.BlockSpec((SUBLANES, C_WIDTH),
                         lambda bi, i: (bi * ns * hb + jnp.maximum(i * hb - 1, 0), u_col)),
            pl.BlockSpec((SUBLANES, C_WIDTH),
                         lambda bi, i: (bi * ns * hb + jnp.minimum((i + 1) * hb, last), u_col)),
            pl.BlockSpec((4, C_WIDTH), lambda bi, i: (0, 0)),
            pl.BlockSpec((1, C_WIDTH), lambda bi, i: (0, 0)),
            pl.BlockSpec((C_WIDTH, 4 * C_WIDTH), lambda bi, i: (0, 0)),
            pl.BlockSpec((1, 4 * C_WIDTH), lambda bi, i: (0, 0)),
            pl.BlockSpec((2, C_WIDTH), lambda bi, i: (0, 0)),
        ],
        out_specs=tuple(pl.BlockSpec((ts, C_WIDTH), tile) for _ in range(3)),
        scratch_shapes=[pltpu.VMEM((ts + 2 * SUBLANES, C_WIDTH), F32),
                        pltpu.VMEM((ts, C_WIDTH), F32), pltpu.VMEM((ts, C_WIDTH), F32),
                        pltpu.VMEM((1, C_WIDTH), F32)],
        compiler_params=_params("parallel", "arbitrary"),
        name="rglru_fwd",
    )(rest, rest, rest, conv_w, conv_b, w_gates, b_gates, lam)
    rev = lambda bi, i: (bi * ns + ns - 1 - i, 0)
    return pl.pallas_call(
        functools.partial(_rglru_bwd_kernel, ts=ts),
        out_shape=jax.ShapeDtypeStruct((t, C_WIDTH), F32),
        grid=(b, ns),
        in_specs=[
            pl.BlockSpec((ts, C_WIDTH), rev),
            pl.BlockSpec((ts, C_WIDTH), rev),
            pl.BlockSpec((ts, C_WIDTH), rev),
            pl.BlockSpec((ts, C_WIDTH), lambda bi, i: (bi * ns + ns - 1 - i, g_col)),
        ],
        out_specs=pl.BlockSpec((ts, C_WIDTH), rev),
        scratch_shapes=[pltpu.VMEM((1, C_WIDTH), F32)],
        compiler_params=_params("parallel", "arbitrary"),
        name="rglru_bwd",
    )(ab, bb, hf, rest)


def _outproj_xattn_kernel(x_ref, o1_ref, o2_ref, o3_ref, l1_ref, l2_ref, l3_ref, yb_ref, yc_ref,
                          ga_ref, gb_ref, gc_ref, w_ref, gx_ref, wxq_ref, kv_ref, wxo_ref, out_ref):
    l1, l2, l3 = l1_ref[...], l2_ref[...], l3_ref[...]
    m = jnp.maximum(jnp.maximum(l1, l2), l3)
    e1, e2, e3 = jnp.exp(l1 - m), jnp.exp(l2 - m), jnp.exp(l3 - m)
    z = e1 + e2 + e3
    w1, w2, w3 = e1 / z, e2 / z, e3 / z
    parts = []
    for h in range(A_HEADS):
        parts.append(w1[:, h:h + 1] * o1_ref[h] + w2[:, h:h + 1] * o2_ref[h] + w3[:, h:h + 1] * o3_ref[h])
    ya = jnp.concatenate(parts, axis=1)
    b0, c0 = A_WIDTH, 2 * A_WIDTH
    y = _dot(_rms(yb_ref[...], gb_ref[...]).astype(BF16), w_ref[b0:c0, :])
    y = y + _dot(_rms(yc_ref[...], gc_ref[...]).astype(BF16), w_ref[c0:, :])
    y = y + _dot(_rms(ya, ga_ref[...]).astype(BF16), w_ref[0:b0, :])
    out_ref[...] = _xattn_block(x_ref[...] + y, gx_ref, wxq_ref, kv_ref, wxo_ref)


def _outproj_xattn(x, o_branches, lse_branches, yb, yc, ga, gb, gc, w, gx, wxq, kv, wxo, *, s, n_mem, tm=512):
    t, d = x.shape
    per_seq = s // tm
    row = lambda n: pl.BlockSpec((tm, n), lambda i: (i, 0))
    full = lambda r, n: pl.BlockSpec((r, n), lambda i: (0, 0))
    heads = pl.BlockSpec((A_HEADS, tm, HEAD), lambda i: (0, i, 0))
    return pl.pallas_call(
        _outproj_xattn_kernel,
        out_shape=jax.ShapeDtypeStruct((t, d), F32),
        grid=(t // tm,),
        in_specs=[row(d), heads, heads, heads, row(LANES), row(LANES), row(LANES),
                  row(A_WIDTH), row(C_WIDTH), full(1, A_WIDTH), full(1, A_WIDTH), full(1, C_WIDTH),
                  full(d, d), full(1, d), full(d, X_WIDTH),
                  pl.BlockSpec((n_mem, 2 * X_WIDTH), lambda i: (i // per_seq, 0)), full(X_WIDTH, d)],
        out_specs=row(d),
        compiler_params=_params("parallel"),
        name="outproj_xattn",
    )(x, *o_branches, *lse_branches, yb, yc, ga, gb, gc, w, gx, wxq, kv, wxo)


def _mem_kv_kernel(m_ref, g_ref, w_ref, o_ref):
    o_ref[...] = _dot(_rms(m_ref[...], g_ref[...]).astype(BF16), w_ref[...]).astype(BF16)


def _mem_kv(mem, g, w_kv):
    t, d = mem.shape
    n = w_kv.shape[1]
    tm = 256
    return pl.pallas_call(
        _mem_kv_kernel,
        out_shape=jax.ShapeDtypeStruct((t, n), BF16),
        grid=(t // tm,),
        in_specs=[pl.BlockSpec((tm, d), lambda i: (i, 0)), pl.BlockSpec((1, d), lambda i: (0, 0)),
                  pl.BlockSpec((d, n), lambda i: (0, 0))],
        out_specs=pl.BlockSpec((tm, n), lambda i: (i, 0)),
        compiler_params=_params("parallel"),
        name="mem_kv",
    )(mem, g, w_kv)


def _xattn_block(x, g_ref, wq_ref, kv_ref, wo_ref):
    q = _dot(_rms(x, g_ref[...]).astype(BF16), wq_ref[...]).astype(BF16)
    c = HEAD ** -0.5 * LOG2E
    ones = jnp.ones((kv_ref.shape[0], HEAD), BF16)
    outs = []
    for h in range(X_HEADS):
        k = kv_ref[:, h * HEAD:(h + 1) * HEAD]
        v = kv_ref[:, X_WIDTH + h * HEAD:X_WIDTH + (h + 1) * HEAD]
        s = _dot_nt(q[:, h * HEAD:(h + 1) * HEAD], k)
        p = jnp.exp2((s - jnp.max(s, axis=1, keepdims=True)) * c).astype(BF16)
        pv = _dot(p, jnp.concatenate([v, ones], axis=1))
        outs.append(pv[:, :HEAD] / pv[:, HEAD:])
    o = jnp.concatenate(outs, axis=1).astype(BF16)
    return x + _dot(o, wo_ref[...])


def _cast_kernel(w_ref, o_ref):
    o_ref[...] = w_ref[...].astype(BF16)


def _layer_to_bf16(w, l):
    _, rows, cols = w.shape
    block_rows = next(c for c in (512, 256, 128, 64, 32, 16) if rows % c == 0 and c * cols * 4 <= (4 << 20))
    return pl.pallas_call(
        _cast_kernel,
        out_shape=jax.ShapeDtypeStruct((rows, cols), BF16),
        grid=(rows // block_rows,),
        in_specs=[pl.BlockSpec((None, block_rows, cols), lambda i: (l, i, 0))],
        out_specs=pl.BlockSpec((block_rows, cols), lambda i: (i, 0)),
        compiler_params=_params("parallel"),
        name="cast_bf16",
    )(w)


def _rope_table(s):
    inv = ROPE_THETA ** (-jnp.arange(0, QK_ROPE, 2, dtype=F32) / QK_ROPE)
    ang = jnp.arange(s, dtype=F32)[:, None] * inv[None, :]
    cos, sin = jnp.cos(ang), jnp.sin(ang)
    return jnp.concatenate([cos, cos, -sin, sin], axis=1)


def _swap_halves(w):
    half = w.shape[-1] // 2
    return jnp.concatenate([w[..., half:], w[..., :half]], axis=-1)


def _prep_layer(p, l):
    row = lambda v: v.reshape(1, -1)
    w_in = p['w_in'][l]
    qkv_end, cq_end, ckv_end, kr_end, u_end = 2304, 2816, 3328, 3392, 3904
    w_rope = w_in[:, ckv_end:kr_end]
    w_rest = jnp.concatenate([w_in[:, qkv_end:ckv_end], w_in[:, kr_end:], w_rope, _swap_halves(w_rope)], axis=1)
    wq = p['w_q_up'][l].reshape(-1, B_HEADS, HEAD + QK_ROPE)
    wq = jnp.concatenate([wq, _swap_halves(wq[..., HEAD:])], axis=-1).reshape(-1, B_HEADS * 2 * HEAD)
    eye = jnp.eye(C_BLOCKS, dtype=F32)
    dense = lambda w: jnp.einsum('ncd,nm->ncmd', w, eye).reshape(C_WIDTH, C_WIDTH)
    w_r, w_i = p['w_rg_r'][l], p['w_rg_i'][l]
    w_gates = jnp.concatenate([dense(w_r[0]), dense(w_i[0]), dense(w_r[1]), dense(w_i[1])], axis=1)
    b_r, b_i = p['b_rg_r'][l], p['b_rg_i'][l]
    b_gates = jnp.concatenate([b_r[0], b_i[0], b_r[1], b_i[1]]).reshape(1, -1)
    bf = lambda w: w.astype(BF16)
    big = lambda name: _layer_to_bf16(p[name], l)
    return dict(
        g_ffn1=row(p['g_ffn1'][l]), w1_gate=big('w1_gate'), w1_up=big('w1_up'), w1_down=big('w1_down'),
        g_mix=row(p['g_mix'][l]), w_qkv=bf(w_in[:, :qkv_end]), w_rest=bf(w_rest),
        g_q_lat=row(p['g_q_lat'][l]), g_kv_lat=row(p['g_kv_lat'][l]), w_q=bf(wq), w_kv=bf(p['w_kv_up'][l]),
        conv_w=p['conv_w'][l], conv_b=row(p['conv_b'][l]), w_gates=bf(w_gates), b_gates=b_gates,
        lam=p['rg_lambda'][l],
        g_out_a=row(p['g_out_a'][l]), g_out_b=row(p['g_out_b'][l]), g_out_c=row(p['g_out_c'][l]),
        w_out=big('w_out'),
        g_xattn=row(p['g_xattn'][l]), g_mem=row(p['g_mem'][l]), w_xq=bf(p['w_xq'][l]),
        w_xkv=bf(jnp.concatenate([p['w_xk'][l], p['w_xv'][l]], axis=1)), w_xo=bf(p['w_xo'][l]),
        g_ffn2=row(p['g_ffn2'][l]), w2_gate=big('w2_gate'), w2_up=big('w2_up'), w2_down=big('w2_down'),
    )


def _trunk(x, mem, layers, g_final):
    b, s, d = x.shape
    n_mem = mem.shape[1]
    x = x.reshape(b * s, d)
    mem = mem.reshape(b * n_mem, d)
    tab = _rope_table(s)
    for l, w in enumerate(layers):
        x = _ffn(x, w['g_ffn1'], w['w1_gate'], w['w1_up'], w['w1_down'], g_final, final_norm=False)
        *qkv_by_dil, rest, q, kt, v = _inproj(x, w['g_mix'], w['w_qkv'], w['w_rest'], tab, w['g_q_lat'],
                                              w['g_kv_lat'], w['w_q'], w['w_kv'], b=b, s=s)
        branches = [_dilated_branch(qkv, dil) for qkv, (_, dil) in zip(qkv_by_dil, DILATED_CONFIGS)]
        yb = _mla_attn(q, kt, v, b=b, s=s)
        yc = _rglru(rest, w['conv_w'], w['conv_b'], w['w_gates'], w['b_gates'], w['lam'], b=b, s=s)
        kv = _mem_kv(mem, w['g_mem'], w['w_xkv'])
        x = _outproj_xattn(x, [o for o, _ in branches], [lse for _, lse in branches], yb, yc,
                           w['g_out_a'], w['g_out_b'], w['g_out_c'], w['w_out'],
                           w['g_xattn'], w['w_xq'], kv, w['w_xo'], s=s, n_mem=n_mem)
        x = _ffn(x, w['g_ffn2'], w['w2_gate'], w['w2_up'], w['w2_down'], g_final,
                 final_norm=(l == len(layers) - 1))
    return x.reshape(b, s, d)


def kernel(x_prompt, x_sample, mem_prompt, mem_sample, g_ffn1, w1_gate, w1_up, w1_down, g_mix, w_in, g_q_lat, w_q_up, g_kv_lat, w_kv_up, conv_w, conv_b, w_rg_r, b_rg_r, w_rg_i, b_rg_i, rg_lambda, g_out_a, g_out_b, g_out_c, w_out, g_xattn, g_mem, w_xq, w_xk, w_xv, w_xo, g_ffn2, w2_gate, w2_up, w2_down, g_final):
    p = dict(g_ffn1=g_ffn1, w1_gate=w1_gate, w1_up=w1_up, w1_down=w1_down,
             g_mix=g_mix, w_in=w_in, g_q_lat=g_q_lat, w_q_up=w_q_up, g_kv_lat=g_kv_lat, w_kv_up=w_kv_up,
             conv_w=conv_w, conv_b=conv_b, w_rg_r=w_rg_r, b_rg_r=b_rg_r, w_rg_i=w_rg_i, b_rg_i=b_rg_i,
             rg_lambda=rg_lambda, g_out_a=g_out_a, g_out_b=g_out_b, g_out_c=g_out_c, w_out=w_out,
             g_xattn=g_xattn, g_mem=g_mem, w_xq=w_xq, w_xk=w_xk, w_xv=w_xv, w_xo=w_xo,
             g_ffn2=g_ffn2, w2_gate=w2_gate, w2_up=w2_up, w2_down=w2_down)
    layers = [_prep_layer(p, l) for l in range(g_ffn1.shape[0])]
    gf = g_final.reshape(1, -1)
    return (_trunk(x_prompt, mem_prompt, layers, gf), _trunk(x_sample, mem_sample, layers, gf))
```

```python
import functools
import math

import jax
import jax.numpy as jnp
from jax import lax
from jax.experimental import pallas as pl
from jax.experimental.pallas import tpu as pltpu

BF16 = jnp.bfloat16
F32 = jnp.float32

A_HEADS = 6
HEAD = 128
A_WIDTH = A_HEADS * HEAD
DILATED_CONFIGS = ((128, 1), (512, 4), (2048, 16))
B_HEADS = 6
QK_ROPE = 64
ROPE_THETA = 10000.0
C_WIDTH = 512
C_BLOCKS = 8
RG_C = 8.0
X_HEADS = 4
X_WIDTH = 512
EPS = 1e-6
NEG_INF = -1e30
LOG2E = math.log2(math.e)
LN2 = math.log(2.0)

V7X_VMEM_LIMIT_BYTES = 56 * 1024 * 1024
LANES = 128
SUBLANES = 8

LATENT = 512
BAND_HALF = 64
SCAN_UNROLL = 4


def _params(*sem):
    return pltpu.CompilerParams(dimension_semantics=sem, vmem_limit_bytes=V7X_VMEM_LIMIT_BYTES)


def _rms(x, g):
    return x * lax.rsqrt(jnp.mean(x * x, axis=-1, keepdims=True) + EPS) * g


def _dot(a, b):
    return jnp.dot(a, b, preferred_element_type=F32)


def _dot_nt(a, b):
    return lax.dot_general(a, b, (((1,), (1,)), ((), ())), preferred_element_type=F32)


def _ffn_kernel(x_ref, g_ref, wg_ref, wu_ref, wd_ref, gf_ref, o_ref, h_scr, *, final_norm):
    j = pl.program_id(1)

    def partial_out(h):
        a = _dot(h, wg_ref[...])
        u = _dot(h, wu_ref[...])
        return _dot((jax.nn.silu(a) * u).astype(BF16), wd_ref[...])

    last = pl.num_programs(1) - 1

    @pl.when(j == 0)
    def _():
        h = _rms(x_ref[...], g_ref[...]).astype(BF16)
        h_scr[...] = h
        o_ref[...] = partial_out(h)

    @pl.when((j > 0) & (j < last))
    def _():
        o_ref[...] += partial_out(h_scr[...])

    @pl.when(j == last)
    def _():
        o_ref[...] = x_ref[...] + 0.5 * (o_ref[...] + partial_out(h_scr[...]))

    if final_norm:
        @pl.when(j == last)
        def _():
            rows_per_pass = 256

            def finish(r, carry):
                rows = pl.ds(pl.multiple_of(r * rows_per_pass, rows_per_pass), rows_per_pass)
                o_ref[rows, :] = _rms(o_ref[rows, :], gf_ref[...])
                return carry

            lax.fori_loop(0, o_ref.shape[0] // rows_per_pass, finish, 0)


def _ffn(x, g, wg, wu, wd, gf, *, final_norm, tm=1024, tf=512):
    t, d = x.shape
    f = wg.shape[1]
    return pl.pallas_call(
        functools.partial(_ffn_kernel, final_norm=final_norm),
        out_shape=jax.ShapeDtypeStruct((t, d), F32),
        grid=(t // tm, f // tf),
        in_specs=[
            pl.BlockSpec((tm, d), lambda i, j: (i, 0)),
            pl.BlockSpec((1, d), lambda i, j: (0, 0)),
            pl.BlockSpec((d, tf), lambda i, j: (0, j)),
            pl.BlockSpec((d, tf), lambda i, j: (0, j)),
            pl.BlockSpec((tf, d), lambda i, j: (j, 0)),
            pl.BlockSpec((1, d), lambda i, j: (0, 0)),
        ],
        out_specs=pl.BlockSpec((tm, d), lambda i, j: (i, 0)),
        scratch_shapes=[pltpu.VMEM((tm, d), BF16)],
        compiler_params=_params("parallel", "arbitrary"),
        name="ffn",
    )(x, g, wg, wu, wd, gf)


def _inproj_kernel(x_ref, g_ref, w1_ref, w2_ref, tab_ref, gq_ref, gkv_ref, wq_ref, wkv_ref,
                   o1_ref, o4_ref, o16_ref, o2_ref, q_ref, kt_ref, v_ref, y_scr, c4_scr, *, tm):
    h = _rms(x_ref[...], g_ref[...]).astype(BF16)
    y = _dot(h, w1_ref[...])
    o1_ref[0] = y.astype(BF16)
    n4 = tm // 4
    for cb in range(y.shape[1] // LANES):
        cols = slice(cb * LANES, (cb + 1) * LANES)
        y_scr[cb] = y[:, cols]
        for r4 in range(4):
            c4 = y_scr[cb, pl.ds(r4, n4, stride=4), :]
            o4_ref[r4, :, cols] = c4.astype(BF16)
            c4_scr[cb, r4] = c4
            for j in range(4):
                o16_ref[r4 + 4 * j, :, cols] = c4_scr[cb, r4, pl.ds(j, n4 // 4, stride=4), :].astype(BF16)
    rest = _dot(h, w2_ref[...])
    o2_ref[...] = rest[:, 2 * LATENT:4 * LATENT]
    _mla_project(rest[:, :LATENT], rest[:, LATENT:2 * LATENT], rest[:, 4 * LATENT:], tab_ref[...],
                 gq_ref, gkv_ref, wq_ref, wkv_ref, q_ref, kt_ref, v_ref)


def _inproj(x, g, w_qkv, w_rest, tab, gq, gkv, wq, wkv, *, b, s, tm=256):
    t, d = x.shape
    n1, n2 = w_qkv.shape[1], w_rest.shape[1]
    qw = B_HEADS * 2 * HEAD
    per_seq = s // tm
    dils = [dil for _, dil in DILATED_CONFIGS]
    cls_shape = lambda dil: jax.ShapeDtypeStruct((b, dil, s // dil, n1), BF16)
    cls_spec = lambda dil: pl.BlockSpec((None, dil, tm // dil, n1), lambda i: (i // per_seq, 0, i % per_seq, 0))
    resident = dict(pipeline_mode=pl.Buffered(1))
    return pl.pallas_call(
        functools.partial(_inproj_kernel, tm=tm),
        out_shape=(*[cls_shape(dil) for dil in dils], jax.ShapeDtypeStruct((t, 2 * LATENT), F32),
                   jax.ShapeDtypeStruct((t, qw), BF16),
                   jax.ShapeDtypeStruct((B_HEADS, t // tm, 2 * HEAD, tm), BF16),
                   jax.ShapeDtypeStruct((t, B_HEADS * HEAD), BF16)),
        grid=(t // tm,),
        in_specs=[
            pl.BlockSpec((tm, d), lambda i: (i, 0)),
            pl.BlockSpec((1, d), lambda i: (0, 0)),
            pl.BlockSpec((d, n1), lambda i: (0, 0), **resident),
            pl.BlockSpec((d, n2), lambda i: (0, 0), **resident),
            pl.BlockSpec((tm, LANES), lambda i: (i % per_seq, 0)),
            pl.BlockSpec((1, LATENT), lambda i: (0, 0)),
            pl.BlockSpec((1, LATENT), lambda i: (0, 0)),
            pl.BlockSpec((LATENT, qw), lambda i: (0, 0), **resident),
            pl.BlockSpec((LATENT, qw), lambda i: (0, 0), **resident),
        ],
        out_specs=(*[cls_spec(dil) for dil in dils], pl.BlockSpec((tm, 2 * LATENT), lambda i: (i, 0)),
                   pl.BlockSpec((tm, qw), lambda i: (i, 0)),
                   pl.BlockSpec((B_HEADS, None, 2 * HEAD, tm), lambda i: (0, i, 0, 0)),
                   pl.BlockSpec((tm, B_HEADS * HEAD), lambda i: (i, 0))),
        scratch_shapes=[pltpu.VMEM((n1 // LANES, tm, LANES), F32),
                        pltpu.VMEM((n1 // LANES, 4, tm // 4, LANES), F32)],
        compiler_params=_params("parallel"),
        name="inproj",
    )(x, g, w_qkv, w_rest, tab, gq, gkv, wq, wkv)


def _dilated_kernel(q_ref, kp_ref, kc_ref, kn_ref, vp_ref, vc_ref, vn_ref, bias_ref, o_ref, lse_ref,
                    k_scr, v_scr, *, tq, n_cls, dil, rg):
    i = pl.program_id(1)
    scale = HEAD ** -0.5
    sub = 2 * BAND_HALF
    win = sub + 2 * BAND_HALF
    lane = lax.broadcasted_iota(jnp.int32, (sub, LANES), 1)
    scale2 = scale * LOG2E
    ones = jnp.ones((win, HEAD), BF16)
    for rr in range(rg):
        r = pl.program_id(2) * rg + rr
        k_scr[0:BAND_HALF] = kp_ref[rr]
        k_scr[BAND_HALF:BAND_HALF + tq] = kc_ref[rr]
        k_scr[BAND_HALF + tq:] = kn_ref[rr]
        v_scr[0:BAND_HALF] = vp_ref[rr]
        v_scr[BAND_HALF:BAND_HALF + tq] = vc_ref[rr]
        v_scr[BAND_HALF + tq:] = vn_ref[rr]
        for a in range(tq // sub):
            rows = pl.ds(a * sub, sub) if dil == 1 else pl.ds(a * sub * dil + r, sub, stride=dil)
            edge = a == 0 or a == tq // sub - 1
            kidx = i * tq + (a * sub - BAND_HALF) + lax.broadcasted_iota(jnp.int32, (1, win), 1)
            valid = (kidx >= 0) & (kidx < n_cls)
            lse_all = jnp.zeros((sub, LANES), F32)
            for h in range(A_HEADS):
                cols = slice(h * HEAD, (h + 1) * HEAD)
                q = q_ref[rr, a * sub:(a + 1) * sub, cols]
                k = k_scr[a * sub:a * sub + win, cols]
                v = v_scr[a * sub:a * sub + win, cols]
                s2 = _dot_nt(q, k) * scale2 + bias_ref[h]
                if edge:
                    s2 = jnp.where(valid, s2, NEG_INF)
                m2 = jnp.max(s2, axis=1, keepdims=True)
                p = jnp.exp2(s2 - m2).astype(BF16)
                pv = _dot(p, jnp.concatenate([v, ones], axis=1))
                l = pv[:, HEAD:]
                o_ref[h, rows, :] = pv[:, :HEAD] / l
                lse_all = jnp.where(lane == h, m2 * LN2 + jnp.log(l), lse_all)
            lse_ref[rows, :] = lse_all


def _band_bias(dil):
    slopes = 2.0 ** (-8.0 * jnp.arange(1, A_HEADS + 1, dtype=F32) / A_HEADS)
    sub, win = 2 * BAND_HALF, 4 * BAND_HALF
    rel = jnp.abs(BAND_HALF + jnp.arange(sub)[:, None] - jnp.arange(win)[None, :])
    bias = -slopes[:, None, None] * (dil * rel).astype(F32)[None]
    return jnp.where((rel <= BAND_HALF)[None], bias * LOG2E, NEG_INF)


def _dilated_branch(qkv, dil):
    b, _, n_cls, _ = qkv.shape
    queries_per_step = 1024
    tq = min(queries_per_step, n_cls, 4096 // dil)
    nq = n_cls // tq
    nh = tq // BAND_HALF
    last_halo = n_cls // BAND_HALF - 1
    rg = min(dil, queries_per_step // tq)

    def cur(which):
        return pl.BlockSpec((None, rg, tq, A_WIDTH), lambda bi, i, r: (bi, r, i, which))

    def prev(which):
        return pl.BlockSpec((None, rg, BAND_HALF, A_WIDTH),
                            lambda bi, i, r: (bi, r, jnp.maximum(i * nh - 1, 0), which))

    def nxt(which):
        return pl.BlockSpec((None, rg, BAND_HALF, A_WIDTH),
                            lambda bi, i, r: (bi, r, jnp.minimum((i + 1) * nh, last_halo), which))

    t = b * n_cls * dil
    return pl.pallas_call(
        functools.partial(_dilated_kernel, tq=tq, n_cls=n_cls, dil=dil, rg=rg),
        out_shape=(jax.ShapeDtypeStruct((A_HEADS, t, HEAD), F32), jax.ShapeDtypeStruct((t, LANES), F32)),
        grid=(b, nq, dil // rg),
        in_specs=[cur(0), prev(1), cur(1), nxt(1), prev(2), cur(2), nxt(2),
                  pl.BlockSpec((A_HEADS, 2 * BAND_HALF, 4 * BAND_HALF), lambda bi, i, r: (0, 0, 0))],
        out_specs=(pl.BlockSpec((A_HEADS, tq * dil, HEAD), lambda bi, i, r: (0, bi * nq + i, 0)),
                   pl.BlockSpec((tq * dil, LANES), lambda bi, i, r: (bi * nq + i, 0))),
        scratch_shapes=[pltpu.VMEM((tq + 2 * BAND_HALF, A_WIDTH), BF16),
                        pltpu.VMEM((tq + 2 * BAND_HALF, A_WIDTH), BF16)],
        compiler_params=_params("parallel", "parallel", "arbitrary"),
        name=f"dilated{dil}",
    )(qkv, qkv, qkv, qkv, qkv, qkv, qkv, _band_bias(dil))


def _rotate(y, tab):
    z = y * tab
    r = z + pltpu.roll(z, QK_ROPE, axis=1)
    lane = lax.broadcasted_iota(jnp.int32, r.shape, 1)
    return jnp.where(lane < QK_ROPE, r, 0.0)


def _mla_project(c_q, c_kv, k_rope, tab, gq_ref, gkv_ref, wq_ref, wkv_ref, q_ref, kt_ref, v_ref):
    cq = _rms(c_q, gq_ref[...]).astype(BF16)
    ckv = _rms(c_kv, gkv_ref[...]).astype(BF16)
    kr_t = _rotate(k_rope, tab).T.astype(BF16)
    for h in range(B_HEADS):
        lo, mid, hi = 2 * h * HEAD, (2 * h + 1) * HEAD, (2 * h + 2) * HEAD
        q = _dot(cq, wq_ref[:, lo:hi])
        q_ref[:, lo:mid] = q[:, :HEAD].astype(BF16)
        q_ref[:, mid:hi] = _rotate(q[:, HEAD:], tab).astype(BF16)
        kv = _dot(ckv, wkv_ref[:, lo:hi])
        kt_ref[h, :HEAD, :] = kv[:, :HEAD].T.astype(BF16)
        kt_ref[h, HEAD:, :] = kr_t
        v_ref[:, h * HEAD:(h + 1) * HEAD] = kv[:, HEAD:].astype(BF16)


def _mla_attn_kernel(q_ref, kt_ref, v_ref, o_ref, m_scr, a_scr, acc_scr, s_scr, p_scr, *, tq, tk, nk):
    c = (HEAD + QK_ROPE) ** -0.5 * math.log2(math.e)
    nl = tk // LANES
    ones = jnp.ones((tk, HEAD), BF16)

    def chunk(kk):
        return pl.ds(pl.multiple_of(kk * tk, tk), tk)

    def scores(kk, slot):
        kw = kt_ref.shape[-1]
        for u in range(tk // kw):
            s_scr[slot, :, u * kw:(u + 1) * kw] = _dot(q_ref[...], kt_ref[(tk // kw) * kk + u])

    def softmax(slot, first):
        s = s_scr[slot]
        mx = s[:, :LANES]
        for j in range(1, nl):
            mx = jnp.maximum(mx, s[:, j * LANES:(j + 1) * LANES])
        m_new = jnp.broadcast_to(jnp.max(mx, axis=1, keepdims=True), (tq, LANES))
        if not first:
            m_old = m_scr[...]
            m_new = jnp.maximum(m_old, m_new)
            a_scr[slot] = jnp.exp2((m_old - m_new) * c)
        for j in range(nl):
            cols = slice(j * LANES, (j + 1) * LANES)
            p_scr[slot, :, cols] = jnp.exp2((s[:, cols] - m_new) * c).astype(BF16)
        m_scr[...] = m_new

    def values(kk, slot, first):
        pv = _dot(p_scr[slot], jnp.concatenate([v_ref[chunk(kk), :], ones], axis=1))
        if first:
            acc_scr[...] = pv
        else:
            alpha = a_scr[slot]
            acc_scr[...] = jnp.concatenate([alpha, alpha], axis=1) * acc_scr[...] + pv

    scores(0, 0)
    softmax(0, True)
    scores(1, 1)
    softmax(1, False)
    values(0, 0, True)
    scores(2, 0)

    pairs = next(n for n in (6, 3, 2, 1) if (nk - 4) % (2 * n) == 0)

    def body(g, carry):
        for u in range(pairs):
            kk = 2 * (pairs * g + u + 1)
            softmax(0, False)
            values(kk - 1, 1, False)
            scores(kk + 1, 1)
            softmax(1, False)
            values(kk, 0, False)
            scores(kk + 2, 0)
        return carry

    lax.fori_loop(0, (nk - 4) // (2 * pairs), body, 0)
    softmax(0, False)
    values(nk - 3, 1, False)
    scores(nk - 1, 1)
    softmax(1, False)
    values(nk - 2, 0, False)
    values(nk - 1, 1, False)
    o_ref[...] = acc_scr[:, :HEAD] / acc_scr[:, HEAD:]


def _mla_attn(q, kt, v, *, b, s, tq=1024):
    t = b * s
    nq = s // tq
    tk = 512
    kw = kt.shape[-1]
    nk = s // tk
    assert nk % 2 == 0 and nk >= 4
    return pl.pallas_call(
        functools.partial(_mla_attn_kernel, tq=tq, tk=tk, nk=nk),
        out_shape=jax.ShapeDtypeStruct((t, B_HEADS * HEAD), F32),
        grid=(b, B_HEADS, nq),
        in_specs=[
            pl.BlockSpec((tq, 2 * HEAD), lambda bi, h, i: (bi * nq + i, h)),
            pl.BlockSpec((None, s // kw, 2 * HEAD, kw), lambda bi, h, i: (h, bi, 0, 0)),
            pl.BlockSpec((s, HEAD), lambda bi, h, i: (bi, h)),
        ],
        out_specs=pl.BlockSpec((tq, HEAD), lambda bi, h, i: (bi * nq + i, h)),
        scratch_shapes=[pltpu.VMEM((tq, LANES), F32), pltpu.VMEM((2, tq, LANES), F32),
                        pltpu.VMEM((tq, 2 * HEAD), F32), pltpu.VMEM((2, tq, tk), F32),
                        pltpu.VMEM((2, tq, tk), BF16)],
        compiler_params=_params("parallel", "parallel", "arbitrary"),
        name="mla_attn",
    )(q, kt, v)


def _softplus(x):
    return jnp.maximum(x, 0.0) + jnp.log1p(jnp.exp(-jnp.abs(x)))


def _scan_tile(a, b, row, reverse):
    for sh in (1, 2, 4):
        if reverse:
            keep = row < SUBLANES - sh
            a_s = pltpu.roll(a, SUBLANES - sh, axis=0)
            b_s = pltpu.roll(b, SUBLANES - sh, axis=0)
        else:
            keep = row >= sh
            a_s = pltpu.roll(a, sh, axis=0)
            b_s = pltpu.roll(b, sh, axis=0)
        b = b + a * jnp.where(keep, b_s, 0.0)
        a = a * jnp.where(keep, a_s, 1.0)
    return a, b


def _rglru_fwd_kernel(uc_ref, up_ref, un_ref, cw_ref, cb_ref, wg_ref, bg_ref, lam_ref,
                      hf_ref, ab_ref, bb_ref, ubuf, af_scr, bf_scr, carry, *, ts):
    i = pl.program_id(1)
    ns = pl.num_programs(1)
    ubuf[0:SUBLANES] = jnp.where(i > 0, up_ref[...], 0.0)
    ubuf[SUBLANES:SUBLANES + ts] = uc_ref[...]
    ubuf[SUBLANES + ts:] = jnp.where(i < ns - 1, un_ref[...], 0.0)
    u = cb_ref[...] + cw_ref[0:1, :] * ubuf[SUBLANES - 2:SUBLANES - 2 + ts]
    u = u + cw_ref[1:2, :] * ubuf[SUBLANES - 1:SUBLANES - 1 + ts]
    u = u + cw_ref[2:3, :] * ubuf[SUBLANES:SUBLANES + ts]
    u = u + cw_ref[3:4, :] * ubuf[SUBLANES + 1:SUBLANES + 1 + ts]
    gates = _dot(u.astype(BF16), wg_ref[...]) + bg_ref[...]
    sp = _softplus(-lam_ref[...])
    for d in range(2):
        r = jax.nn.sigmoid(gates[:, (2 * d) * C_WIDTH:(2 * d + 1) * C_WIDTH])
        ig = jax.nn.sigmoid(gates[:, (2 * d + 1) * C_WIDTH:(2 * d + 2) * C_WIDTH])
        log_a = -RG_C * r * sp[d:d + 1, :]
        a = jnp.exp(log_a)
        bterm = jnp.sqrt(-jnp.tanh(log_a) * (a * a + 1.0)) * (ig * u)
        if d == 0:
            af_scr[...] = a
            bf_scr[...] = bterm
        else:
            ab_ref[...] = a
            bb_ref[...] = bterm

    @pl.when(i == 0)
    def _():
        carry[...] = jnp.zeros_like(carry)

    row = lax.broadcasted_iota(jnp.int32, (SUBLANES, C_WIDTH), 0)

    def body(t, c):
        rows = pl.ds(pl.multiple_of(t * SUBLANES, SUBLANES), SUBLANES)
        a, b = _scan_tile(af_scr[rows, :], bf_scr[rows, :], row, reverse=False)
        h = b + a * carry[...]
        hf_ref[rows, :] = h
        carry[...] = h[SUBLANES - 1:SUBLANES, :]
        return c

    lax.fori_loop(0, ts // SUBLANES, body, 0, unroll=SCAN_UNROLL)


def _rglru_bwd_kernel(a_ref, b_ref, hf_ref, g_ref, y_ref, carry, *, ts):
    @pl.when(pl.program_id(1) == 0)
    def _():
        carry[...] = jnp.zeros_like(carry)

    row = lax.broadcasted_iota(jnp.int32, (SUBLANES, C_WIDTH), 0)
    nt = ts // SUBLANES

    def body(t, c):
        rows = pl.ds(pl.multiple_of((nt - 1 - t) * SUBLANES, SUBLANES), SUBLANES)
        a, b = _scan_tile(a_ref[rows, :], b_ref[rows, :], row, reverse=True)
        h = b + a * carry[...]
        carry[...] = h[0:1, :]
        y_ref[rows, :] = jax.nn.gelu(g_ref[rows, :]) * (hf_ref[rows, :] + h)
        return c

    lax.fori_loop(0, nt, body, 0, unroll=SCAN_UNROLL)


def _rglru(rest, conv_w, conv_b, w_gates, b_gates, lam, *, b, s, ts=1024):
    t = b * s
    ns = s // ts
    hb = ts // SUBLANES
    last = s // SUBLANES - 1
    u_col, g_col = 0, 1
    tile = lambda bi, i: (bi * ns + i, 0)
    hf, ab, bb = pl.pallas_call(
        functools.partial(_rglru_fwd_kernel, ts=ts),
        out_shape=tuple(jax.ShapeDtypeStruct((t, C_WIDTH), F32) for _ in range(3)),
        grid=(b, ns),
        in_specs=[
            pl.BlockSpec((ts, C_WIDTH), lambda bi, i: (bi * ns + i, u_col)),
            pl.BlockSpec((SUBLANES, C_WIDTH),
                         lambda bi, i: (bi * ns * hb + jnp.maximum(i * hb - 1, 0), u_col)),
            pl.BlockSpec((SUBLANES, C_WIDTH),
                         lambda bi, i: (bi * ns * hb + jnp.minimum((i + 1) * hb, last), u_col)),
            pl.BlockSpec((4, C_WIDTH), lambda bi, i: (0, 0)),
            pl.BlockSpec((1, C_WIDTH), lambda bi, i: (0, 0)),
            pl.BlockSpec((C_WIDTH, 4 * C_WIDTH), lambda bi, i: (0, 0)),
            pl.BlockSpec((1, 4 * C_WIDTH), lambda bi, i: (0, 0)),
            pl.BlockSpec((2, C_WIDTH), lambda bi, i: (0, 0)),
        ],
        out_specs=tuple(pl.BlockSpec((ts, C_WIDTH), tile) for _ in range(3)),
        scratch_shapes=[pltpu.VMEM((ts + 2 * SUBLANES, C_WIDTH), F32),
                        pltpu.VMEM((ts, C_WIDTH), F32), pltpu.VMEM((ts, C_WIDTH), F32),
                        pltpu.VMEM((1, C_WIDTH), F32)],
        compiler_params=_params("parallel", "arbitrary"),
        name="rglru_fwd",
    )(rest, rest, rest, conv_w, conv_b, w_gates, b_gates, lam)
    rev = lambda bi, i: (bi * ns + ns - 1 - i, 0)
    return pl.pallas_call(
        functools.partial(_rglru_bwd_kernel, ts=ts),
        out_shape=jax.ShapeDtypeStruct((t, C_WIDTH), F32),
        grid=(b, ns),
        in_specs=[
            pl.BlockSpec((ts, C_WIDTH), rev),
            pl.BlockSpec((ts, C_WIDTH), rev),
            pl.BlockSpec((ts, C_WIDTH), rev),
            pl.BlockSpec((ts, C_WIDTH), lambda bi, i: (bi * ns + ns - 1 - i, g_col)),
        ],
        out_specs=pl.BlockSpec((ts, C_WIDTH), rev),
        scratch_shapes=[pltpu.VMEM((1, C_WIDTH), F32)],
        compiler_params=_params("parallel", "arbitrary"),
        name="rglru_bwd",
    )(ab, bb, hf, rest)


def _outproj_xattn_kernel(x_ref, o1_ref, o2_ref, o3_ref, l1_ref, l2_ref, l3_ref, yb_ref, yc_ref,
                          ga_ref, gb_ref, gc_ref, w_ref, gx_ref, wxq_ref, kv_ref, wxo_ref, out_ref):
    l1, l2, l3 = l1_ref[...], l2_ref[...], l3_ref[...]
    m = jnp.maximum(jnp.maximum(l1, l2), l3)
    e1, e2, e3 = jnp.exp(l1 - m), jnp.exp(l2 - m), jnp.exp(l3 - m)
    z = e1 + e2 + e3
    w1, w2, w3 = e1 / z, e2 / z, e3 / z
    parts = []
    for h in range(A_HEADS):
        parts.append(w1[:, h:h + 1] * o1_ref[h] + w2[:, h:h + 1] * o2_ref[h] + w3[:, h:h + 1] * o3_ref[h])
    ya = jnp.concatenate(parts, axis=1)
    b0, c0 = A_WIDTH, 2 * A_WIDTH
    y = _dot(_rms(yb_ref[...], gb_ref[...]).astype(BF16), w_ref[b0:c0, :])
    y = y + _dot(_rms(yc_ref[...], gc_ref[...]).astype(BF16), w_ref[c0:, :])
    y = y + _dot(_rms(ya, ga_ref[...]).astype(BF16), w_ref[0:b0, :])
    out_ref[...] = _xattn_block(x_ref[...] + y, gx_ref, wxq_ref, kv_ref, wxo_ref)


def _outproj_xattn(x, o_branches, lse_branches, yb, yc, ga, gb, gc, w, gx, wxq, kv, wxo, *, s, n_mem, tm=512):
    t, d = x.shape
    per_seq = s // tm
    row = lambda n: pl.BlockSpec((tm, n), lambda i: (i, 0))
    full = lambda r, n: pl.BlockSpec((r, n), lambda i: (0, 0))
    heads = pl.BlockSpec((A_HEADS, tm, HEAD), lambda i: (0, i, 0))
    return pl.pallas_call(
        _outproj_xattn_kernel,
        out_shape=jax.ShapeDtypeStruct((t, d), F32),
        grid=(t // tm,),
        in_specs=[row(d), heads, heads, heads, row(LANES), row(LANES), row(LANES),
                  row(A_WIDTH), row(C_WIDTH), full(1, A_WIDTH), full(1, A_WIDTH), full(1, C_WIDTH),
                  full(d, d), full(1, d), full(d, X_WIDTH),
                  pl.BlockSpec((n_mem, 2 * X_WIDTH), lambda i: (i // per_seq, 0)), full(X_WIDTH, d)],
        out_specs=row(d),
        compiler_params=_params("parallel"),
        name="outproj_xattn",
    )(x, *o_branches, *lse_branches, yb, yc, ga, gb, gc, w, gx, wxq, kv, wxo)


def _mem_kv_kernel(m_ref, g_ref, w_ref, o_ref):
    o_ref[...] = _dot(_rms(m_ref[...], g_ref[...]).astype(BF16), w_ref[...]).astype(BF16)


def _mem_kv(mem, g, w_kv):
    t, d = mem.shape
    n = w_kv.shape[1]
    tm = 256
    return pl.pallas_call(
        _mem_kv_kernel,
        out_shape=jax.ShapeDtypeStruct((t, n), BF16),
        grid=(t // tm,),
        in_specs=[pl.BlockSpec((tm, d), lambda i: (i, 0)), pl.BlockSpec((1, d), lambda i: (0, 0)),
                  pl.BlockSpec((d, n), lambda i: (0, 0))],
        out_specs=pl.BlockSpec((tm, n), lambda i: (i, 0)),
        compiler_params=_params("parallel"),
        name="mem_kv",
    )(mem, g, w_kv)


def _xattn_block(x, g_ref, wq_ref, kv_ref, wo_ref):
    q = _dot(_rms(x, g_ref[...]).astype(BF16), wq_ref[...]).astype(BF16)
    c = HEAD ** -0.5 * LOG2E
    ones = jnp.ones((kv_ref.shape[0], HEAD), BF16)
    outs = []
    for h in range(X_HEADS):
        k = kv_ref[:, h * HEAD:(h + 1) * HEAD]
        v = kv_ref[:, X_WIDTH + h * HEAD:X_WIDTH + (h + 1) * HEAD]
        s = _dot_nt(q[:, h * HEAD:(h + 1) * HEAD], k)
        p = jnp.exp2((s - jnp.max(s, axis=1, keepdims=True)) * c).astype(BF16)
        pv = _dot(p, jnp.concatenate([v, ones], axis=1))
        outs.append(pv[:, :HEAD] / pv[:, HEAD:])
    o = jnp.concatenate(outs, axis=1).astype(BF16)
    return x + _dot(o, wo_ref[...])


def _cast_kernel(w_ref, o_ref):
    o_ref[...] = w_ref[...].astype(BF16)


def _layer_to_bf16(w, l):
    _, rows, cols = w.shape
    block_rows = next(c for c in (512, 256, 128, 64, 32, 16) if rows % c == 0 and c * cols * 4 <= (4 << 20))
    return pl.pallas_call(
        _cast_kernel,
        out_shape=jax.ShapeDtypeStruct((rows, cols), BF16),
        grid=(rows // block_rows,),
        in_specs=[pl.BlockSpec((None, block_rows, cols), lambda i: (l, i, 0))],
        out_specs=pl.BlockSpec((block_rows, cols), lambda i: (i, 0)),
        compiler_params=_params("parallel"),
        name="cast_bf16",
    )(w)


def _rope_table(s):
    inv = ROPE_THETA ** (-jnp.arange(0, QK_ROPE, 2, dtype=F32) / QK_ROPE)
    ang = jnp.arange(s, dtype=F32)[:, None] * inv[None, :]
    cos, sin = jnp.cos(ang), jnp.sin(ang)
    return jnp.concatenate([cos, cos, -sin, sin], axis=1)


def _swap_halves(w):
    half = w.shape[-1] // 2
    return jnp.concatenate([w[..., half:], w[..., :half]], axis=-1)


def _prep_layer(p, l):
    row = lambda v: v.reshape(1, -1)
    w_in = p['w_in'][l]
    qkv_end, cq_end, ckv_end, kr_end, u_end = 2304, 2816, 3328, 3392, 3904
    w_rope = w_in[:, ckv_end:kr_end]
    w_rest = jnp.concatenate([w_in[:, qkv_end:ckv_end], w_in[:, kr_end:], w_rope, _swap_halves(w_rope)], axis=1)
    wq = p['w_q_up'][l].reshape(-1, B_HEADS, HEAD + QK_ROPE)
    wq = jnp.concatenate([wq, _swap_halves(wq[..., HEAD:])], axis=-1).reshape(-1, B_HEADS * 2 * HEAD)
    eye = jnp.eye(C_BLOCKS, dtype=F32)
    dense = lambda w: jnp.einsum('ncd,nm->ncmd', w, eye).reshape(C_WIDTH, C_WIDTH)
    w_r, w_i = p['w_rg_r'][l], p['w_rg_i'][l]
    w_gates = jnp.concatenate([dense(w_r[0]), dense(w_i[0]), dense(w_r[1]), dense(w_i[1])], axis=1)
    b_r, b_i = p['b_rg_r'][l], p['b_rg_i'][l]
    b_gates = jnp.concatenate([b_r[0], b_i[0], b_r[1], b_i[1]]).reshape(1, -1)
    bf = lambda w: w.astype(BF16)
    big = lambda name: _layer_to_bf16(p[name], l)
    return dict(
        g_ffn1=row(p['g_ffn1'][l]), w1_gate=big('w1_gate'), w1_up=big('w1_up'), w1_down=big('w1_down'),
        g_mix=row(p['g_mix'][l]), w_qkv=bf(w_in[:, :qkv_end]), w_rest=bf(w_rest),
        g_q_lat=row(p['g_q_lat'][l]), g_kv_lat=row(p['g_kv_lat'][l]), w_q=bf(wq), w_kv=bf(p['w_kv_up'][l]),
        conv_w=p['conv_w'][l], conv_b=row(p['conv_b'][l]), w_gates=bf(w_gates), b_gates=b_gates,
        lam=p['rg_lambda'][l],
        g_out_a=row(p['g_out_a'][l]), g_out_b=row(p['g_out_b'][l]), g_out_c=row(p['g_out_c'][l]),
        w_out=big('w_out'),
        g_xattn=row(p['g_xattn'][l]), g_mem=row(p['g_mem'][l]), w_xq=bf(p['w_xq'][l]),
        w_xkv=bf(jnp.concatenate([p['w_xk'][l], p['w_xv'][l]], axis=1)), w_xo=bf(p['w_xo'][l]),
        g_ffn2=row(p['g_ffn2'][l]), w2_gate=big('w2_gate'), w2_up=big('w2_up'), w2_down=big('w2_down'),
    )


def _trunk(x, mem, layers, g_final):
    b, s, d = x.shape
    n_mem = mem.shape[1]
    x = x.reshape(b * s, d)
    mem = mem.reshape(b * n_mem, d)
    tab = _rope_table(s)
    for l, w in enumerate(layers):
        x = _ffn(x, w['g_ffn1'], w['w1_gate'], w['w1_up'], w['w1_down'], g_final, final_norm=False)
        *qkv_by_dil, rest, q, kt, v = _inproj(x, w['g_mix'], w['w_qkv'], w['w_rest'], tab, w['g_q_lat'],
                                              w['g_kv_lat'], w['w_q'], w['w_kv'], b=b, s=s)
        branches = [_dilated_branch(qkv, dil) for qkv, (_, dil) in zip(qkv_by_dil, DILATED_CONFIGS)]
        yb = _mla_attn(q, kt, v, b=b, s=s)
        yc = _rglru(rest, w['conv_w'], w['conv_b'], w['w_gates'], w['b_gates'], w['lam'], b=b, s=s)
        kv = _mem_kv(mem, w['g_mem'], w['w_xkv'])
        x = _outproj_xattn(x, [o for o, _ in branches], [lse for _, lse in branches], yb, yc,
                           w['g_out_a'], w['g_out_b'], w['g_out_c'], w['w_out'],
                           w['g_xattn'], w['w_xq'], kv, w['w_xo'], s=s, n_mem=n_mem)
        x = _ffn(x, w['g_ffn2'], w['w2_gate'], w['w2_up'], w['w2_down'], g_final,
                 final_norm=(l == len(layers) - 1))
    return x.reshape(b, s, d)


def kernel(x_prompt, x_sample, mem_prompt, mem_sample, g_ffn1, w1_gate, w1_up, w1_down, g_mix, w_in, g_q_lat, w_q_up, g_kv_lat, w_kv_up, conv_w, conv_b, w_rg_r, b_rg_r, w_rg_i, b_rg_i, rg_lambda, g_out_a, g_out_b, g_out_c, w_out, g_xattn, g_mem, w_xq, w_xk, w_xv, w_xo, g_ffn2, w2_gate, w2_up, w2_down, g_final):
    p = dict(g_ffn1=g_ffn1, w1_gate=w1_gate, w1_up=w1_up, w1_down=w1_down,
             g_mix=g_mix, w_in=w_in, g_q_lat=g_q_lat, w_q_up=w_q_up, g_kv_lat=g_kv_lat, w_kv_up=w_kv_up,
             conv_w=conv_w, conv_b=conv_b, w_rg_r=w_rg_r, b_rg_r=b_rg_r, w_rg_i=w_rg_i, b_rg_i=b_rg_i,
             rg_lambda=rg_lambda, g_out_a=g_out_a, g_out_b=g_out_b, g_out_c=g_out_c, w_out=w_out,
             g_xattn=g_xattn, g_mem=g_mem, w_xq=w_xq, w_xk=w_xk, w_xv=w_xv, w_xo=w_xo,
             g_ffn2=g_ffn2, w2_gate=w2_gate, w2_up=w2_up, w2_down=w2_down)
    layers = [_prep_layer(p, l) for l in range(g_ffn1.shape[0])]
    gf = g_final.reshape(1, -1)
    return (_trunk(x_prompt, mem_prompt, layers, gf), _trunk(x_sample, mem_sample, layers, gf))
```
